```python
import math
import jax, jax.numpy as jnp
from jax import lax
import numpy as np

D_MODEL = 4096
BATCH = 2
SEQ = 4096
DEPTH = 2

ATTN_HEADS = D_MODEL // 256
ATTN_HEAD_DIM = 128
KV_LATENT = 512
IDX_HEADS = D_MODEL // 128
IDX_HEAD_DIM = 128
IDX_TOPK = 256
Q_BLOCK = 128
REL_BUCKETS = 32
REL_MAX_DIST = 128
HGRN_HEADS = D_MODEL // 256
HGRN_KEY_DIM = 128
HGRN_VAL_DIM = 128
HGRN_CHUNK = 64
D_FF = 256 * ((8 * D_MODEL // 3 + 255) // 256)
CONV_WIDTH = 3
NORM_EPS = 1e-6

ATTN_WIDTH = ATTN_HEADS * ATTN_HEAD_DIM
HGRN_KEY_WIDTH = HGRN_HEADS * HGRN_KEY_DIM
HGRN_VAL_WIDTH = HGRN_HEADS * HGRN_VAL_DIM
IN_SPLITS = (ATTN_WIDTH, KV_LATENT, IDX_HEADS * IDX_HEAD_DIM, IDX_HEAD_DIM, IDX_HEADS,
             HGRN_KEY_WIDTH, HGRN_KEY_WIDTH, HGRN_VAL_WIDTH, HGRN_VAL_WIDTH, D_MODEL, D_MODEL)
D_IN = sum(IN_SPLITS)

kernel_name = "hybrid_dsa_hgrn2_gated_merge_convffn"


def rms_norm(x, g):
    xf = x.astype(jnp.float32)
    y = xf * lax.rsqrt(jnp.mean(xf * xf, axis=-1, keepdims=True) + NORM_EPS)
    return (y * g.astype(jnp.float32)).astype(x.dtype)


def layer_norm(x, g, b):
    xf = x.astype(jnp.float32)
    mu = jnp.mean(xf, axis=-1, keepdims=True)
    xc = xf - mu
    y = xc * lax.rsqrt(jnp.mean(xc * xc, axis=-1, keepdims=True) + NORM_EPS)
    return (y * g.astype(jnp.float32) + b.astype(jnp.float32)).astype(x.dtype)


def t5_bucket(dist):
    max_exact = REL_BUCKETS // 2
    d = jnp.maximum(dist, 0)
    df = jnp.maximum(d, 1).astype(jnp.float32)
    large = max_exact + (jnp.log(df / max_exact) / math.log(REL_MAX_DIST / max_exact)
                         * (REL_BUCKETS - max_exact)).astype(jnp.int32)
    large = jnp.minimum(large, REL_BUCKETS - 1)
    return jnp.where(d < max_exact, d, large)


def dsa_attention(q, c_kv, q_idx, k_idx, w_idx, rel_bias, kv_norm_g, w_uk, w_uv):
    B, T = q.shape[0], q.shape[1]
    f32 = jnp.float32
    c = rms_norm(c_kv, kv_norm_g)
    k = c @ w_uk
    v = c @ w_uv
    topk = min(IDX_TOPK, T // 4)
    nb = T // Q_BLOCK
    key_pos = jnp.arange(T)
    scale = ATTN_HEAD_DIM ** -0.5
    k_idx32 = k_idx.astype(f32)

    def to_blocks(a):
        return jnp.moveaxis(a.reshape((B, nb, Q_BLOCK) + a.shape[2:]), 1, 0)

    def block(args):
        qb, qib, wib, start = args
        q_pos = start + jnp.arange(Q_BLOCK)
        causal = key_pos[None, :] <= q_pos[:, None]
        s = jnp.einsum('bqhd,bsd->bqhs', qib.astype(f32), k_idx32)
        idx_score = jnp.einsum('bqhs,bqh->bqs', jax.nn.relu(s), wib.astype(f32))
        idx_score = jnp.where(causal[None], idx_score, -jnp.inf)
        _, sel = lax.top_k(idx_score, topk)
        k_sel = jax.vmap(lambda kb, ib: kb[ib])(k, sel)
        v_sel = jax.vmap(lambda vb, ib: vb[ib])(v, sel)
        dist = q_pos[None, :, None] - sel
        valid = dist >= 0
        bias = jnp.transpose(rel_bias[t5_bucket(dist)], (0, 1, 3, 2)).astype(f32)
        logits = jnp.einsum('bqhd,bqkd->bqhk', qb.astype(f32), k_sel.astype(f32)) * scale + bias
        logits = jnp.where(valid[:, :, None, :], logits, -jnp.inf)
        p = jax.nn.softmax(logits, axis=-1)
        o = jnp.einsum('bqhk,bqkd->bqhd', p, v_sel.astype(f32))
        return o.astype(q.dtype)

    starts = jnp.arange(nb) * Q_BLOCK
    o = lax.map(block, (to_blocks(q), to_blocks(q_idx), to_blocks(w_idx), starts))
    return jnp.moveaxis(o, 0, 1).reshape(B, T, ATTN_WIDTH)


def hgrn2(q, f_logit, i, g, lb, norm_g):
    B, T = q.shape[0], q.shape[1]
    f32 = jnp.float32
    C = HGRN_CHUNK
    nc = T // C
    lbf = lb.astype(f32)
    zf = f_logit.astype(f32)
    qf = jax.nn.silu(q.astype(f32))
    log_f = jnp.log(lbf + (1.0 - lbf) * jax.nn.sigmoid(zf))
    kf = (1.0 - lbf) * jax.nn.sigmoid(-zf)
    vf = i.astype(f32)

    def chunks(a, d):
        return a.reshape(B, nc, C, HGRN_HEADS, d).transpose(1, 0, 3, 2, 4)

    tri = jnp.tril(jnp.ones((C, C), dtype=bool))

    def step(S, inp):
        qc, kc, vc, lfc = inp
        A = jnp.cumsum(lfc, axis=2)
        decay = jnp.exp(jnp.where(tri[:, :, None],
                                  A[:, :, :, None, :] - A[:, :, None, :, :], -jnp.inf))
        P = jnp.einsum('bhtk,bhsk,bhtsk->bhts', qc, kc, decay)
        o = (jnp.einsum('bhts,bhsv->bhtv', P, vc)
             + jnp.einsum('bhtk,bhkv->bhtv', qc * jnp.exp(A), S))
        A_last = A[:, :, -1:, :]
        S = (jnp.exp(A_last[:, :, 0, :])[..., None] * S
             + jnp.einsum('bhsk,bhsv->bhkv', kc * jnp.exp(A_last - A), vc))
        return S, o

    S0 = jnp.zeros((B, HGRN_HEADS, HGRN_KEY_DIM, HGRN_VAL_DIM), f32)
    _, o = lax.scan(step, S0, (chunks(qf, HGRN_KEY_DIM), chunks(kf, HGRN_KEY_DIM),
                               chunks(vf, HGRN_VAL_DIM), chunks(log_f, HGRN_KEY_DIM)))
    o = o.transpose(1, 0, 3, 2, 4).reshape(B, T, HGRN_HEADS, HGRN_VAL_DIM)
    o = rms_norm(o, norm_g)
    gate = jax.nn.silu(g.astype(f32)).reshape(B, T, HGRN_HEADS, HGRN_VAL_DIM)
    return (o * gate).reshape(B, T, HGRN_VAL_WIDTH).astype(q.dtype)


def hybrid_mixer(h, lb, rel_bias, w_in, kv_norm_g, w_uk, w_uv, idx_k_ln_g, idx_k_ln_b,
                 hgrn_norm_g, w_proj_attn, w_proj_hgrn, w_out):
    B, T = h.shape[0], h.shape[1]
    proj = h @ w_in
    points = np.cumsum(IN_SPLITS)[:-1].tolist()
    (q_a, c_kv, q_i, k_i, w_i, q_h, f_h, i_h, g_h, gate_a, gate_b) = jnp.split(proj, points, axis=-1)
    q_a = q_a.reshape(B, T, ATTN_HEADS, ATTN_HEAD_DIM)
    q_i = q_i.reshape(B, T, IDX_HEADS, IDX_HEAD_DIM)
    k_i = layer_norm(k_i, idx_k_ln_g, idx_k_ln_b)
    w_i = w_i * (IDX_HEADS ** -0.5 * IDX_HEAD_DIM ** -0.5)
    y_a = dsa_attention(q_a, c_kv, q_i, k_i, w_i, rel_bias, kv_norm_g, w_uk, w_uv) @ w_proj_attn
    y_b = hgrn2(q_h, f_h, i_h, g_h, lb, hgrn_norm_g) @ w_proj_hgrn
    merged = jax.nn.sigmoid(gate_a) * y_a + jax.nn.sigmoid(gate_b) * y_b
    return merged @ w_out


def conv_ffn(h, w_up, conv_w, conv_b, w_down):
    T = h.shape[1]
    u = h @ w_up
    up = jnp.pad(u, ((0, 0), (CONV_WIDTH - 1, 0), (0, 0)))
    c = conv_b
    for j in range(CONV_WIDTH):
        c = c + conv_w[j] * up[:, j:j + T]
    gate, val = jnp.split(c, 2, axis=-1)
    return (jax.nn.gelu(gate, approximate=True) * val) @ w_down


def setup_inputs(seed: int = 0) -> dict:
    key = jax.random.key(seed)
    ks = jax.random.split(key, 24)
    f32 = jnp.float32
    L = DEPTH

    def dense(k, shape, fan_in):
        return jax.random.normal(k, shape, f32) * fan_in ** -0.5

    def gain(k, shape):
        return 1.0 + 0.05 * jax.random.normal(k, shape, f32)

    return {
        "x": jax.random.normal(ks[0], (BATCH, SEQ, D_MODEL), f32),
        "rel_bias": 0.5 * jax.random.normal(ks[1], (REL_BUCKETS, ATTN_HEADS), f32),
        "hgrn_lb_logits": jax.random.normal(ks[2], (L, HGRN_KEY_WIDTH), f32),
        "mix_pre_g": gain(ks[3], (L, D_MODEL)),
        "mix_post_g": gain(ks[4], (L, D_MODEL)),
        "w_in": dense(ks[5], (L, D_MODEL, D_IN), D_MODEL),
        "kv_norm_g": gain(ks[6], (L, KV_LATENT)),
        "w_uk": dense(ks[7], (L, KV_LATENT, ATTN_HEAD_DIM), KV_LATENT),
        "w_uv": dense(ks[8], (L, KV_LATENT, ATTN_HEAD_DIM), KV_LATENT),
        "idx_k_ln_g": gain(ks[9], (L, IDX_HEAD_DIM)),
        "idx_k_ln_b": 0.02 * jax.random.normal(ks[10], (L, IDX_HEAD_DIM), f32),
        "hgrn_norm_g": gain(ks[11], (L, HGRN_VAL_DIM)),
        "w_proj_attn": dense(ks[12], (L, ATTN_WIDTH, D_MODEL), ATTN_WIDTH),
        "w_proj_hgrn": dense(ks[13], (L, HGRN_VAL_WIDTH, D_MODEL), HGRN_VAL_WIDTH),
        "w_out": dense(ks[14], (L, D_MODEL, D_MODEL), D_MODEL),
        "ffn_pre_g": gain(ks[15], (L, D_MODEL)),
        "ffn_post_g": gain(ks[16], (L, D_MODEL)),
        "w_up": dense(ks[17], (L, D_MODEL, 2 * D_FF), D_MODEL),
        "conv_w": dense(ks[18], (L, CONV_WIDTH, 2 * D_FF), CONV_WIDTH),
        "conv_b": 0.02 * jax.random.normal(ks[19], (L, 2 * D_FF), f32),
        "w_down": dense(ks[20], (L, D_FF, D_MODEL), D_FF),
    }


def reference(x, rel_bias, hgrn_lb_logits, mix_pre_g, mix_post_g, w_in, kv_norm_g, w_uk, w_uv,
              idx_k_ln_g, idx_k_ln_b, hgrn_norm_g, w_proj_attn, w_proj_hgrn, w_out,
              ffn_pre_g, ffn_post_g, w_up, conv_w, conv_b, w_down):
    lb_p = jax.nn.softmax(hgrn_lb_logits.astype(jnp.float32), axis=0)
    lb_all = jnp.cumsum(lb_p, axis=0) - lb_p[0:1]
    for l in range(DEPTH):
        h = rms_norm(x, mix_pre_g[l])
        y = hybrid_mixer(h, lb_all[l], rel_bias, w_in[l], kv_norm_g[l], w_uk[l], w_uv[l],
                         idx_k_ln_g[l], idx_k_ln_b[l], hgrn_norm_g[l],
                         w_proj_attn[l], w_proj_hgrn[l], w_out[l])
        x = x + rms_norm(y, mix_post_g[l])
        h = rms_norm(x, ffn_pre_g[l])
        y = conv_ffn(h, w_up[l], conv_w[l], conv_b[l], w_down[l])
        x = x + rms_norm(y, ffn_post_g[l])
    return x
```

```python
import functools
import math

import jax
import jax.numpy as jnp
import numpy as np
from jax import lax
from jax.experimental import pallas as pl
from jax.experimental.pallas import tpu as pltpu

F32 = jnp.float32
BF16 = jnp.bfloat16

NORM_EPS = 1e-6
HEAD_DIM = 128
IDX_TOPK = 256
REL_BUCKETS = 32
REL_MAX_DIST = 128
CONV_WIDTH = 3
LANES = 128
VMEM_LIMIT_BYTES = 56 * 1024 * 1024

ATT_BLOCK = 256
HGRN_CHUNK = 64
HGRN_SUB = 16
INT_MIN = -2 ** 31
NEG_BIG = -1e30


def _cparams(sem):
    return pltpu.CompilerParams(dimension_semantics=sem, vmem_limit_bytes=VMEM_LIMIT_BYTES)


def _dot(a, b):
    return jnp.dot(a, b, preferred_element_type=F32)


def _dot_nt(a, b):
    return lax.dot_general(a, b, (((1,), (1,)), ((), ())), preferred_element_type=F32)


def _dot_tn(a, b):
    return lax.dot_general(a, b, (((0,), (0,)), ((), ())), preferred_element_type=F32)


def _rmsnorm_kernel(x_ref, g_ref, o_ref):
    x = x_ref[...]
    r = lax.rsqrt(jnp.mean(x * x, axis=-1, keepdims=True) + NORM_EPS)
    o_ref[...] = (x * r * g_ref[...]).astype(o_ref.dtype)


def rmsnorm_cast(x, g, rows=256):
    n, d = x.shape
    return pl.pallas_call(
        _rmsnorm_kernel,
        grid=(n // rows,),
        in_specs=[pl.BlockSpec((rows, d), lambda i: (i, 0)),
                  pl.BlockSpec((1, d), lambda i: (0, 0))],
        out_specs=pl.BlockSpec((rows, d), lambda i: (i, 0)),
        out_shape=jax.ShapeDtypeStruct((n, d), BF16),
        compiler_params=_cparams(("parallel",)),
        name="rmsnorm_cast",
    )(x, g.reshape(1, d))


def _mm_kernel(x_ref, w_ref, o_ref):
    o_ref[...] = _dot(x_ref[...], w_ref[...]).astype(o_ref.dtype)


def _mm_acc_kernel(x_ref, w_ref, o_ref, acc_ref, *, nk):
    k = pl.program_id(2)

    @pl.when(k == 0)
    def _():
        acc_ref[...] = jnp.zeros_like(acc_ref)

    acc_ref[...] += _dot(x_ref[...], w_ref[...])

    @pl.when(k == nk - 1)
    def _():
        o_ref[...] = acc_ref[...].astype(o_ref.dtype)


def matmul(x, w, tm, tn, tk=None, out_dtype=BF16, name="matmul"):
    m, kdim = x.shape
    n = w.shape[1]
    tk = kdim if tk is None else tk
    nk = kdim // tk
    assert m % tm == 0 and n % tn == 0 and kdim % tk == 0
    if nk == 1:
        return pl.pallas_call(
            _mm_kernel,
            grid=(m // tm, n // tn),
            in_specs=[pl.BlockSpec((tm, kdim), lambda i, j: (i, 0)),
                      pl.BlockSpec((kdim, tn), lambda i, j: (0, j))],
            out_specs=pl.BlockSpec((tm, tn), lambda i, j: (i, j)),
            out_shape=jax.ShapeDtypeStruct((m, n), out_dtype),
            compiler_params=_cparams(("parallel", "parallel")),
            name=name,
        )(x, w)
    return pl.pallas_call(
        functools.partial(_mm_acc_kernel, nk=nk),
        grid=(m // tm, n // tn, nk),
        in_specs=[pl.BlockSpec((tm, tk), lambda i, j, k: (i, k)),
                  pl.BlockSpec((tk, tn), lambda i, j, k: (k, j))],
        out_specs=pl.BlockSpec((tm, tn), lambda i, j, k: (i, j)),
        out_shape=jax.ShapeDtypeStruct((m, n), out_dtype),
        scratch_shapes=[pltpu.VMEM((tm, tn), F32)],
        compiler_params=_cparams(("parallel", "parallel", "arbitrary")),
        name=name,
    )(x, w)


def _dsa_prep_kernel(ckv_ref, kw_ref, kvg_ref, wuk_ref, wuv_ref, lng_ref, lnb_ref,
                     k_ref, v_ref, kidx_ref, wi_ref, *, k_scale, w_scale):
    c = ckv_ref[...].astype(F32)
    c = c * lax.rsqrt(jnp.mean(c * c, axis=-1, keepdims=True) + NORM_EPS) * kvg_ref[...]
    cb = c.astype(BF16)
    k_ref[...] = (_dot(cb, wuk_ref[...]) * k_scale).astype(k_ref.dtype)
    v_ref[...] = _dot(cb, wuv_ref[...]).astype(v_ref.dtype)
    kw = kw_ref[...].astype(F32)
    ki = kw[:, :HEAD_DIM]
    mu = jnp.mean(ki, axis=-1, keepdims=True)
    kc = ki - mu
    kn = kc * lax.rsqrt(jnp.mean(kc * kc, axis=-1, keepdims=True) + NORM_EPS)
    kidx_ref[...] = (kn * lng_ref[...] + lnb_ref[...]).astype(kidx_ref.dtype)
    wi_ref[...] = kw[:, HEAD_DIM:] * w_scale


def dsa_prep(proj, lay, kv_norm_g, w_uk, w_uv, ln_g, ln_b, idx_heads, rows=512):
    n = proj.shape[0]
    kvl = w_uk.shape[0]
    row = lambda i: (i, 0)
    const = lambda i: (0, 0)
    out = jax.ShapeDtypeStruct((n, HEAD_DIM), BF16)
    return pl.pallas_call(
        functools.partial(_dsa_prep_kernel, k_scale=HEAD_DIM ** -0.5,
                          w_scale=idx_heads ** -0.5 * HEAD_DIM ** -0.5),
        grid=(n // rows,),
        in_specs=[pl.BlockSpec((rows, kvl), lambda i: (i, lay["c_kv"] // kvl)),
                  pl.BlockSpec((rows, 2 * HEAD_DIM), lambda i: (i, lay["k_i"] // (2 * HEAD_DIM))),
                  pl.BlockSpec((1, kvl), const),
                  pl.BlockSpec((kvl, HEAD_DIM), const),
                  pl.BlockSpec((kvl, HEAD_DIM), const),
                  pl.BlockSpec((1, HEAD_DIM), const),
                  pl.BlockSpec((1, HEAD_DIM), const)],
        out_specs=[pl.BlockSpec((rows, HEAD_DIM), row)] * 4,
        out_shape=[out, out, out, jax.ShapeDtypeStruct((n, HEAD_DIM), F32)],
        compiler_params=_cparams(("parallel",)),
        name="dsa_prep",
    )(proj, proj, kv_norm_g.reshape(1, kvl), w_uk.astype(BF16), w_uv.astype(BF16),
      ln_g.reshape(1, HEAD_DIM), ln_b.reshape(1, HEAD_DIM))


def _sortable_key(x):
    u = lax.bitcast_convert_type(x, jnp.int32)
    return u ^ ((u >> 31) & jnp.int32(0x7FFFFFFF))


def _dsa_kernel(qa_ref, qi_ref, wi_ref, kidx_ref, k_ref, v_ref, bias_ref, o_ref,
                qas_ref, qis_ref, keys_ref, m_ref, l_ref, acc_ref, *, heads, idx_heads, topk):
    qb = ATT_BLOCK
    ck = ATT_BLOCK
    j = pl.program_id(1)

    for h in range(heads):
        qas_ref[h * qb:(h + 1) * qb, :] = qa_ref[:, h * HEAD_DIM:(h + 1) * HEAD_DIM]
    for h in range(idx_heads):
        qis_ref[h * qb:(h + 1) * qb, :] = qi_ref[:, h * HEAD_DIM:(h + 1) * HEAD_DIM]

    row_pos = j * qb + lax.broadcasted_iota(jnp.int32, (qb, ck), 0)
    lane = lax.broadcasted_iota(jnp.int32, (qb, ck), 1)
    hg = 8 if idx_heads % 8 == 0 else idx_heads

    def score_chunk(c, carry):
        kc = kidx_ref[pl.ds(pl.multiple_of(c * ck, ck), ck), :]
        acc = jnp.zeros((qb, ck), F32)
        for g in range(idx_heads // hg):
            s = _dot_nt(qis_ref[g * hg * qb:(g + 1) * hg * qb, :], kc)
            for hh in range(hg):
                h = g * hg + hh
                acc = acc + jnp.maximum(s[hh * qb:(hh + 1) * qb, :], 0.0) * wi_ref[:, h:h + 1]
        valid = (c * ck + lane) <= row_pos
        keys_ref[c] = jnp.where(valid, _sortable_key(acc), jnp.int32(INT_MIN))
        return carry

    lax.fori_loop(0, j + 1, score_chunk, 0)

    def bit_step(i, lo):
        cand = lo + lax.shift_left(jnp.int32(1), jnp.int32(31) - i)

        def count_chunk(c, part):
            ge = (keys_ref[c] >= cand).astype(jnp.int32)
            for t in range(ck // LANES):
                part = part + ge[:, t * LANES:(t + 1) * LANES]
            return part

        part = lax.fori_loop(0, j + 1, count_chunk, jnp.zeros((qb, LANES), jnp.int32))
        cnt = jnp.sum(part, axis=-1, keepdims=True)
        return jnp.where(cnt >= topk, cand, lo)

    lo = lax.fori_loop(0, 32, bit_step, jnp.full((qb, 1), INT_MIN, jnp.int32))
    thr = jnp.maximum(lo, jnp.int32(INT_MIN + 1))

    m_ref[...] = jnp.full_like(m_ref, NEG_BIG)
    l_ref[...] = jnp.zeros_like(l_ref)
    acc_ref[...] = jnp.zeros_like(acc_ref)

    def attend(c, bias_idx):
        off = pl.multiple_of(c * ck, ck)
        kc = k_ref[pl.ds(off, ck), :]
        vc = v_ref[pl.ds(off, ck), :]
        sel = keys_ref[c] >= thr
        for h in range(heads):
            rows = slice(h * qb, (h + 1) * qb)
            s = _dot_nt(qas_ref[rows, :], kc)
            if bias_idx is not None:
                s = s + bias_ref[bias_idx, rows, :]
            s = jnp.where(sel, s, NEG_BIG)
            m_prev = m_ref[rows, :]
            m_new = jnp.maximum(m_prev, jnp.max(s, axis=-1, keepdims=True))
            p = jnp.where(sel, jnp.exp(s - m_new), 0.0)
            alpha = jnp.exp(m_prev - m_new)
            l_ref[rows, :] = alpha * l_ref[rows, :] + jnp.sum(p, axis=-1, keepdims=True)
            acc_ref[rows, :] = alpha * acc_ref[rows, :] + _dot(p.astype(BF16), vc)
            m_ref[rows, :] = m_new

    def far_chunk(c, carry):
        attend(c, None)
        return carry

    lax.fori_loop(0, j - 1, far_chunk, 0)

    @pl.when(j >= 1)
    def _():
        attend(j - 1, 1)

    attend(j, 0)

    for h in range(heads):
        rows = slice(h * qb, (h + 1) * qb)
        o = acc_ref[rows, :] * (1.0 / l_ref[rows, :])
        o_ref[:, h * HEAD_DIM:(h + 1) * HEAD_DIM] = o.astype(o_ref.dtype)


def dsa_attention(proj, lay, wi, kidx, k, v, bias_tiles, batch, seq, heads, idx_heads):
    qb = ATT_BLOCK
    nqb = seq // qb
    aw = heads * HEAD_DIM
    iw = idx_heads * HEAD_DIM
    topk = min(IDX_TOPK, seq // 4)
    return pl.pallas_call(
        functools.partial(_dsa_kernel, heads=heads, idx_heads=idx_heads, topk=topk),
        grid=(batch, nqb),
        in_specs=[pl.BlockSpec((qb, aw), lambda b, j: (b * nqb + j, lay["q_a"] // aw)),
                  pl.BlockSpec((qb, iw), lambda b, j: (b * nqb + j, lay["q_i"] // iw)),
                  pl.BlockSpec((qb, HEAD_DIM), lambda b, j: (b * nqb + j, 0)),
                  pl.BlockSpec((seq, HEAD_DIM), lambda b, j: (b, 0)),
                  pl.BlockSpec((seq, HEAD_DIM), lambda b, j: (b, 0)),
                  pl.BlockSpec((seq, HEAD_DIM), lambda b, j: (b, 0)),
                  pl.BlockSpec((2, heads * qb, qb), lambda b, j: (0, 0, 0),
                               pipeline_mode=pl.Buffered(1))],
        out_specs=pl.BlockSpec((qb, aw), lambda b, j: (b * nqb + j, 0)),
        out_shape=jax.ShapeDtypeStruct((batch * seq, aw), BF16),
        scratch_shapes=[pltpu.VMEM((heads * qb, HEAD_DIM), BF16),
                        pltpu.VMEM((idx_heads * qb, HEAD_DIM), BF16),
                        pltpu.VMEM((seq // qb, qb, qb), jnp.int32),
                        pltpu.VMEM((heads * qb, 1), F32),
                        pltpu.VMEM((heads * qb, 1), F32),
                        pltpu.VMEM((heads * qb, HEAD_DIM), F32)],
        compiler_params=_cparams(("parallel", "arbitrary")),
        name="dsa_attention",
    )(proj, proj, wi, kidx, k, v, bias_tiles)


def _t5_bucket(dist):
    max_exact = REL_BUCKETS // 2
    d = jnp.maximum(dist, 0)
    df = jnp.maximum(d, 1).astype(F32)
    large = max_exact + (jnp.log(df / max_exact) / math.log(REL_MAX_DIST / max_exact)
                         * (REL_BUCKETS - max_exact)).astype(jnp.int32)
    large = jnp.minimum(large, REL_BUCKETS - 1)
    return jnp.where(d < max_exact, d, large)


def rel_bias_tiles(rel_bias, heads):
    qb = ATT_BLOCK
    assert _static_far_bucket(qb) == REL_BUCKETS - 1
    i = jnp.arange(qb)[:, None]
    r = jnp.arange(qb)[None, :]
    tiles = []
    for off in (0, qb):
        b = rel_bias[_t5_bucket(off + i - r)]
        b = b - rel_bias[REL_BUCKETS - 1][None, None, :]
        tiles.append(jnp.transpose(b, (2, 0, 1)).reshape(heads * qb, qb))
    return jnp.stack(tiles).astype(F32)


def _static_far_bucket(d):
    max_exact = REL_BUCKETS // 2
    v = max_exact + int(math.log(d / max_exact) / math.log(REL_MAX_DIST / max_exact)
                        * (REL_BUCKETS - max_exact))
    return min(v, REL_BUCKETS - 1)


def _hgrn_kernel(q_ref, f_ref, i_ref, g_ref, lb_ref, ng_ref, o_ref, st_ref, *, heads_per_step, tb):
    c = HGRN_CHUNK
    sub = HGRN_SUB
    nsub = c // sub

    @pl.when(pl.program_id(2) == 0)
    def _():
        st_ref[...] = jnp.zeros_like(st_ref)

    r_io = lax.broadcasted_iota(jnp.int32, (c, c), 0)
    c_io = lax.broadcasted_iota(jnp.int32, (c, c), 1)
    tri = (c_io <= r_io).astype(F32)
    off_mask = c_io < (r_io // sub) * sub
    sub_r = lax.broadcasted_iota(jnp.int32, (sub, c), 0)
    sub_c = lax.broadcasted_iota(jnp.int32, (sub, c), 1)
    ng = ng_ref[...]

    def chunk(ci, carry):
        r0 = pl.multiple_of(ci * c, c)
        for hh in range(heads_per_step):
            cols = slice(hh * HEAD_DIM, (hh + 1) * HEAD_DIM)
            lb = lb_ref[:, cols]
            z = f_ref[pl.ds(r0, c), cols].astype(F32)
            qx = q_ref[pl.ds(r0, c), cols].astype(F32)
            v = i_ref[pl.ds(r0, c), cols].astype(F32)
            gx = g_ref[pl.ds(r0, c), cols].astype(F32)
            sig = jax.nn.sigmoid(z)
            lf = jnp.log(lb + (1.0 - lb) * sig)
            kf = (1.0 - lb) * jax.nn.sigmoid(-z)
            qf = qx * jax.nn.sigmoid(qx)
            a = jnp.dot(tri, lf, preferred_element_type=F32, precision=lax.Precision.HIGHEST)
            vb = v.astype(BF16)

            p_rows = []
            for sb in range(nsub):
                rs = slice(sb * sub, (sb + 1) * sub)
                a_blk = a[rs, :]
                q_blk = qf[rs, :]
                a_start = a[sb * sub - 1:sb * sub, :] if sb > 0 else jnp.zeros((1, HEAD_DIM), F32)
                q_t = q_blk * jnp.exp(a_blk - a_start)
                k_t = kf * jnp.exp(jnp.minimum(a_start - a, 0.0))
                p_off = _dot_nt(q_t.astype(BF16), k_t.astype(BF16))
                p_diag = jnp.zeros((sub, c), F32)
                t_io = lax.broadcasted_iota(jnp.int32, (sub, HEAD_DIM), 0)
                for s in range(sub):
                    a_s = a[sb * sub + s:sb * sub + s + 1, :]
                    k_s = kf[sb * sub + s:sb * sub + s + 1, :]
                    e = jnp.exp(jnp.minimum(a_blk - a_s, 0.0))
                    col = jnp.sum(q_blk * e * k_s, axis=-1, keepdims=True)
                    p_diag = jnp.where((sub_c == sb * sub + s) & (sub_r >= s), col, p_diag)
                p_rows.append(jnp.where(sub_c < sb * sub, p_off, p_diag))
            p = jnp.concatenate(p_rows, axis=0)
            st = st_ref[hh]
            o = _dot(p.astype(BF16), vb) + _dot_nt((qf * jnp.exp(a)).astype(BF16), st.astype(BF16))
            a_last = a[c - 1:c, :]
            k_end = kf * jnp.exp(a_last - a)
            st_ref[hh] = st * jnp.exp(a_last) + _dot(v.T.astype(BF16), k_end.astype(BF16))
            o = o * lax.rsqrt(jnp.mean(o * o, axis=-1, keepdims=True) + NORM_EPS) * ng
            o_ref[pl.ds(r0, c), cols] = (o * (gx * jax.nn.sigmoid(gx))).astype(o_ref.dtype)
        return carry

    lax.fori_loop(0, tb // c, chunk, 0)


def hgrn2(proj, lay, lb, norm_g, batch, seq, heads, heads_per_step=4, tb=512):
    hw = heads_per_step * HEAD_DIM
    nt = seq // tb
    ng = heads // heads_per_step

    def col(name):
        base = lay[name] // hw
        return lambda b, h, t: (b * nt + t, base + h)

    return pl.pallas_call(
        functools.partial(_hgrn_kernel, heads_per_step=heads_per_step, tb=tb),
        grid=(batch, ng, nt),
        in_specs=[pl.BlockSpec((tb, hw), col("q_h")),
                  pl.BlockSpec((tb, hw), col("f_h")),
                  pl.BlockSpec((tb, hw), col("i_h")),
                  pl.BlockSpec((tb, hw), col("g_h")),
                  pl.BlockSpec((1, hw), lambda b, h, t: (0, h)),
                  pl.BlockSpec((1, HEAD_DIM), lambda b, h, t: (0, 0))],
        out_specs=pl.BlockSpec((tb, hw), lambda b, h, t: (b * nt + t, h)),
        out_shape=jax.ShapeDtypeStruct((batch * seq, heads * HEAD_DIM), BF16),
        scratch_shapes=[pltpu.VMEM((heads_per_step, HEAD_DIM, HEAD_DIM), F32)],
        compiler_params=_cparams(("parallel", "parallel", "arbitrary")),
        name="hgrn2",
    )(proj, proj, proj, proj, lb.reshape(1, -1), norm_g.reshape(1, HEAD_DIM))


def _merge_kernel(a_ref, b_ref, wa_ref, wb_ref, ga_ref, gb_ref, o_ref):
    ya = _dot(a_ref[...], wa_ref[...])
    yb = _dot(b_ref[...], wb_ref[...])
    ga = jax.nn.sigmoid(ga_ref[...].astype(F32))
    gb = jax.nn.sigmoid(gb_ref[...].astype(F32))
    o_ref[...] = (ga * ya + gb * yb).astype(o_ref.dtype)


def gated_merge(attn, hg, w_pa, w_pb, proj, lay, tm=1024, tn=512):
    m, ka = attn.shape
    kb = hg.shape[1]
    d = w_pa.shape[1]
    ga0, gb0 = lay["gate_a"] // tn, lay["gate_b"] // tn
    return pl.pallas_call(
        _merge_kernel,
        grid=(m // tm, d // tn),
        in_specs=[pl.BlockSpec((tm, ka), lambda i, j: (i, 0)),
                  pl.BlockSpec((tm, kb), lambda i, j: (i, 0)),
                  pl.BlockSpec((ka, tn), lambda i, j: (0, j)),
                  pl.BlockSpec((kb, tn), lambda i, j: (0, j)),
                  pl.BlockSpec((tm, tn), lambda i, j: (i, ga0 + j)),
                  pl.BlockSpec((tm, tn), lambda i, j: (i, gb0 + j))],
        out_specs=pl.BlockSpec((tm, tn), lambda i, j: (i, j)),
        out_shape=jax.ShapeDtypeStruct((m, d), BF16),
        compiler_params=_cparams(("parallel", "parallel")),
        name="gated_merge",
    )(attn, hg, w_pa, w_pb, proj, proj)


def _norm_residual_kernel(x_ref, y_ref, gp_ref, gn_ref, xo_ref, ho_ref):
    y = y_ref[...]
    yn = y * lax.rsqrt(jnp.mean(y * y, axis=-1, keepdims=True) + NORM_EPS) * gp_ref[...]
    x = x_ref[...] + yn
    xo_ref[...] = x
    ho_ref[...] = (x * lax.rsqrt(jnp.mean(x * x, axis=-1, keepdims=True) + NORM_EPS)
                   * gn_ref[...]).astype(ho_ref.dtype)


def norm_residual(x, y, g_post, g_next, rows=256):
    n, d = x.shape
    row = lambda i: (i, 0)
    const = lambda i: (0, 0)
    return pl.pallas_call(
        _norm_residual_kernel,
        grid=(n // rows,),
        in_specs=[pl.BlockSpec((rows, d), row), pl.BlockSpec((rows, d), row),
                  pl.BlockSpec((1, d), const), pl.BlockSpec((1, d), const)],
        out_specs=[pl.BlockSpec((rows, d), row), pl.BlockSpec((rows, d), row)],
        out_shape=[jax.ShapeDtypeStruct((n, d), F32), jax.ShapeDtypeStruct((n, d), BF16)],
        compiler_params=_cparams(("parallel",)),
        name="norm_residual",
    )(x, y, g_post.reshape(1, d), g_next.reshape(1, d))


def _conv_geglu_kernel(ug_ref, uv_ref, hg_ref, hv_ref, wg_ref, wv_ref, bg_ref, bv_ref, o_ref,
                       *, rows, seq):
    first = (pl.program_id(0) * rows) % seq == 0
    row = lax.broadcasted_iota(jnp.int32, (rows, 1), 0)

    def conv(u_ref, h_ref, w_ref, b_ref):
        u = u_ref[...].astype(F32)
        halo = jnp.where(first, 0.0, h_ref[...].astype(F32))
        u1 = pltpu.roll(u, 1, 0)
        u1 = jnp.where(row == 0, halo[7:8, :], u1)
        u2 = pltpu.roll(u, 2, 0)
        u2 = jnp.where(row == 0, halo[6:7, :], jnp.where(row == 1, halo[7:8, :], u2))
        return b_ref[...] + w_ref[0:1, :] * u2 + w_ref[1:2, :] * u1 + w_ref[2:3, :] * u

    gate = conv(ug_ref, hg_ref, wg_ref, bg_ref)
    val = conv(uv_ref, hv_ref, wv_ref, bv_ref)
    o_ref[...] = (jax.nn.gelu(gate, approximate=True) * val).astype(o_ref.dtype)


def conv_geglu(u, conv_w, conv_b, seq, rows=128):
    n, two_f = u.shape
    dff = two_f // 2
    tc = dff // 2 if (dff // 2) % LANES == 0 else dff
    nc = dff // tc
    hb = rows // 8
    cur = lambda off: (lambda i, j: (i, off + j))
    halo = lambda off: (lambda i, j: (jnp.maximum(i * hb - 1, 0), off + j))
    par = lambda off: (lambda i, j: (0, off + j))
    return pl.pallas_call(
        functools.partial(_conv_geglu_kernel, rows=rows, seq=seq),
        grid=(n // rows, nc),
        in_specs=[pl.BlockSpec((rows, tc), cur(0)), pl.BlockSpec((rows, tc), cur(nc)),
                  pl.BlockSpec((8, tc), halo(0)), pl.BlockSpec((8, tc), halo(nc)),
                  pl.BlockSpec((CONV_WIDTH, tc), par(0)), pl.BlockSpec((CONV_WIDTH, tc), par(nc)),
                  pl.BlockSpec((1, tc), par(0)), pl.BlockSpec((1, tc), par(nc))],
        out_specs=pl.BlockSpec((rows, tc), lambda i, j: (i, j)),
        out_shape=jax.ShapeDtypeStruct((n, dff), BF16),
        compiler_params=_cparams(("parallel", "parallel")),
        name="conv_geglu",
    )(u, u, u, u, conv_w, conv_w, conv_b.reshape(1, two_f), conv_b.reshape(1, two_f))


def _in_layout(d, aw, iw, hk, hv, kvl):
    segs = [("gate_a", d), ("gate_b", d), ("q_i", iw), ("q_a", aw), ("q_h", hk), ("f_h", hk),
            ("i_h", hv), ("g_h", hv), ("c_kv", kvl), ("k_i", HEAD_DIM), ("w_i", HEAD_DIM)]
    lay, off = {}, 0
    for name, w in segs:
        assert off % w == 0, (name, off, w)
        lay[name] = off
        off += w
    lay["total"] = off
    return lay


def _reorder_w_in(w, lay, d, aw, iw, ih, hk, hv, kvl):
    splits = np.cumsum([aw, kvl, iw, HEAD_DIM, ih, hk, hk, hv, hv, d, d])[:-1].tolist()
    q_a, c_kv, q_i, k_i, w_i, q_h, f_h, i_h, g_h, gate_a, gate_b = jnp.split(w, splits, axis=1)
    w_i = jnp.pad(w_i, ((0, 0), (0, HEAD_DIM - ih)))
    out = jnp.concatenate([gate_a, gate_b, q_i, q_a, q_h, f_h, i_h, g_h, c_kv, k_i, w_i], axis=1)
    assert out.shape[1] == lay["total"]
    return out.astype(BF16)


def _pick(n, cands):
    for c in cands:
        if n % c == 0:
            return c
    raise ValueError(f"no tile for {n}")


def kernel(x, rel_bias, hgrn_lb_logits, mix_pre_g, mix_post_g, w_in, kv_norm_g, w_uk, w_uv, idx_k_ln_g, idx_k_ln_b, hgrn_norm_g, w_proj_attn, w_proj_hgrn, w_out, ffn_pre_g, ffn_post_g, w_up, conv_w, conv_b, w_down):
    batch, seq, d = x.shape
    depth = w_in.shape[0]
    aw = w_proj_attn.shape[1]
    hv = w_proj_hgrn.shape[1]
    hk = hgrn_lb_logits.shape[1]
    kvl = w_uk.shape[1]
    dff = w_down.shape[1]
    heads = aw // HEAD_DIM
    hheads = hv // HEAD_DIM
    ih = w_in.shape[2] - (aw + kvl + HEAD_DIM + 2 * hk + 2 * hv + 2 * d)
    ih = ih // (HEAD_DIM + 1)
    iw = ih * HEAD_DIM
    lay = _in_layout(d, aw, iw, hk, hv, kvl)
    n = batch * seq

    lb_p = jax.nn.softmax(hgrn_lb_logits.astype(F32), axis=0)
    lb_all = jnp.cumsum(lb_p, axis=0) - lb_p[0:1]
    bias_tiles = rel_bias_tiles(rel_bias, heads)

    tm = _pick(n, (1024, 512, 256))
    xf = x.reshape(n, d)
    h = rmsnorm_cast(xf, mix_pre_g[0])
    for l in range(depth):
        w_in_l = _reorder_w_in(w_in[l], lay, d, aw, iw, ih, hk, hv, kvl)
        proj = matmul(h, w_in_l, tm, _pick(lay["total"], (896, 512, 256, 128)), name="in_proj")
        k, v, kidx, wi = dsa_prep(proj, lay, kv_norm_g[l], w_uk[l], w_uv[l],
                                  idx_k_ln_g[l], idx_k_ln_b[l], ih)
        attn = dsa_attention(proj, lay, wi, kidx, k, v, bias_tiles, batch, seq, heads, ih)
        hg = hgrn2(proj, lay, lb_all[l], hgrn_norm_g[l], batch, seq, hheads)
        merged = gated_merge(attn, hg, w_proj_attn[l].astype(BF16), w_proj_hgrn[l].astype(BF16),
                             proj, lay, tm=tm)
        y = matmul(merged, w_out[l].astype(BF16), tm, 512, out_dtype=F32, name="out_proj")
        xf, h = norm_residual(xf, y, mix_post_g[l], ffn_pre_g[l])
        u = matmul(h, w_up[l].astype(BF16), tm, _pick(2 * dff, (512, 256, 128)), name="ffn_up")
        act = conv_geglu(u, conv_w[l], conv_b[l], seq)
        y = matmul(act, w_down[l].astype(BF16), tm, 512, tk=_pick(dff, (dff // 2, dff)),
                   out_dtype=F32, name="ffn_down")
        g_next = mix_pre_g[l + 1] if l + 1 < depth else mix_pre_g[0]
        xf, h = norm_residual(xf, y, ffn_post_g[l], g_next)
    return xf.reshape(batch, seq, d)
```

```python
import functools
import math

import jax
import jax.numpy as jnp
import numpy as np
from jax import lax
from jax.experimental import pallas as pl
from jax.experimental.pallas import tpu as pltpu

F32 = jnp.float32
BF16 = jnp.bfloat16

NORM_EPS = 1e-6
HEAD_DIM = 128
IDX_TOPK = 256
REL_BUCKETS = 32
REL_MAX_DIST = 128
CONV_WIDTH = 3
LANES = 128
VMEM_LIMIT_BYTES = 56 * 1024 * 1024

ATT_BLOCK = 256
HGRN_CHUNK = 64
HGRN_SUB = 16
INT_MIN = -2 ** 31
MASK_NEG = -1e30
M_INIT = -1e20
LOG2E = math.log2(math.e)


def _cparams(sem):
    return pltpu.CompilerParams(dimension_semantics=sem, vmem_limit_bytes=VMEM_LIMIT_BYTES)


def _dot(a, b):
    return jnp.dot(a, b, preferred_element_type=F32)


def _dot_nt(a, b):
    return lax.dot_general(a, b, (((1,), (1,)), ((), ())), preferred_element_type=F32)


def _dot_tn(a, b):
    return lax.dot_general(a, b, (((0,), (0,)), ((), ())), preferred_element_type=F32)


def _rmsnorm_kernel(x_ref, g_ref, o_ref):
    x = x_ref[...]
    r = lax.rsqrt(jnp.mean(x * x, axis=-1, keepdims=True) + NORM_EPS)
    o_ref[...] = (x * r * g_ref[...]).astype(o_ref.dtype)


def rmsnorm_cast(x, g, rows=256):
    n, d = x.shape
    return pl.pallas_call(
        _rmsnorm_kernel,
        grid=(n // rows,),
        in_specs=[pl.BlockSpec((rows, d), lambda i: (i, 0)),
                  pl.BlockSpec((1, d), lambda i: (0, 0))],
        out_specs=pl.BlockSpec((rows, d), lambda i: (i, 0)),
        out_shape=jax.ShapeDtypeStruct((n, d), BF16),
        compiler_params=_cparams(("parallel",)),
        name="rmsnorm_cast",
    )(x, g.reshape(1, d))


def _mm_kernel(x_ref, w_ref, o_ref):
    o_ref[...] = _dot(x_ref[...], w_ref[...]).astype(o_ref.dtype)


def _mm_acc_kernel(x_ref, w_ref, o_ref, acc_ref, *, nk):
    k = pl.program_id(2)

    @pl.when(k == 0)
    def _():
        acc_ref[...] = jnp.zeros_like(acc_ref)

    acc_ref[...] += _dot(x_ref[...], w_ref[...])

    @pl.when(k == nk - 1)
    def _():
        o_ref[...] = acc_ref[...].astype(o_ref.dtype)


def matmul(x, w, tm, tn, tk=None, out_dtype=BF16, name="matmul"):
    m, kdim = x.shape
    n = w.shape[1]
    tk = kdim if tk is None else tk
    nk = kdim // tk
    assert m % tm == 0 and n % tn == 0 and kdim % tk == 0
    if nk == 1:
        return pl.pallas_call(
            _mm_kernel,
            grid=(m // tm, n // tn),
            in_specs=[pl.BlockSpec((tm, kdim), lambda i, j: (i, 0)),
                      pl.BlockSpec((kdim, tn), lambda i, j: (0, j))],
            out_specs=pl.BlockSpec((tm, tn), lambda i, j: (i, j)),
            out_shape=jax.ShapeDtypeStruct((m, n), out_dtype),
            compiler_params=_cparams(("parallel", "parallel")),
            name=name,
        )(x, w)
    return pl.pallas_call(
        functools.partial(_mm_acc_kernel, nk=nk),
        grid=(m // tm, n // tn, nk),
        in_specs=[pl.BlockSpec((tm, tk), lambda i, j, k: (i, k)),
                  pl.BlockSpec((tk, tn), lambda i, j, k: (k, j))],
        out_specs=pl.BlockSpec((tm, tn), lambda i, j, k: (i, j)),
        out_shape=jax.ShapeDtypeStruct((m, n), out_dtype),
        scratch_shapes=[pltpu.VMEM((tm, tn), F32)],
        compiler_params=_cparams(("parallel", "parallel", "arbitrary")),
        name=name,
    )(x, w)


def _mm_wstat_kernel(x_ref, w_ref, o_ref, wb_ref):
    @pl.when(pl.program_id(1) == 0)
    def _():
        wb_ref[...] = w_ref[...].astype(BF16)

    o_ref[...] = _dot(x_ref[...], wb_ref[...]).astype(o_ref.dtype)


def matmul_wstat(x, w, layer, tm, tn, out_dtype=BF16, name="matmul_wstat"):
    m, kdim = x.shape
    n = w.shape[2]
    assert m % tm == 0 and n % tn == 0
    return pl.pallas_call(
        _mm_wstat_kernel,
        grid=(n // tn, m // tm),
        in_specs=[pl.BlockSpec((tm, kdim), lambda j, i: (i, 0)),
                  pl.BlockSpec((None, kdim, tn), lambda j, i: (layer, 0, j))],
        out_specs=pl.BlockSpec((tm, tn), lambda j, i: (i, j)),
        out_shape=jax.ShapeDtypeStruct((m, n), out_dtype),
        scratch_shapes=[pltpu.VMEM((kdim, tn), BF16)],
        compiler_params=_cparams(("parallel", "arbitrary")),
        name=name,
    )(x, w)


def _dsa_prep_kernel(ckv_ref, kw_ref, kvg_ref, wuk_ref, wuv_ref, lng_ref, lnb_ref,
                     k_ref, v_ref, kidx_ref, wi_ref, *, k_scale, w_scale):
    c = ckv_ref[...].astype(F32)
    c = c * lax.rsqrt(jnp.mean(c * c, axis=-1, keepdims=True) + NORM_EPS) * kvg_ref[...]
    cb = c.astype(BF16)
    k_ref[...] = (_dot(cb, wuk_ref[...]) * k_scale).astype(k_ref.dtype)
    v_ref[:, :HEAD_DIM] = _dot(cb, wuv_ref[...]).astype(v_ref.dtype)
    v_ref[:, HEAD_DIM:] = jnp.ones((v_ref.shape[0], HEAD_DIM), v_ref.dtype)
    kw = kw_ref[...].astype(F32)
    ki = kw[:, :HEAD_DIM]
    mu = jnp.mean(ki, axis=-1, keepdims=True)
    kc = ki - mu
    kn = kc * lax.rsqrt(jnp.mean(kc * kc, axis=-1, keepdims=True) + NORM_EPS)
    kidx_ref[...] = (kn * lng_ref[...] + lnb_ref[...]).astype(kidx_ref.dtype)
    wi_ref[...] = kw[:, HEAD_DIM:] * w_scale


def dsa_prep(proj, lay, kv_norm_g, w_uk, w_uv, ln_g, ln_b, idx_heads, rows=512):
    n = proj.shape[0]
    kvl = w_uk.shape[0]
    row = lambda i: (i, 0)
    const = lambda i: (0, 0)
    out = jax.ShapeDtypeStruct((n, HEAD_DIM), BF16)
    return pl.pallas_call(
        functools.partial(_dsa_prep_kernel, k_scale=HEAD_DIM ** -0.5 * LOG2E,
                          w_scale=idx_heads ** -0.5 * HEAD_DIM ** -0.5),
        grid=(n // rows,),
        in_specs=[pl.BlockSpec((rows, kvl), lambda i: (i, lay["c_kv"] // kvl)),
                  pl.BlockSpec((rows, 2 * HEAD_DIM), lambda i: (i, lay["k_i"] // (2 * HEAD_DIM))),
                  pl.BlockSpec((1, kvl), const),
                  pl.BlockSpec((kvl, HEAD_DIM), const),
                  pl.BlockSpec((kvl, HEAD_DIM), const),
                  pl.BlockSpec((1, HEAD_DIM), const),
                  pl.BlockSpec((1, HEAD_DIM), const)],
        out_specs=[pl.BlockSpec((rows, HEAD_DIM), row), pl.BlockSpec((rows, 2 * HEAD_DIM), row),
                   pl.BlockSpec((rows, HEAD_DIM), row), pl.BlockSpec((rows, HEAD_DIM), row)],
        out_shape=[out, jax.ShapeDtypeStruct((n, 2 * HEAD_DIM), BF16), out,
                   jax.ShapeDtypeStruct((n, HEAD_DIM), F32)],
        compiler_params=_cparams(("parallel",)),
        name="dsa_prep",
    )(proj, proj, kv_norm_g.reshape(1, kvl), w_uk.astype(BF16), w_uv.astype(BF16),
      ln_g.reshape(1, HEAD_DIM), ln_b.reshape(1, HEAD_DIM))


def _sortable_key(x):
    u = lax.bitcast_convert_type(x, jnp.int32)
    return u ^ ((u >> 31) & jnp.int32(0x7FFFFFFF))


def _dsa_kernel(qa_ref, qi_ref, wi_ref, kidx_ref, k_ref, v_ref, bias_ref, o_ref,
                qas_ref, qis_ref, keys_ref, mask_ref, m_ref, acc_ref, *, heads, idx_heads, topk):
    qb = ATT_BLOCK
    ck = ATT_BLOCK
    j = pl.program_id(1)

    for h in range(heads):
        qas_ref[h * qb:(h + 1) * qb, :] = qa_ref[:, h * HEAD_DIM:(h + 1) * HEAD_DIM]
    for h in range(idx_heads):
        qis_ref[h * qb:(h + 1) * qb, :] = qi_ref[:, h * HEAD_DIM:(h + 1) * HEAD_DIM]

    row_pos = j * qb + lax.broadcasted_iota(jnp.int32, (qb, ck), 0)
    lane = lax.broadcasted_iota(jnp.int32, (qb, ck), 1)
    hg = 8 if idx_heads % 8 == 0 else idx_heads

    def score_chunk(c, carry):
        kc = kidx_ref[pl.ds(pl.multiple_of(c * ck, ck), ck), :]
        acc = jnp.zeros((qb, ck), F32)
        for g in range(idx_heads // hg):
            s = _dot_nt(qis_ref[g * hg * qb:(g + 1) * hg * qb, :], kc)
            for hh in range(hg):
                h = g * hg + hh
                acc = acc + jnp.maximum(s[hh * qb:(hh + 1) * qb, :], 0.0) * wi_ref[:, h:h + 1]
        valid = (c * ck + lane) <= row_pos
        keys_ref[c] = jnp.where(valid, _sortable_key(acc), jnp.int32(INT_MIN))
        return carry

    lax.fori_loop(0, j + 1, score_chunk, 0)

    def bit_step(i, lo):
        cand = lo + lax.shift_left(jnp.int32(1), jnp.int32(31) - i)

        def count_chunk(c, part):
            ge = (keys_ref[c] >= cand).astype(jnp.int32)
            for t in range(ck // LANES):
                part = part + ge[:, t * LANES:(t + 1) * LANES]
            return part

        part = lax.fori_loop(0, j + 1, count_chunk, jnp.zeros((qb, LANES), jnp.int32))
        cnt = jnp.sum(part, axis=-1, keepdims=True)
        return jnp.where(cnt >= topk, cand, lo)

    lo = lax.fori_loop(0, 32, bit_step, jnp.full((qb, 1), INT_MIN, jnp.int32))
    thr = jnp.maximum(lo, jnp.int32(INT_MIN + 1))

    m_ref[...] = jnp.full_like(m_ref, M_INIT)
    acc_ref[...] = jnp.zeros_like(acc_ref)

    def attend(c, bias_idx):
        off = pl.multiple_of(c * ck, ck)
        kc = k_ref[pl.ds(off, ck), :]
        vc = v_ref[pl.ds(off, ck), :]
        mask_ref[...] = jnp.where(keys_ref[c] >= thr, 0.0, MASK_NEG)
        for h in range(heads):
            rows = slice(h * qb, (h + 1) * qb)
            s = _dot_nt(qas_ref[rows, :], kc) + mask_ref[...]
            if bias_idx is not None:
                s = s + bias_ref[bias_idx, rows, :]
            parts = [s[:, t * LANES:(t + 1) * LANES] for t in range(ck // LANES)]
            smax = functools.reduce(jnp.maximum, parts)
            m_prev = m_ref[rows, :]
            m_new = jnp.maximum(m_prev, jnp.max(smax, axis=-1, keepdims=True))
            p = jnp.concatenate([jnp.exp2(x - m_new) for x in parts], axis=1).astype(BF16)
            alpha = jnp.exp2(m_prev - m_new)
            pv = _dot(p, vc)
            acc_ref[rows, :HEAD_DIM] = alpha * acc_ref[rows, :HEAD_DIM] + pv[:, :HEAD_DIM]
            acc_ref[rows, HEAD_DIM:] = alpha * acc_ref[rows, HEAD_DIM:] + pv[:, HEAD_DIM:]
            m_ref[rows, :] = m_new

    def far_chunk(c, carry):
        attend(c, None)
        return carry

    lax.fori_loop(0, j - 1, far_chunk, 0)

    @pl.when(j >= 1)
    def _():
        attend(j - 1, 1)

    attend(j, 0)

    for h in range(heads):
        rows = slice(h * qb, (h + 1) * qb)
        o = acc_ref[rows, :HEAD_DIM] / acc_ref[rows, HEAD_DIM:]
        o_ref[:, h * HEAD_DIM:(h + 1) * HEAD_DIM] = o.astype(o_ref.dtype)


def dsa_attention(proj, lay, wi, kidx, k, v, bias_tiles, batch, seq, heads, idx_heads):
    qb = ATT_BLOCK
    nqb = seq // qb
    aw = heads * HEAD_DIM
    iw = idx_heads * HEAD_DIM
    topk = min(IDX_TOPK, seq // 4)
    return pl.pallas_call(
        functools.partial(_dsa_kernel, heads=heads, idx_heads=idx_heads, topk=topk),
        grid=(batch, nqb),
        in_specs=[pl.BlockSpec((qb, aw), lambda b, j: (b * nqb + j, lay["q_a"] // aw)),
                  pl.BlockSpec((qb, iw), lambda b, j: (b * nqb + j, lay["q_i"] // iw)),
                  pl.BlockSpec((qb, HEAD_DIM), lambda b, j: (b * nqb + j, 0)),
                  pl.BlockSpec((seq, HEAD_DIM), lambda b, j: (b, 0)),
                  pl.BlockSpec((seq, HEAD_DIM), lambda b, j: (b, 0)),
                  pl.BlockSpec((seq, 2 * HEAD_DIM), lambda b, j: (b, 0)),
                  pl.BlockSpec((2, heads * qb, qb), lambda b, j: (0, 0, 0),
                               pipeline_mode=pl.Buffered(1))],
        out_specs=pl.BlockSpec((qb, aw), lambda b, j: (b * nqb + j, 0)),
        out_shape=jax.ShapeDtypeStruct((batch * seq, aw), BF16),
        scratch_shapes=[pltpu.VMEM((heads * qb, HEAD_DIM), BF16),
                        pltpu.VMEM((idx_heads * qb, HEAD_DIM), BF16),
                        pltpu.VMEM((seq // qb, qb, qb), jnp.int32),
                        pltpu.VMEM((qb, qb), F32),
                        pltpu.VMEM((heads * qb, LANES), F32),
                        pltpu.VMEM((heads * qb, 2 * HEAD_DIM), F32)],
        compiler_params=_cparams(("parallel", "arbitrary")),
        name="dsa_attention",
    )(proj, proj, wi, kidx, k, v, bias_tiles)


def _t5_bucket(dist):
    max_exact = REL_BUCKETS // 2
    d = jnp.maximum(dist, 0)
    df = jnp.maximum(d, 1).astype(F32)
    large = max_exact + (jnp.log(df / max_exact) / math.log(REL_MAX_DIST / max_exact)
                         * (REL_BUCKETS - max_exact)).astype(jnp.int32)
    large = jnp.minimum(large, REL_BUCKETS - 1)
    return jnp.where(d < max_exact, d, large)


def rel_bias_tiles(rel_bias, heads):
    qb = ATT_BLOCK
    assert _static_far_bucket(qb) == REL_BUCKETS - 1
    tbl = (rel_bias[_t5_bucket(jnp.arange(4 * qb))] - rel_bias[REL_BUCKETS - 1][None, :]).T * LOG2E
    y = jnp.arange(2 * qb)
    tiles = []
    for off in (0, qb):
        d = off + jnp.where(y < qb, -y, 2 * qb - y)
        u = tbl[:, jnp.maximum(d, 0)]
        flat = jnp.broadcast_to(u[:, None, :], (heads, qb, 2 * qb)).reshape(heads, 2 * qb * qb)
        skew = flat[:, :qb * (2 * qb - 1)].reshape(heads, qb, 2 * qb - 1)
        tiles.append(skew[:, :, :qb].reshape(heads * qb, qb))
    return jnp.stack(tiles).astype(F32)


def _static_far_bucket(d):
    max_exact = REL_BUCKETS // 2
    v = max_exact + int(math.log(d / max_exact) / math.log(REL_MAX_DIST / max_exact)
                        * (REL_BUCKETS - max_exact))
    return min(v, REL_BUCKETS - 1)


def _hgrn_kernel(q_ref, f_ref, i_ref, g_ref, lb_ref, ng_ref, o_ref, st_ref, *, heads_per_step, tb):
    c = HGRN_CHUNK
    sub = HGRN_SUB
    nsub = c // sub

    @pl.when(pl.program_id(2) == 0)
    def _():
        st_ref[...] = jnp.zeros_like(st_ref)

    r_io = lax.broadcasted_iota(jnp.int32, (c, c), 0)
    c_io = lax.broadcasted_iota(jnp.int32, (c, c), 1)
    tri = (c_io <= r_io).astype(F32)
    off_mask = c_io < (r_io // sub) * sub
    sub_r = lax.broadcasted_iota(jnp.int32, (sub, c), 0)
    sub_c = lax.broadcasted_iota(jnp.int32, (sub, c), 1)
    ng = ng_ref[...]

    def chunk(ci, carry):
        r0 = pl.multiple_of(ci * c, c)
        for hh in range(heads_per_step):
            cols = slice(hh * HEAD_DIM, (hh + 1) * HEAD_DIM)
            lb = lb_ref[:, cols]
            z = f_ref[pl.ds(r0, c), cols].astype(F32)
            qx = q_ref[pl.ds(r0, c), cols].astype(F32)
            v = i_ref[pl.ds(r0, c), cols].astype(F32)
            gx = g_ref[pl.ds(r0, c), cols].astype(F32)
            sig = jax.nn.sigmoid(z)
            lf = jnp.log(lb + (1.0 - lb) * sig)
            kf = (1.0 - lb) * jax.nn.sigmoid(-z)
            qf = qx * jax.nn.sigmoid(qx)
            a = jnp.dot(tri, lf, preferred_element_type=F32, precision=lax.Precision.HIGHEST)
            vb = v.astype(BF16)

            p_rows = []
            for sb in range(nsub):
                rs = slice(sb * sub, (sb + 1) * sub)
                a_blk = a[rs, :]
                q_blk = qf[rs, :]
                a_start = a[sb * sub - 1:sb * sub, :] if sb > 0 else jnp.zeros((1, HEAD_DIM), F32)
                q_t = q_blk * jnp.exp(a_blk - a_start)
                k_t = kf * jnp.exp(jnp.minimum(a_start - a, 0.0))
                p_off = _dot_nt(q_t.astype(BF16), k_t.astype(BF16))
                p_diag = jnp.zeros((sub, c), F32)
                t_io = lax.broadcasted_iota(jnp.int32, (sub, HEAD_DIM), 0)
                for s in range(sub):
                    a_s = a[sb * sub + s:sb * sub + s + 1, :]
                    k_s = kf[sb * sub + s:sb * sub + s + 1, :]
                    e = jnp.exp(jnp.minimum(a_blk - a_s, 0.0))
                    col = jnp.sum(q_blk * e * k_s, axis=-1, keepdims=True)
                    p_diag = jnp.where((sub_c == sb * sub + s) & (sub_r >= s), col, p_diag)
                p_rows.append(jnp.where(sub_c < sb * sub, p_off, p_diag))
            p = jnp.concatenate(p_rows, axis=0)
            st = st_ref[hh]
            o = _dot(p.astype(BF16), vb) + _dot_nt((qf * jnp.exp(a)).astype(BF16), st.astype(BF16))
            a_last = a[c - 1:c, :]
            k_end = kf * jnp.exp(a_last - a)
            st_ref[hh] = st * jnp.exp(a_last) + _dot(v.T.astype(BF16), k_end.astype(BF16))
            o = o * lax.rsqrt(jnp.mean(o * o, axis=-1, keepdims=True) + NORM_EPS) * ng
            o_ref[pl.ds(r0, c), cols] = (o * (gx * jax.nn.sigmoid(gx))).astype(o_ref.dtype)
        return carry

    lax.fori_loop(0, tb // c, chunk, 0)


def hgrn2(proj, lay, lb, norm_g, batch, seq, heads, heads_per_step=4, tb=512):
    hw = heads_per_step * HEAD_DIM
    nt = seq // tb
    ng = heads // heads_per_step

    def col(name):
        base = lay[name] // hw
        return lambda b, h, t: (b * nt + t, base + h)

    return pl.pallas_call(
        functools.partial(_hgrn_kernel, heads_per_step=heads_per_step, tb=tb),
        grid=(batch, ng, nt),
        in_specs=[pl.BlockSpec((tb, hw), col("q_h")),
                  pl.BlockSpec((tb, hw), col("f_h")),
                  pl.BlockSpec((tb, hw), col("i_h")),
                  pl.BlockSpec((tb, hw), col("g_h")),
                  pl.BlockSpec((1, hw), lambda b, h, t: (0, h)),
                  pl.BlockSpec((1, HEAD_DIM), lambda b, h, t: (0, 0))],
        out_specs=pl.BlockSpec((tb, hw), lambda b, h, t: (b * nt + t, h)),
        out_shape=jax.ShapeDtypeStruct((batch * seq, heads * HEAD_DIM), BF16),
        scratch_shapes=[pltpu.VMEM((heads_per_step, HEAD_DIM, HEAD_DIM), F32)],
        compiler_params=_cparams(("parallel", "parallel", "arbitrary")),
        name="hgrn2",
    )(proj, proj, proj, proj, lb.reshape(1, -1), norm_g.reshape(1, HEAD_DIM))


def _merge_kernel(a_ref, b_ref, wa_ref, wb_ref, ga_ref, gb_ref, o_ref):
    ya = _dot(a_ref[...], wa_ref[...])
    yb = _dot(b_ref[...], wb_ref[...])
    ga = jax.nn.sigmoid(ga_ref[...].astype(F32))
    gb = jax.nn.sigmoid(gb_ref[...].astype(F32))
    o_ref[...] = (ga * ya + gb * yb).astype(o_ref.dtype)


def gated_merge(attn, hg, w_pa, w_pb, proj, lay, tm=1024, tn=512):
    m, ka = attn.shape
    kb = hg.shape[1]
    d = w_pa.shape[1]
    ga0, gb0 = lay["gate_a"] // tn, lay["gate_b"] // tn
    return pl.pallas_call(
        _merge_kernel,
        grid=(m // tm, d // tn),
        in_specs=[pl.BlockSpec((tm, ka), lambda i, j: (i, 0)),
                  pl.BlockSpec((tm, kb), lambda i, j: (i, 0)),
                  pl.BlockSpec((ka, tn), lambda i, j: (0, j)),
                  pl.BlockSpec((kb, tn), lambda i, j: (0, j)),
                  pl.BlockSpec((tm, tn), lambda i, j: (i, ga0 + j)),
                  pl.BlockSpec((tm, tn), lambda i, j: (i, gb0 + j))],
        out_specs=pl.BlockSpec((tm, tn), lambda i, j: (i, j)),
        out_shape=jax.ShapeDtypeStruct((m, d), BF16),
        compiler_params=_cparams(("parallel", "parallel")),
        name="gated_merge",
    )(attn, hg, w_pa, w_pb, proj, proj)


def _norm_residual_kernel(x_ref, y_ref, gp_ref, gn_ref, xo_ref, ho_ref):
    y = y_ref[...]
    yn = y * lax.rsqrt(jnp.mean(y * y, axis=-1, keepdims=True) + NORM_EPS) * gp_ref[...]
    x = x_ref[...] + yn
    xo_ref[...] = x
    ho_ref[...] = (x * lax.rsqrt(jnp.mean(x * x, axis=-1, keepdims=True) + NORM_EPS)
                   * gn_ref[...]).astype(ho_ref.dtype)


def norm_residual(x, y, g_post, g_next, rows=256):
    n, d = x.shape
    row = lambda i: (i, 0)
    const = lambda i: (0, 0)
    return pl.pallas_call(
        _norm_residual_kernel,
        grid=(n // rows,),
        in_specs=[pl.BlockSpec((rows, d), row), pl.BlockSpec((rows, d), row),
                  pl.BlockSpec((1, d), const), pl.BlockSpec((1, d), const)],
        out_specs=[pl.BlockSpec((rows, d), row), pl.BlockSpec((rows, d), row)],
        out_shape=[jax.ShapeDtypeStruct((n, d), F32), jax.ShapeDtypeStruct((n, d), BF16)],
        compiler_params=_cparams(("parallel",)),
        name="norm_residual",
    )(x, y, g_post.reshape(1, d), g_next.reshape(1, d))


def _conv_geglu_kernel(ug_ref, uv_ref, hg_ref, hv_ref, wg_ref, wv_ref, bg_ref, bv_ref, o_ref,
                       *, rows, seq):
    first = (pl.program_id(0) * rows) % seq == 0
    row = lax.broadcasted_iota(jnp.int32, (rows, 1), 0)

    def conv(u_ref, h_ref, w_ref, b_ref):
        u = u_ref[...].astype(F32)
        halo = jnp.where(first, 0.0, h_ref[...].astype(F32))
        u1 = pltpu.roll(u, 1, 0)
        u1 = jnp.where(row == 0, halo[7:8, :], u1)
        u2 = pltpu.roll(u, 2, 0)
        u2 = jnp.where(row == 0, halo[6:7, :], jnp.where(row == 1, halo[7:8, :], u2))
        return b_ref[...] + w_ref[0:1, :] * u2 + w_ref[1:2, :] * u1 + w_ref[2:3, :] * u

    gate = conv(ug_ref, hg_ref, wg_ref, bg_ref)
    val = conv(uv_ref, hv_ref, wv_ref, bv_ref)
    o_ref[...] = (jax.nn.gelu(gate, approximate=True) * val).astype(o_ref.dtype)


def conv_geglu(u, conv_w, conv_b, seq, rows=128):
    n, two_f = u.shape
    dff = two_f // 2
    tc = dff // 2 if (dff // 2) % LANES == 0 else dff
    nc = dff // tc
    hb = rows // 8
    cur = lambda off: (lambda i, j: (i, off + j))
    halo = lambda off: (lambda i, j: (jnp.maximum(i * hb - 1, 0), off + j))
    par = lambda off: (lambda i, j: (0, off + j))
    return pl.pallas_call(
        functools.partial(_conv_geglu_kernel, rows=rows, seq=seq),
        grid=(n // rows, nc),
        in_specs=[pl.BlockSpec((rows, tc), cur(0)), pl.BlockSpec((rows, tc), cur(nc)),
                  pl.BlockSpec((8, tc), halo(0)), pl.BlockSpec((8, tc), halo(nc)),
                  pl.BlockSpec((CONV_WIDTH, tc), par(0)), pl.BlockSpec((CONV_WIDTH, tc), par(nc)),
                  pl.BlockSpec((1, tc), par(0)), pl.BlockSpec((1, tc), par(nc))],
        out_specs=pl.BlockSpec((rows, tc), lambda i, j: (i, j)),
        out_shape=jax.ShapeDtypeStruct((n, dff), BF16),
        compiler_params=_cparams(("parallel", "parallel")),
        name="conv_geglu",
    )(u, u, u, u, conv_w, conv_w, conv_b.reshape(1, two_f), conv_b.reshape(1, two_f))


def _in_layout(d, aw, iw, hk, hv, kvl):
    segs = [("gate_a", d), ("gate_b", d), ("q_i", iw), ("q_a", aw), ("q_h", hk), ("f_h", hk),
            ("i_h", hv), ("g_h", hv), ("c_kv", kvl), ("k_i", HEAD_DIM), ("w_i", HEAD_DIM)]
    lay, off = {}, 0
    for name, w in segs:
        assert off % w == 0, (name, off, w)
        lay[name] = off
        off += w
    lay["total"] = off
    return lay


def _reorder_w_in(w, lay, d, aw, iw, ih, hk, hv, kvl):
    splits = np.cumsum([aw, kvl, iw, HEAD_DIM, ih, hk, hk, hv, hv, d, d])[:-1].tolist()
    q_a, c_kv, q_i, k_i, w_i, q_h, f_h, i_h, g_h, gate_a, gate_b = jnp.split(w, splits, axis=1)
    w_i = jnp.pad(w_i, ((0, 0), (0, HEAD_DIM - ih)))
    out = jnp.concatenate([gate_a, gate_b, q_i, q_a, q_h, f_h, i_h, g_h, c_kv, k_i, w_i], axis=1)
    assert out.shape[1] == lay["total"]
    return out.astype(BF16)


def _pick(n, cands):
    for c in cands:
        if n % c == 0:
            return c
    raise ValueError(f"no tile for {n}")


def kernel(x, rel_bias, hgrn_lb_logits, mix_pre_g, mix_post_g, w_in, kv_norm_g, w_uk, w_uv, idx_k_ln_g, idx_k_ln_b, hgrn_norm_g, w_proj_attn, w_proj_hgrn, w_out, ffn_pre_g, ffn_post_g, w_up, conv_w, conv_b, w_down):
    batch, seq, d = x.shape
    depth = w_in.shape[0]
    aw = w_proj_attn.shape[1]
    hv = w_proj_hgrn.shape[1]
    hk = hgrn_lb_logits.shape[1]
    kvl = w_uk.shape[1]
    dff = w_down.shape[1]
    heads = aw // HEAD_DIM
    hheads = hv // HEAD_DIM
    ih = w_in.shape[2] - (aw + kvl + HEAD_DIM + 2 * hk + 2 * hv + 2 * d)
    ih = ih // (HEAD_DIM + 1)
    iw = ih * HEAD_DIM
    lay = _in_layout(d, aw, iw, hk, hv, kvl)
    n = batch * seq

    lb_p = jax.nn.softmax(hgrn_lb_logits.astype(F32), axis=0)
    lb_all = jnp.cumsum(lb_p, axis=0) - lb_p[0:1]
    bias_tiles = rel_bias_tiles(rel_bias, heads)

    tm = _pick(n, (1024, 512, 256))
    xf = x.reshape(n, d)
    h = rmsnorm_cast(xf, mix_pre_g[0])
    for l in range(depth):
        w_in_l = _reorder_w_in(w_in[l], lay, d, aw, iw, ih, hk, hv, kvl)
        proj = matmul(h, w_in_l, tm, _pick(lay["total"], (896, 512, 256, 128)), name="in_proj")
        k, v, kidx, wi = dsa_prep(proj, lay, kv_norm_g[l], w_uk[l], w_uv[l],
                                  idx_k_ln_g[l], idx_k_ln_b[l], ih)
        attn = dsa_attention(proj, lay, wi, kidx, k, v, bias_tiles, batch, seq, heads, ih)
        hg = hgrn2(proj, lay, lb_all[l], hgrn_norm_g[l], batch, seq, hheads)
        merged = gated_merge(attn, hg, w_proj_attn[l].astype(BF16), w_proj_hgrn[l].astype(BF16),
                             proj, lay, tm=tm)
        y = matmul_wstat(merged, w_out, l, tm, 512, out_dtype=F32, name="out_proj")
        xf, h = norm_residual(xf, y, mix_post_g[l], ffn_pre_g[l])
        u = matmul_wstat(h, w_up, l, tm, _pick(2 * dff, (512, 256, 128)), name="ffn_up")
        act = conv_geglu(u, conv_w[l], conv_b[l], seq)
        y = matmul(act, w_down[l].astype(BF16), tm, 512, tk=_pick(dff, (dff // 2, dff)),
                   out_dtype=F32, name="ffn_down")
        g_next = mix_pre_g[l + 1] if l + 1 < depth else mix_pre_g[0]
        xf, h = norm_residual(xf, y, ffn_post_g[l], g_next)
    return xf.reshape(batch, seq, d)
```

```python
import functools
import math

import jax
import jax.numpy as jnp
import numpy as np
from jax import lax
from jax.experimental import pallas as pl
from jax.experimental.pallas import tpu as pltpu

F32 = jnp.float32
BF16 = jnp.bfloat16

NORM_EPS = 1e-6
HEAD_DIM = 128
IDX_TOPK = 256
REL_BUCKETS = 32
REL_MAX_DIST = 128
CONV_WIDTH = 3
LANES = 128
VMEM_LIMIT_BYTES = 56 * 1024 * 1024

ATT_BLOCK = 256
HGRN_CHUNK = 64
HGRN_SUB = 16
IN_PROJ_TN = 512
INT_MIN = -2 ** 31
MASK_NEG = -1e30
M_INIT = -1e20
LOG2E = math.log2(math.e)


def _cparams(sem):
    return pltpu.CompilerParams(dimension_semantics=sem, vmem_limit_bytes=VMEM_LIMIT_BYTES)


def _dot(a, b):
    return jnp.dot(a, b, preferred_element_type=F32)


def _dot_nt(a, b):
    return lax.dot_general(a, b, (((1,), (1,)), ((), ())), preferred_element_type=F32)


def _dot_tn(a, b):
    return lax.dot_general(a, b, (((0,), (0,)), ((), ())), preferred_element_type=F32)


def _rmsnorm_kernel(x_ref, g_ref, o_ref):
    x = x_ref[...]
    r = lax.rsqrt(jnp.mean(x * x, axis=-1, keepdims=True) + NORM_EPS)
    o_ref[...] = (x * r * g_ref[...]).astype(o_ref.dtype)


def rmsnorm_cast(x, g, rows=256):
    n, d = x.shape
    return pl.pallas_call(
        _rmsnorm_kernel,
        grid=(n // rows,),
        in_specs=[pl.BlockSpec((rows, d), lambda i: (i, 0)),
                  pl.BlockSpec((1, d), lambda i: (0, 0))],
        out_specs=pl.BlockSpec((rows, d), lambda i: (i, 0)),
        out_shape=jax.ShapeDtypeStruct((n, d), BF16),
        compiler_params=_cparams(("parallel",)),
        name="rmsnorm_cast",
    )(x, g.reshape(1, d))


def _mm_kernel(x_ref, w_ref, o_ref):
    o_ref[...] = _dot(x_ref[...], w_ref[...]).astype(o_ref.dtype)


def _mm_acc_kernel(x_ref, w_ref, o_ref, acc_ref, *, nk):
    k = pl.program_id(2)

    @pl.when(k == 0)
    def _():
        acc_ref[...] = jnp.zeros_like(acc_ref)

    acc_ref[...] += _dot(x_ref[...], w_ref[...])

    @pl.when(k == nk - 1)
    def _():
        o_ref[...] = acc_ref[...].astype(o_ref.dtype)


def matmul(x, w, tm, tn, tk=None, out_dtype=BF16, name="matmul"):
    m, kdim = x.shape
    n = w.shape[1]
    tk = kdim if tk is None else tk
    nk = kdim // tk
    assert m % tm == 0 and n % tn == 0 and kdim % tk == 0
    if nk == 1:
        return pl.pallas_call(
            _mm_kernel,
            grid=(m // tm, n // tn),
            in_specs=[pl.BlockSpec((tm, kdim), lambda i, j: (i, 0)),
                      pl.BlockSpec((kdim, tn), lambda i, j: (0, j))],
            out_specs=pl.BlockSpec((tm, tn), lambda i, j: (i, j)),
            out_shape=jax.ShapeDtypeStruct((m, n), out_dtype),
            compiler_params=_cparams(("parallel", "parallel")),
            name=name,
        )(x, w)
    return pl.pallas_call(
        functools.partial(_mm_acc_kernel, nk=nk),
        grid=(m // tm, n // tn, nk),
        in_specs=[pl.BlockSpec((tm, tk), lambda i, j, k: (i, k)),
                  pl.BlockSpec((tk, tn), lambda i, j, k: (k, j))],
        out_specs=pl.BlockSpec((tm, tn), lambda i, j, k: (i, j)),
        out_shape=jax.ShapeDtypeStruct((m, n), out_dtype),
        scratch_shapes=[pltpu.VMEM((tm, tn), F32)],
        compiler_params=_cparams(("parallel", "parallel", "arbitrary")),
        name=name,
    )(x, w)


def _mm_wstat_kernel(x_ref, w_ref, o_ref, wb_ref):
    @pl.when(pl.program_id(1) == 0)
    def _():
        wb_ref[...] = w_ref[...].astype(BF16)

    o_ref[...] = _dot(x_ref[...], wb_ref[...]).astype(o_ref.dtype)


def matmul_wstat(x, w, layer, tm, tn, out_dtype=BF16, name="matmul_wstat"):
    m, kdim = x.shape
    n = w.shape[2]
    assert m % tm == 0 and n % tn == 0
    return pl.pallas_call(
        _mm_wstat_kernel,
        grid=(n // tn, m // tm),
        in_specs=[pl.BlockSpec((tm, kdim), lambda j, i: (i, 0)),
                  pl.BlockSpec((None, kdim, tn), lambda j, i: (layer, 0, j))],
        out_specs=pl.BlockSpec((tm, tn), lambda j, i: (i, j)),
        out_shape=jax.ShapeDtypeStruct((m, n), out_dtype),
        scratch_shapes=[pltpu.VMEM((kdim, tn), BF16)],
        compiler_params=_cparams(("parallel", "arbitrary")),
        name=name,
    )(x, w)


def _dsa_prep_kernel(ckv_ref, kw_ref, kvg_ref, wuk_ref, wuv_ref, lng_ref, lnb_ref,
                     k_ref, v_ref, kidx_ref, wi_ref, *, k_scale, w_scale):
    c = ckv_ref[...].astype(F32)
    c = c * lax.rsqrt(jnp.mean(c * c, axis=-1, keepdims=True) + NORM_EPS) * kvg_ref[...]
    cb = c.astype(BF16)
    k_ref[...] = (_dot(cb, wuk_ref[...]) * k_scale).astype(k_ref.dtype)
    v_ref[:, :HEAD_DIM] = _dot(cb, wuv_ref[...]).astype(v_ref.dtype)
    v_ref[:, HEAD_DIM:] = jnp.ones((v_ref.shape[0], HEAD_DIM), v_ref.dtype)
    kw = kw_ref[...].astype(F32)
    ki = kw[:, :HEAD_DIM]
    mu = jnp.mean(ki, axis=-1, keepdims=True)
    kc = ki - mu
    kn = kc * lax.rsqrt(jnp.mean(kc * kc, axis=-1, keepdims=True) + NORM_EPS)
    kidx_ref[...] = (kn * lng_ref[...] + lnb_ref[...]).astype(kidx_ref.dtype)
    wi_ref[...] = kw[:, HEAD_DIM:] * w_scale


def dsa_prep(proj, lay, kv_norm_g, w_uk, w_uv, ln_g, ln_b, idx_heads, rows=512):
    n = proj.shape[0]
    kvl = w_uk.shape[0]
    row = lambda i: (i, 0)
    const = lambda i: (0, 0)
    out = jax.ShapeDtypeStruct((n, HEAD_DIM), BF16)
    return pl.pallas_call(
        functools.partial(_dsa_prep_kernel, k_scale=HEAD_DIM ** -0.5 * LOG2E,
                          w_scale=idx_heads ** -0.5 * HEAD_DIM ** -0.5),
        grid=(n // rows,),
        in_specs=[pl.BlockSpec((rows, kvl), lambda i: (i, lay["c_kv"] // kvl)),
                  pl.BlockSpec((rows, 2 * HEAD_DIM), lambda i: (i, lay["k_i"] // (2 * HEAD_DIM))),
                  pl.BlockSpec((1, kvl), const),
                  pl.BlockSpec((kvl, HEAD_DIM), const),
                  pl.BlockSpec((kvl, HEAD_DIM), const),
                  pl.BlockSpec((1, HEAD_DIM), const),
                  pl.BlockSpec((1, HEAD_DIM), const)],
        out_specs=[pl.BlockSpec((rows, HEAD_DIM), row), pl.BlockSpec((rows, 2 * HEAD_DIM), row),
                   pl.BlockSpec((rows, HEAD_DIM), row), pl.BlockSpec((rows, HEAD_DIM), row)],
        out_shape=[out, jax.ShapeDtypeStruct((n, 2 * HEAD_DIM), BF16), out,
                   jax.ShapeDtypeStruct((n, HEAD_DIM), F32)],
        compiler_params=_cparams(("parallel",)),
        name="dsa_prep",
    )(proj, proj, kv_norm_g.reshape(1, kvl), w_uk.astype(BF16), w_uv.astype(BF16),
      ln_g.reshape(1, HEAD_DIM), ln_b.reshape(1, HEAD_DIM))


def _sortable_key(x):
    u = lax.bitcast_convert_type(x, jnp.int32)
    return u ^ ((u >> 31) & jnp.int32(0x7FFFFFFF))


def _dsa_kernel(qa_ref, qi_ref, wi_ref, kidx_ref, k_ref, v_ref, bias_ref, o_ref,
                qas_ref, qis_ref, keys_ref, mask_ref, m_ref, acc_ref, *, heads, idx_heads, topk):
    qb = ATT_BLOCK
    ck = ATT_BLOCK
    j = pl.program_id(1)

    for h in range(heads):
        qas_ref[h * qb:(h + 1) * qb, :] = qa_ref[:, h * HEAD_DIM:(h + 1) * HEAD_DIM]
    for h in range(idx_heads):
        qis_ref[h * qb:(h + 1) * qb, :] = qi_ref[:, h * HEAD_DIM:(h + 1) * HEAD_DIM]

    row_pos = j * qb + lax.broadcasted_iota(jnp.int32, (qb, ck), 0)
    lane = lax.broadcasted_iota(jnp.int32, (qb, ck), 1)
    hg = 8 if idx_heads % 8 == 0 else idx_heads

    def score_chunk(c, carry):
        kc = kidx_ref[pl.ds(pl.multiple_of(c * ck, ck), ck), :]
        acc = jnp.zeros((qb, ck), F32)
        for g in range(idx_heads // hg):
            s = _dot_nt(qis_ref[g * hg * qb:(g + 1) * hg * qb, :], kc)
            for hh in range(hg):
                h = g * hg + hh
                acc = acc + jnp.maximum(s[hh * qb:(hh + 1) * qb, :], 0.0) * wi_ref[:, h:h + 1]
        valid = (c * ck + lane) <= row_pos
        keys_ref[c] = jnp.where(valid, _sortable_key(acc), jnp.int32(INT_MIN))
        return carry

    lax.fori_loop(0, j + 1, score_chunk, 0)

    def bit_step(i, lo):
        cand = lo + lax.shift_left(jnp.int32(1), jnp.int32(31) - i)

        def count_chunk(c, part):
            ge = (keys_ref[c] >= cand).astype(jnp.int32)
            for t in range(ck // LANES):
                part = part + ge[:, t * LANES:(t + 1) * LANES]
            return part

        part = lax.fori_loop(0, j + 1, count_chunk, jnp.zeros((qb, LANES), jnp.int32))
        cnt = jnp.sum(part, axis=-1, keepdims=True)
        return jnp.where(cnt >= topk, cand, lo)

    lo = lax.fori_loop(0, 32, bit_step, jnp.full((qb, 1), INT_MIN, jnp.int32))
    thr = jnp.maximum(lo, jnp.int32(INT_MIN + 1))

    m_ref[...] = jnp.full_like(m_ref, M_INIT)
    acc_ref[...] = jnp.zeros_like(acc_ref)

    def attend(c, bias_idx):
        off = pl.multiple_of(c * ck, ck)
        kc = k_ref[pl.ds(off, ck), :]
        vc = v_ref[pl.ds(off, ck), :]
        mask_ref[...] = jnp.where(keys_ref[c] >= thr, 0.0, MASK_NEG)
        for h in range(heads):
            rows = slice(h * qb, (h + 1) * qb)
            s = _dot_nt(qas_ref[rows, :], kc) + mask_ref[...]
            if bias_idx is not None:
                s = s + bias_ref[bias_idx, rows, :]
            parts = [s[:, t * LANES:(t + 1) * LANES] for t in range(ck // LANES)]
            smax = functools.reduce(jnp.maximum, parts)
            m_prev = m_ref[rows, :]
            m_new = jnp.maximum(m_prev, jnp.max(smax, axis=-1, keepdims=True))
            p = jnp.concatenate([jnp.exp2(x - m_new) for x in parts], axis=1).astype(BF16)
            alpha = jnp.exp2(m_prev - m_new)
            pv = _dot(p, vc)
            acc_ref[rows, :HEAD_DIM] = alpha * acc_ref[rows, :HEAD_DIM] + pv[:, :HEAD_DIM]
            acc_ref[rows, HEAD_DIM:] = alpha * acc_ref[rows, HEAD_DIM:] + pv[:, HEAD_DIM:]
            m_ref[rows, :] = m_new

    def far_chunk(c, carry):
        attend(c, None)
        return carry

    lax.fori_loop(0, j - 1, far_chunk, 0)

    @pl.when(j >= 1)
    def _():
        attend(j - 1, 1)

    attend(j, 0)

    for h in range(heads):
        rows = slice(h * qb, (h + 1) * qb)
        o = acc_ref[rows, :HEAD_DIM] / acc_ref[rows, HEAD_DIM:]
        o_ref[:, h * HEAD_DIM:(h + 1) * HEAD_DIM] = o.astype(o_ref.dtype)


def dsa_attention(proj, lay, wi, kidx, k, v, bias_tiles, batch, seq, heads, idx_heads):
    qb = ATT_BLOCK
    nqb = seq // qb
    aw = heads * HEAD_DIM
    iw = idx_heads * HEAD_DIM
    topk = min(IDX_TOPK, seq // 4)
    return pl.pallas_call(
        functools.partial(_dsa_kernel, heads=heads, idx_heads=idx_heads, topk=topk),
        grid=(batch, nqb),
        in_specs=[pl.BlockSpec((qb, aw), lambda b, j: (b * nqb + j, lay["q_a"] // aw)),
                  pl.BlockSpec((qb, iw), lambda b, j: (b * nqb + j, lay["q_i"] // iw)),
                  pl.BlockSpec((qb, HEAD_DIM), lambda b, j: (b * nqb + j, 0)),
                  pl.BlockSpec((seq, HEAD_DIM), lambda b, j: (b, 0)),
                  pl.BlockSpec((seq, HEAD_DIM), lambda b, j: (b, 0)),
                  pl.BlockSpec((seq, 2 * HEAD_DIM), lambda b, j: (b, 0)),
                  pl.BlockSpec((2, heads * qb, qb), lambda b, j: (0, 0, 0),
                               pipeline_mode=pl.Buffered(1))],
        out_specs=pl.BlockSpec((qb, aw), lambda b, j: (b * nqb + j, 0)),
        out_shape=jax.ShapeDtypeStruct((batch * seq, aw), BF16),
        scratch_shapes=[pltpu.VMEM((heads * qb, HEAD_DIM), BF16),
                        pltpu.VMEM((idx_heads * qb, HEAD_DIM), BF16),
                        pltpu.VMEM((seq // qb, qb, qb), jnp.int32),
                        pltpu.VMEM((qb, qb), F32),
                        pltpu.VMEM((heads * qb, LANES), F32),
                        pltpu.VMEM((heads * qb, 2 * HEAD_DIM), F32)],
        compiler_params=_cparams(("parallel", "arbitrary")),
        name="dsa_attention",
    )(proj, proj, wi, kidx, k, v, bias_tiles)


def _t5_bucket(dist):
    max_exact = REL_BUCKETS // 2
    d = jnp.maximum(dist, 0)
    df = jnp.maximum(d, 1).astype(F32)
    large = max_exact + (jnp.log(df / max_exact) / math.log(REL_MAX_DIST / max_exact)
                         * (REL_BUCKETS - max_exact)).astype(jnp.int32)
    large = jnp.minimum(large, REL_BUCKETS - 1)
    return jnp.where(d < max_exact, d, large)


def rel_bias_tiles(rel_bias, heads):
    qb = ATT_BLOCK
    assert _static_far_bucket(qb) == REL_BUCKETS - 1
    tbl = (rel_bias[_t5_bucket(jnp.arange(4 * qb))] - rel_bias[REL_BUCKETS - 1][None, :]).T * LOG2E
    y = jnp.arange(2 * qb)
    tiles = []
    for off in (0, qb):
        d = off + jnp.where(y < qb, -y, 2 * qb - y)
        u = tbl[:, jnp.maximum(d, 0)]
        flat = jnp.broadcast_to(u[:, None, :], (heads, qb, 2 * qb)).reshape(heads, 2 * qb * qb)
        skew = flat[:, :qb * (2 * qb - 1)].reshape(heads, qb, 2 * qb - 1)
        tiles.append(skew[:, :, :qb].reshape(heads * qb, qb))
    return jnp.stack(tiles).astype(F32)


def _static_far_bucket(d):
    max_exact = REL_BUCKETS // 2
    v = max_exact + int(math.log(d / max_exact) / math.log(REL_MAX_DIST / max_exact)
                        * (REL_BUCKETS - max_exact))
    return min(v, REL_BUCKETS - 1)


def _hgrn_kernel(q_ref, f_ref, i_ref, g_ref, lb_ref, ng_ref, o_ref, st_ref, *, heads_per_step, tb):
    c = HGRN_CHUNK
    sub = HGRN_SUB
    nsub = c // sub

    @pl.when(pl.program_id(2) == 0)
    def _():
        st_ref[...] = jnp.zeros_like(st_ref)

    r_io = lax.broadcasted_iota(jnp.int32, (c, c), 0)
    c_io = lax.broadcasted_iota(jnp.int32, (c, c), 1)
    tri = (c_io <= r_io).astype(F32)
    off_mask = c_io < (r_io // sub) * sub
    sub_r = lax.broadcasted_iota(jnp.int32, (sub, c), 0)
    sub_c = lax.broadcasted_iota(jnp.int32, (sub, c), 1)
    ng = ng_ref[...]

    def chunk(ci, carry):
        r0 = pl.multiple_of(ci * c, c)
        for hh in range(heads_per_step):
            cols = slice(hh * HEAD_DIM, (hh + 1) * HEAD_DIM)
            lb = lb_ref[:, cols]
            z = f_ref[pl.ds(r0, c), cols].astype(F32)
            qx = q_ref[pl.ds(r0, c), cols].astype(F32)
            v = i_ref[pl.ds(r0, c), cols].astype(F32)
            gx = g_ref[pl.ds(r0, c), cols].astype(F32)
            sig = jax.nn.sigmoid(z)
            lf = jnp.log2(lb + (1.0 - lb) * sig)
            kf = (1.0 - lb) * jax.nn.sigmoid(-z)
            qf = qx * jax.nn.sigmoid(qx)
            a = jnp.dot(tri, lf, preferred_element_type=F32, precision=lax.Precision.HIGHEST)
            vb = v.astype(BF16)

            p_rows = []
            for sb in range(nsub):
                rs = slice(sb * sub, (sb + 1) * sub)
                a_blk = a[rs, :]
                q_blk = qf[rs, :]
                a_start = a[sb * sub - 1:sb * sub, :] if sb > 0 else jnp.zeros((1, HEAD_DIM), F32)
                q_t = q_blk * jnp.exp2(a_blk - a_start)
                k_t = kf * jnp.exp2(a_start - a)
                p_off = _dot_nt(q_t.astype(BF16), k_t.astype(BF16))
                p_diag = jnp.zeros((sub, c), F32)
                t_io = lax.broadcasted_iota(jnp.int32, (sub, HEAD_DIM), 0)
                for s in range(sub):
                    a_s = a[sb * sub + s:sb * sub + s + 1, :]
                    k_s = kf[sb * sub + s:sb * sub + s + 1, :]
                    e = jnp.exp2(a_blk - a_s)
                    col = jnp.sum(q_blk * e * k_s, axis=-1, keepdims=True)
                    p_diag = jnp.where((sub_c == sb * sub + s) & (sub_r >= s), col, p_diag)
                p_rows.append(jnp.where(sub_c < sb * sub, p_off, p_diag))
            p = jnp.concatenate(p_rows, axis=0)
            st = st_ref[hh]
            o = _dot(p.astype(BF16), vb) + _dot_nt((qf * jnp.exp2(a)).astype(BF16), st.astype(BF16))
            a_last = a[c - 1:c, :]
            k_end = kf * jnp.exp2(a_last - a)
            st_ref[hh] = st * jnp.exp2(a_last) + _dot(v.T.astype(BF16), k_end.astype(BF16))
            o = o * lax.rsqrt(jnp.mean(o * o, axis=-1, keepdims=True) + NORM_EPS) * ng
            o_ref[pl.ds(r0, c), cols] = (o * (gx * jax.nn.sigmoid(gx))).astype(o_ref.dtype)
        return carry

    lax.fori_loop(0, tb // c, chunk, 0)


def hgrn2(proj, lay, lb, norm_g, batch, seq, heads, heads_per_step=4, tb=512):
    hw = heads_per_step * HEAD_DIM
    nt = seq // tb
    ng = heads // heads_per_step

    def col(name):
        base = lay[name] // hw
        return lambda b, h, t: (b * nt + t, base + h)

    return pl.pallas_call(
        functools.partial(_hgrn_kernel, heads_per_step=heads_per_step, tb=tb),
        grid=(batch, ng, nt),
        in_specs=[pl.BlockSpec((tb, hw), col("q_h")),
                  pl.BlockSpec((tb, hw), col("f_h")),
                  pl.BlockSpec((tb, hw), col("i_h")),
                  pl.BlockSpec((tb, hw), col("g_h")),
                  pl.BlockSpec((1, hw), lambda b, h, t: (0, h)),
                  pl.BlockSpec((1, HEAD_DIM), lambda b, h, t: (0, 0))],
        out_specs=pl.BlockSpec((tb, hw), lambda b, h, t: (b * nt + t, h)),
        out_shape=jax.ShapeDtypeStruct((batch * seq, heads * HEAD_DIM), BF16),
        scratch_shapes=[pltpu.VMEM((heads_per_step, HEAD_DIM, HEAD_DIM), F32)],
        compiler_params=_cparams(("parallel", "parallel", "arbitrary")),
        name="hgrn2",
    )(proj, proj, proj, proj, lb.reshape(1, -1), norm_g.reshape(1, HEAD_DIM))


def _merge_kernel(a_ref, b_ref, wa_ref, wb_ref, ga_ref, gb_ref, o_ref):
    ya = _dot(a_ref[...], wa_ref[...])
    yb = _dot(b_ref[...], wb_ref[...])
    ga = jax.nn.sigmoid(ga_ref[...].astype(F32))
    gb = jax.nn.sigmoid(gb_ref[...].astype(F32))
    o_ref[...] = (ga * ya + gb * yb).astype(o_ref.dtype)


def gated_merge(attn, hg, w_pa, w_pb, proj, lay, tm=1024, tn=512):
    m, ka = attn.shape
    kb = hg.shape[1]
    d = w_pa.shape[1]
    ga0, gb0 = lay["gate_a"] // tn, lay["gate_b"] // tn
    return pl.pallas_call(
        _merge_kernel,
        grid=(m // tm, d // tn),
        in_specs=[pl.BlockSpec((tm, ka), lambda i, j: (i, 0)),
                  pl.BlockSpec((tm, kb), lambda i, j: (i, 0)),
                  pl.BlockSpec((ka, tn), lambda i, j: (0, j)),
                  pl.BlockSpec((kb, tn), lambda i, j: (0, j)),
                  pl.BlockSpec((tm, tn), lambda i, j: (i, ga0 + j)),
                  pl.BlockSpec((tm, tn), lambda i, j: (i, gb0 + j))],
        out_specs=pl.BlockSpec((tm, tn), lambda i, j: (i, j)),
        out_shape=jax.ShapeDtypeStruct((m, d), BF16),
        compiler_params=_cparams(("parallel", "parallel")),
        name="gated_merge",
    )(attn, hg, w_pa, w_pb, proj, proj)


def _norm_residual_kernel(x_ref, y_ref, gp_ref, gn_ref, xo_ref, ho_ref):
    y = y_ref[...]
    yn = y * lax.rsqrt(jnp.mean(y * y, axis=-1, keepdims=True) + NORM_EPS) * gp_ref[...]
    x = x_ref[...] + yn
    xo_ref[...] = x
    ho_ref[...] = (x * lax.rsqrt(jnp.mean(x * x, axis=-1, keepdims=True) + NORM_EPS)
                   * gn_ref[...]).astype(ho_ref.dtype)


def norm_residual(x, y, g_post, g_next, rows=256):
    n, d = x.shape
    row = lambda i: (i, 0)
    const = lambda i: (0, 0)
    return pl.pallas_call(
        _norm_residual_kernel,
        grid=(n // rows,),
        in_specs=[pl.BlockSpec((rows, d), row), pl.BlockSpec((rows, d), row),
                  pl.BlockSpec((1, d), const), pl.BlockSpec((1, d), const)],
        out_specs=[pl.BlockSpec((rows, d), row), pl.BlockSpec((rows, d), row)],
        out_shape=[jax.ShapeDtypeStruct((n, d), F32), jax.ShapeDtypeStruct((n, d), BF16)],
        compiler_params=_cparams(("parallel",)),
        name="norm_residual",
    )(x, y, g_post.reshape(1, d), g_next.reshape(1, d))


GELU_C = math.sqrt(2.0 / math.pi)


def _gelu_tanh(x):
    z = x * (x * x * (-2.0 * GELU_C * 0.044715 * LOG2E) + (-2.0 * GELU_C * LOG2E))
    return x / (1.0 + jnp.exp2(z))


def _ffn_up_kernel(x_ref, wg_ref, wv_ref, cwg_ref, cwv_ref, cbg_ref, cbv_ref, o_ref,
                   wb_ref, u_ref, *, tm, tn, seq):
    i = pl.program_id(1)
    starts_seq = (i * tm) % seq == 0

    @pl.when(i == 0)
    def _():
        wb_ref[:, :tn] = wg_ref[...].astype(BF16)
        wb_ref[:, tn:] = wv_ref[...].astype(BF16)

    @pl.when(starts_seq)
    def _():
        u_ref[0:8, :] = jnp.zeros((8, 2 * tn), F32)

    @pl.when(jnp.logical_not(starts_seq))
    def _():
        u_ref[0:8, :] = u_ref[tm:tm + 8, :]

    u_ref[8:8 + tm, :] = _dot(x_ref[...], wb_ref[...])

    def conv(lo, w_ref, b_ref):
        cols = slice(lo, lo + tn)
        return (b_ref[...] + w_ref[0:1, :] * u_ref[6:6 + tm, cols]
                + w_ref[1:2, :] * u_ref[7:7 + tm, cols] + w_ref[2:3, :] * u_ref[8:8 + tm, cols])

    gate = conv(0, cwg_ref, cbg_ref)
    val = conv(tn, cwv_ref, cbv_ref)
    o_ref[...] = (_gelu_tanh(gate) * val).astype(o_ref.dtype)


def ffn_up_conv_geglu(h, w_up, conv_w, conv_b, layer, seq, tm, tn):
    m, kdim = h.shape
    dff = w_up.shape[2] // 2
    assert m % tm == 0 and dff % tn == 0 and seq % tm == 0
    nc = dff // tn
    wspec = lambda off: pl.BlockSpec((None, kdim, tn), lambda j, i: (layer, 0, off + j))
    cspec = lambda r, off: pl.BlockSpec((None, r, tn), lambda j, i: (layer, 0, off + j))
    return pl.pallas_call(
        functools.partial(_ffn_up_kernel, tm=tm, tn=tn, seq=seq),
        grid=(nc, m // tm),
        in_specs=[pl.BlockSpec((tm, kdim), lambda j, i: (i, 0)),
                  wspec(0), wspec(nc),
                  cspec(CONV_WIDTH, 0), cspec(CONV_WIDTH, nc),
                  cspec(1, 0), cspec(1, nc)],
        out_specs=pl.BlockSpec((tm, tn), lambda j, i: (i, j)),
        out_shape=jax.ShapeDtypeStruct((m, dff), BF16),
        scratch_shapes=[pltpu.VMEM((kdim, 2 * tn), BF16), pltpu.VMEM((tm + 8, 2 * tn), F32)],
        compiler_params=_cparams(("parallel", "arbitrary")),
        name="ffn_up",
    )(h, w_up, w_up, conv_w, conv_w, conv_b[:, None, :], conv_b[:, None, :])


def _in_layout(d, aw, iw, hk, hv, kvl):
    segs = [("gate_a", d), ("gate_b", d), ("q_i", iw), ("q_a", aw), ("q_h", hk), ("f_h", hk),
            ("i_h", hv), ("g_h", hv), ("c_kv", kvl), ("k_i", HEAD_DIM), ("w_i", HEAD_DIM)]
    lay, off = {}, 0
    for name, w in segs:
        assert off % w == 0, (name, off, w)
        lay[name] = off
        off += w
    lay["pad"] = (-off) % IN_PROJ_TN
    lay["total"] = off + lay["pad"]
    return lay


def _reorder_w_in(w, lay, d, aw, iw, ih, hk, hv, kvl):
    splits = np.cumsum([aw, kvl, iw, HEAD_DIM, ih, hk, hk, hv, hv, d, d])[:-1].tolist()
    q_a, c_kv, q_i, k_i, w_i, q_h, f_h, i_h, g_h, gate_a, gate_b = jnp.split(w, splits, axis=1)
    w_i = jnp.pad(w_i, ((0, 0), (0, HEAD_DIM - ih + lay["pad"])))
    out = jnp.concatenate([gate_a, gate_b, q_i, q_a, q_h, f_h, i_h, g_h, c_kv, k_i, w_i], axis=1)
    assert out.shape[1] == lay["total"]
    return out.astype(BF16)


def _pick(n, cands):
    for c in cands:
        if n % c == 0:
            return c
    raise ValueError(f"no tile for {n}")


def kernel(x, rel_bias, hgrn_lb_logits, mix_pre_g, mix_post_g, w_in, kv_norm_g, w_uk, w_uv, idx_k_ln_g, idx_k_ln_b, hgrn_norm_g, w_proj_attn, w_proj_hgrn, w_out, ffn_pre_g, ffn_post_g, w_up, conv_w, conv_b, w_down):
    batch, seq, d = x.shape
    depth = w_in.shape[0]
    aw = w_proj_attn.shape[1]
    hv = w_proj_hgrn.shape[1]
    hk = hgrn_lb_logits.shape[1]
    kvl = w_uk.shape[1]
    dff = w_down.shape[1]
    heads = aw // HEAD_DIM
    hheads = hv // HEAD_DIM
    ih = w_in.shape[2] - (aw + kvl + HEAD_DIM + 2 * hk + 2 * hv + 2 * d)
    ih = ih // (HEAD_DIM + 1)
    iw = ih * HEAD_DIM
    lay = _in_layout(d, aw, iw, hk, hv, kvl)
    n = batch * seq

    lb_p = jax.nn.softmax(hgrn_lb_logits.astype(F32), axis=0)
    lb_all = jnp.cumsum(lb_p, axis=0) - lb_p[0:1]
    bias_tiles = rel_bias_tiles(rel_bias, heads)

    tm = _pick(n, (1024, 512, 256))
    xf = x.reshape(n, d)
    h = rmsnorm_cast(xf, mix_pre_g[0])
    for l in range(depth):
        w_in_l = _reorder_w_in(w_in[l], lay, d, aw, iw, ih, hk, hv, kvl)
        proj = matmul(h, w_in_l, tm, IN_PROJ_TN, name="in_proj")
        k, v, kidx, wi = dsa_prep(proj, lay, kv_norm_g[l], w_uk[l], w_uv[l],
                                  idx_k_ln_g[l], idx_k_ln_b[l], ih)
        attn = dsa_attention(proj, lay, wi, kidx, k, v, bias_tiles, batch, seq, heads, ih)
        hg = hgrn2(proj, lay, lb_all[l], hgrn_norm_g[l], batch, seq, hheads)
        merged = gated_merge(attn, hg, w_proj_attn[l].astype(BF16), w_proj_hgrn[l].astype(BF16),
                             proj, lay, tm=tm)
        y = matmul_wstat(merged, w_out, l, tm, 512, out_dtype=F32, name="out_proj")
        xf, h = norm_residual(xf, y, mix_post_g[l], ffn_pre_g[l])
        act = ffn_up_conv_geglu(h, w_up, conv_w, conv_b, l, seq, tm, _pick(dff, (256, 128)))
        y = matmul(act, w_down[l].astype(BF16), tm, 512, tk=_pick(dff, (dff // 2, dff)),
                   out_dtype=F32, name="ffn_down")
        g_next = mix_pre_g[l + 1] if l + 1 < depth else mix_pre_g[0]
        xf, h = norm_residual(xf, y, ffn_post_g[l], g_next)
    return xf.reshape(batch, seq, d)
```

```python
import functools
import math

import jax
import jax.numpy as jnp
import numpy as np
from jax import lax
from jax.experimental import pallas as pl
from jax.experimental.pallas import tpu as pltpu

F32 = jnp.float32
BF16 = jnp.bfloat16

NORM_EPS = 1e-6
HEAD_DIM = 128
IDX_TOPK = 256
REL_BUCKETS = 32
REL_MAX_DIST = 128
CONV_WIDTH = 3
LANES = 128
VMEM_LIMIT_BYTES = 56 * 1024 * 1024

ATT_BLOCK = 256
HGRN_CHUNK = 64
HGRN_SUB = 16
HGRN_SAFE_LOG2 = 100.0
IN_PROJ_TN = 512
INT_MIN = -2 ** 31
MASK_NEG = -1e30
M_INIT = -1e20
LOG2E = math.log2(math.e)


def _cparams(sem):
    return pltpu.CompilerParams(dimension_semantics=sem, vmem_limit_bytes=VMEM_LIMIT_BYTES)


def _dot(a, b):
    return jnp.dot(a, b, preferred_element_type=F32)


def _dot_nt(a, b):
    return lax.dot_general(a, b, (((1,), (1,)), ((), ())), preferred_element_type=F32)


def _dot_tn(a, b):
    return lax.dot_general(a, b, (((0,), (0,)), ((), ())), preferred_element_type=F32)


def _rmsnorm_kernel(x_ref, g_ref, o_ref):
    x = x_ref[...]
    r = lax.rsqrt(jnp.mean(x * x, axis=-1, keepdims=True) + NORM_EPS)
    o_ref[...] = (x * r * g_ref[...]).astype(o_ref.dtype)


def rmsnorm_cast(x, g, rows=256):
    n, d = x.shape
    return pl.pallas_call(
        _rmsnorm_kernel,
        grid=(n // rows,),
        in_specs=[pl.BlockSpec((rows, d), lambda i: (i, 0)),
                  pl.BlockSpec((1, d), lambda i: (0, 0))],
        out_specs=pl.BlockSpec((rows, d), lambda i: (i, 0)),
        out_shape=jax.ShapeDtypeStruct((n, d), BF16),
        compiler_params=_cparams(("parallel",)),
        name="rmsnorm_cast",
    )(x, g.reshape(1, d))


def _mm_kernel(x_ref, w_ref, o_ref):
    o_ref[...] = _dot(x_ref[...], w_ref[...]).astype(o_ref.dtype)


def _mm_acc_kernel(x_ref, w_ref, o_ref, acc_ref, *, nk):
    k = pl.program_id(2)

    @pl.when(k == 0)
    def _():
        acc_ref[...] = jnp.zeros_like(acc_ref)

    acc_ref[...] += _dot(x_ref[...], w_ref[...])

    @pl.when(k == nk - 1)
    def _():
        o_ref[...] = acc_ref[...].astype(o_ref.dtype)


def matmul(x, w, tm, tn, tk=None, out_dtype=BF16, name="matmul"):
    m, kdim = x.shape
    n = w.shape[1]
    tk = kdim if tk is None else tk
    nk = kdim // tk
    assert m % tm == 0 and n % tn == 0 and kdim % tk == 0
    if nk == 1:
        return pl.pallas_call(
            _mm_kernel,
            grid=(m // tm, n // tn),
            in_specs=[pl.BlockSpec((tm, kdim), lambda i, j: (i, 0)),
                      pl.BlockSpec((kdim, tn), lambda i, j: (0, j))],
            out_specs=pl.BlockSpec((tm, tn), lambda i, j: (i, j)),
            out_shape=jax.ShapeDtypeStruct((m, n), out_dtype),
            compiler_params=_cparams(("parallel", "parallel")),
            name=name,
        )(x, w)
    return pl.pallas_call(
        functools.partial(_mm_acc_kernel, nk=nk),
        grid=(m // tm, n // tn, nk),
        in_specs=[pl.BlockSpec((tm, tk), lambda i, j, k: (i, k)),
                  pl.BlockSpec((tk, tn), lambda i, j, k: (k, j))],
        out_specs=pl.BlockSpec((tm, tn), lambda i, j, k: (i, j)),
        out_shape=jax.ShapeDtypeStruct((m, n), out_dtype),
        scratch_shapes=[pltpu.VMEM((tm, tn), F32)],
        compiler_params=_cparams(("parallel", "parallel", "arbitrary")),
        name=name,
    )(x, w)


def _mm_wstat_kernel(x_ref, w_ref, o_ref, wb_ref):
    @pl.when(pl.program_id(1) == 0)
    def _():
        wb_ref[...] = w_ref[...].astype(BF16)

    o_ref[...] = _dot(x_ref[...], wb_ref[...]).astype(o_ref.dtype)


def matmul_wstat(x, w, layer, tm, tn, out_dtype=BF16, name="matmul_wstat"):
    m, kdim = x.shape
    n = w.shape[2]
    assert m % tm == 0 and n % tn == 0
    return pl.pallas_call(
        _mm_wstat_kernel,
        grid=(n // tn, m // tm),
        in_specs=[pl.BlockSpec((tm, kdim), lambda j, i: (i, 0)),
                  pl.BlockSpec((None, kdim, tn), lambda j, i: (layer, 0, j))],
        out_specs=pl.BlockSpec((tm, tn), lambda j, i: (i, j)),
        out_shape=jax.ShapeDtypeStruct((m, n), out_dtype),
        scratch_shapes=[pltpu.VMEM((kdim, tn), BF16)],
        compiler_params=_cparams(("parallel", "arbitrary")),
        name=name,
    )(x, w)


def _dsa_prep_kernel(ckv_ref, kw_ref, kvg_ref, wuk_ref, wuv_ref, lng_ref, lnb_ref,
                     k_ref, v_ref, kidx_ref, wi_ref, *, k_scale, w_scale):
    c = ckv_ref[...].astype(F32)
    c = c * lax.rsqrt(jnp.mean(c * c, axis=-1, keepdims=True) + NORM_EPS) * kvg_ref[...]
    cb = c.astype(BF16)
    k_ref[...] = (_dot(cb, wuk_ref[...]) * k_scale).astype(k_ref.dtype)
    v_ref[:, :HEAD_DIM] = _dot(cb, wuv_ref[...]).astype(v_ref.dtype)
    v_ref[:, HEAD_DIM:] = jnp.ones((v_ref.shape[0], HEAD_DIM), v_ref.dtype)
    kw = kw_ref[...].astype(F32)
    ki = kw[:, :HEAD_DIM]
    mu = jnp.mean(ki, axis=-1, keepdims=True)
    kc = ki - mu
    kn = kc * lax.rsqrt(jnp.mean(kc * kc, axis=-1, keepdims=True) + NORM_EPS)
    kidx_ref[...] = (kn * lng_ref[...] + lnb_ref[...]).astype(kidx_ref.dtype)
    wi_ref[...] = kw[:, HEAD_DIM:] * w_scale


def dsa_prep(proj, lay, kv_norm_g, w_uk, w_uv, ln_g, ln_b, idx_heads, rows=512):
    n = proj.shape[0]
    kvl = w_uk.shape[0]
    row = lambda i: (i, 0)
    const = lambda i: (0, 0)
    out = jax.ShapeDtypeStruct((n, HEAD_DIM), BF16)
    return pl.pallas_call(
        functools.partial(_dsa_prep_kernel, k_scale=HEAD_DIM ** -0.5 * LOG2E,
                          w_scale=idx_heads ** -0.5 * HEAD_DIM ** -0.5),
        grid=(n // rows,),
        in_specs=[pl.BlockSpec((rows, kvl), lambda i: (i, lay["c_kv"] // kvl)),
                  pl.BlockSpec((rows, 2 * HEAD_DIM), lambda i: (i, lay["k_i"] // (2 * HEAD_DIM))),
                  pl.BlockSpec((1, kvl), const),
                  pl.BlockSpec((kvl, HEAD_DIM), const),
                  pl.BlockSpec((kvl, HEAD_DIM), const),
                  pl.BlockSpec((1, HEAD_DIM), const),
                  pl.BlockSpec((1, HEAD_DIM), const)],
        out_specs=[pl.BlockSpec((rows, HEAD_DIM), row), pl.BlockSpec((rows, 2 * HEAD_DIM), row),
                   pl.BlockSpec((rows, HEAD_DIM), row), pl.BlockSpec((rows, HEAD_DIM), row)],
        out_shape=[out, jax.ShapeDtypeStruct((n, 2 * HEAD_DIM), BF16), out,
                   jax.ShapeDtypeStruct((n, HEAD_DIM), F32)],
        compiler_params=_cparams(("parallel",)),
        name="dsa_prep",
    )(proj, proj, kv_norm_g.reshape(1, kvl), w_uk.astype(BF16), w_uv.astype(BF16),
      ln_g.reshape(1, HEAD_DIM), ln_b.reshape(1, HEAD_DIM))


def _sortable_key(x):
    u = lax.bitcast_convert_type(x, jnp.int32)
    return u ^ ((u >> 31) & jnp.int32(0x7FFFFFFF))


def _dsa_kernel(qa_ref, qi_ref, wi_ref, kidx_ref, k_ref, v_ref, bias_ref, o_ref,
                qas_ref, qis_ref, keys_ref, khi_ref, klo_ref, mask_ref, m_ref, acc_ref,
                *, heads, idx_heads, topk):
    qb = ATT_BLOCK
    ck = ATT_BLOCK
    j = pl.program_id(1)

    for h in range(heads):
        qas_ref[h * qb:(h + 1) * qb, :] = qa_ref[:, h * HEAD_DIM:(h + 1) * HEAD_DIM]
    for h in range(idx_heads):
        qis_ref[h * qb:(h + 1) * qb, :] = qi_ref[:, h * HEAD_DIM:(h + 1) * HEAD_DIM]

    row_pos = j * qb + lax.broadcasted_iota(jnp.int32, (qb, ck), 0)
    lane = lax.broadcasted_iota(jnp.int32, (qb, ck), 1)
    hg = 8 if idx_heads % 8 == 0 else idx_heads

    def score_chunk(c, carry):
        kc = kidx_ref[pl.ds(pl.multiple_of(c * ck, ck), ck), :]
        acc = jnp.zeros((qb, ck), F32)
        for g in range(idx_heads // hg):
            s = _dot_nt(qis_ref[g * hg * qb:(g + 1) * hg * qb, :], kc)
            for hh in range(hg):
                h = g * hg + hh
                acc = acc + jnp.maximum(s[hh * qb:(hh + 1) * qb, :], 0.0) * wi_ref[:, h:h + 1]
        valid = (c * ck + lane) <= row_pos
        key = jnp.where(valid, _sortable_key(acc), jnp.int32(INT_MIN))
        keys_ref[c] = key
        khi_ref[c] = (key >> 16).astype(jnp.int16)
        klo_ref[c] = ((key & 0xFFFF) - 0x8000).astype(jnp.int16)
        return carry

    lax.fori_loop(0, j + 1, score_chunk, 0)

    ones = jnp.ones((LANES, LANES), BF16)
    i16_min = jnp.int16(-0x8000)
    one16, zero16 = jnp.ones((), BF16), jnp.zeros((), BF16)
    tiles = [slice(t * LANES, (t + 1) * LANES) for t in range(ck // LANES)]

    def widen(x):
        return jnp.concatenate([x] * (ck // LANES), axis=1)

    def row_total(part):
        return _dot(part, ones)

    def search16(ref, need):
        def bit_step(i, lo):
            cand = lo + lax.shift_left(jnp.int32(1), jnp.int32(15) - i)
            c16 = cand.astype(jnp.int16)

            def count_chunk(c, part):
                for t in tiles:
                    part = part + jnp.where(ref[c, :, t] >= c16, one16, zero16)
                return part

            part = lax.fori_loop(0, j + 1, count_chunk, jnp.zeros((qb, LANES), BF16))
            return jnp.where(row_total(part) >= need, cand, lo)
        return lax.fori_loop(0, 16, bit_step, jnp.full((qb, LANES), -0x8000, jnp.int32))

    hi = search16(khi_ref, jnp.float32(topk))
    hi16 = hi.astype(jnp.int16)

    def split_chunk(c, part):
        for t in tiles:
            x = khi_ref[c, :, t]
            klo_ref[c, :, t] = jnp.where(x == hi16, klo_ref[c, :, t], i16_min)
            part = part + jnp.where(x > hi16, one16, zero16)
        return part

    above = row_total(lax.fori_loop(0, j + 1, split_chunk, jnp.zeros((qb, LANES), BF16)))
    lo = search16(klo_ref, topk - above)
    thr = lax.shift_left(hi, 16) | ((lo + 0x8000) & 0xFFFF)
    thr = widen(jnp.maximum(thr, jnp.int32(INT_MIN + 1)))

    m_ref[...] = jnp.full_like(m_ref, M_INIT)
    acc_ref[...] = jnp.zeros_like(acc_ref)

    def attend(c, bias_idx):
        off = pl.multiple_of(c * ck, ck)
        kc = k_ref[pl.ds(off, ck), :]
        vc = v_ref[pl.ds(off, ck), :]
        mask_ref[...] = jnp.where(keys_ref[c] >= thr, 0.0, MASK_NEG)
        for h in range(heads):
            rows = slice(h * qb, (h + 1) * qb)
            s = _dot_nt(qas_ref[rows, :], kc) + mask_ref[...]
            if bias_idx is not None:
                s = s + bias_ref[bias_idx, rows, :]
            parts = [s[:, t * LANES:(t + 1) * LANES] for t in range(ck // LANES)]
            smax = functools.reduce(jnp.maximum, parts)
            m_prev = m_ref[rows, :]
            m_new = jnp.maximum(m_prev, jnp.max(smax, axis=-1, keepdims=True))
            p = jnp.concatenate([jnp.exp2(x - m_new) for x in parts], axis=1).astype(BF16)
            alpha = jnp.exp2(m_prev - m_new)
            pv = _dot(p, vc)
            acc_ref[rows, :HEAD_DIM] = alpha * acc_ref[rows, :HEAD_DIM] + pv[:, :HEAD_DIM]
            acc_ref[rows, HEAD_DIM:] = alpha * acc_ref[rows, HEAD_DIM:] + pv[:, HEAD_DIM:]
            m_ref[rows, :] = m_new

    def far_chunk(c, carry):
        attend(c, None)
        return carry

    lax.fori_loop(0, j - 1, far_chunk, 0)

    @pl.when(j >= 1)
    def _():
        attend(j - 1, 1)

    attend(j, 0)

    for h in range(heads):
        rows = slice(h * qb, (h + 1) * qb)
        o = acc_ref[rows, :HEAD_DIM] / acc_ref[rows, HEAD_DIM:]
        o_ref[:, h * HEAD_DIM:(h + 1) * HEAD_DIM] = o.astype(o_ref.dtype)


def dsa_attention(proj, lay, wi, kidx, k, v, bias_tiles, batch, seq, heads, idx_heads):
    qb = ATT_BLOCK
    nqb = seq // qb
    aw = heads * HEAD_DIM
    iw = idx_heads * HEAD_DIM
    topk = min(IDX_TOPK, seq // 4)
    return pl.pallas_call(
        functools.partial(_dsa_kernel, heads=heads, idx_heads=idx_heads, topk=topk),
        grid=(batch, nqb),
        in_specs=[pl.BlockSpec((qb, aw), lambda b, j: (b * nqb + j, lay["q_a"] // aw)),
                  pl.BlockSpec((qb, iw), lambda b, j: (b * nqb + j, lay["q_i"] // iw)),
                  pl.BlockSpec((qb, HEAD_DIM), lambda b, j: (b * nqb + j, 0)),
                  pl.BlockSpec((seq, HEAD_DIM), lambda b, j: (b, 0)),
                  pl.BlockSpec((seq, HEAD_DIM), lambda b, j: (b, 0)),
                  pl.BlockSpec((seq, 2 * HEAD_DIM), lambda b, j: (b, 0)),
                  pl.BlockSpec((2, heads * qb, qb), lambda b, j: (0, 0, 0),
                               pipeline_mode=pl.Buffered(1))],
        out_specs=pl.BlockSpec((qb, aw), lambda b, j: (b * nqb + j, 0)),
        out_shape=jax.ShapeDtypeStruct((batch * seq, aw), BF16),
        scratch_shapes=[pltpu.VMEM((heads * qb, HEAD_DIM), BF16),
                        pltpu.VMEM((idx_heads * qb, HEAD_DIM), BF16),
                        pltpu.VMEM((seq // qb, qb, qb), jnp.int32),
                        pltpu.VMEM((seq // qb, qb, qb), jnp.int16),
                        pltpu.VMEM((seq // qb, qb, qb), jnp.int16),
                        pltpu.VMEM((qb, qb), F32),
                        pltpu.VMEM((heads * qb, LANES), F32),
                        pltpu.VMEM((heads * qb, 2 * HEAD_DIM), F32)],
        compiler_params=_cparams(("parallel", "arbitrary")),
        name="dsa_attention",
    )(proj, proj, wi, kidx, k, v, bias_tiles)


def _t5_bucket(dist):
    max_exact = REL_BUCKETS // 2
    d = jnp.maximum(dist, 0)
    df = jnp.maximum(d, 1).astype(F32)
    large = max_exact + (jnp.log(df / max_exact) / math.log(REL_MAX_DIST / max_exact)
                         * (REL_BUCKETS - max_exact)).astype(jnp.int32)
    large = jnp.minimum(large, REL_BUCKETS - 1)
    return jnp.where(d < max_exact, d, large)


def rel_bias_tiles(rel_bias, heads):
    qb = ATT_BLOCK
    assert _static_far_bucket(qb) == REL_BUCKETS - 1
    tbl = (rel_bias[_t5_bucket(jnp.arange(4 * qb))] - rel_bias[REL_BUCKETS - 1][None, :]).T * LOG2E
    y = jnp.arange(2 * qb)
    tiles = []
    for off in (0, qb):
        d = off + jnp.where(y < qb, -y, 2 * qb - y)
        u = tbl[:, jnp.maximum(d, 0)]
        flat = jnp.broadcast_to(u[:, None, :], (heads, qb, 2 * qb)).reshape(heads, 2 * qb * qb)
        skew = flat[:, :qb * (2 * qb - 1)].reshape(heads, qb, 2 * qb - 1)
        tiles.append(skew[:, :, :qb].reshape(heads * qb, qb))
    return jnp.stack(tiles).astype(F32)


def _static_far_bucket(d):
    max_exact = REL_BUCKETS // 2
    v = max_exact + int(math.log(d / max_exact) / math.log(REL_MAX_DIST / max_exact)
                        * (REL_BUCKETS - max_exact))
    return min(v, REL_BUCKETS - 1)


def _hgrn_kernel(q_ref, f_ref, i_ref, g_ref, lb_ref, ng_ref, o_ref,
                 st_ref, a_ref, k_ref, qf_ref, p_ref, *, heads_per_step, tb):
    c = HGRN_CHUNK
    sub = HGRN_SUB
    nsub = c // sub
    hs = range(heads_per_step)

    @pl.when(pl.program_id(2) == 0)
    def _():
        st_ref[...] = jnp.zeros_like(st_ref)

    r_io = lax.broadcasted_iota(jnp.int32, (c, c), 0)
    c_io = lax.broadcasted_iota(jnp.int32, (c, c), 1)
    tri = (c_io <= r_io).astype(F32)
    sub_r = lax.broadcasted_iota(jnp.int32, (sub, c), 0)
    sub_c = lax.broadcasted_iota(jnp.int32, (sub, c), 1)
    ng = ng_ref[...]

    def a_start(hh, sb):
        return a_ref[hh, sb * sub - 1:sb * sub, :] if sb > 0 else jnp.zeros((1, HEAD_DIM), F32)

    def chunk(ci, carry):
        r0 = pl.multiple_of(ci * c, c)

        worst = jnp.zeros((1, HEAD_DIM), F32)
        for hh in hs:
            cols = slice(hh * HEAD_DIM, (hh + 1) * HEAD_DIM)
            lb = lb_ref[:, cols]
            z = f_ref[pl.ds(r0, c), cols].astype(F32)
            qx = q_ref[pl.ds(r0, c), cols].astype(F32)
            sig = jax.nn.sigmoid(z)
            lf = jnp.log2(lb + (1.0 - lb) * sig)
            k_ref[hh] = (1.0 - lb) * jax.nn.sigmoid(-z)
            qf_ref[hh] = qx * jax.nn.sigmoid(qx)
            a = jnp.dot(tri, lf, preferred_element_type=F32, precision=lax.Precision.HIGHEST)
            a_ref[hh] = a
            for sb in range(nsub):
                top = a[sb * sub - 1:sb * sub, :] if sb > 0 else jnp.zeros((1, HEAD_DIM), F32)
                worst = jnp.maximum(worst, top - a[(sb + 1) * sub - 1:(sb + 1) * sub, :])
        safe = jnp.max(worst) <= HGRN_SAFE_LOG2

        @pl.when(safe)
        def _():
            for hh in hs:
                a = a_ref[hh]
                kf = k_ref[hh]
                for sb in range(nsub):
                    rs = slice(sb * sub, (sb + 1) * sub)
                    top = a_start(hh, sb)
                    q_t = qf_ref[hh, rs, :] * jnp.exp2(a_ref[hh, rs, :] - top)
                    k_t = kf * jnp.exp2(top - a)
                    p = _dot_nt(q_t.astype(BF16), k_t.astype(BF16))
                    p_ref[hh, rs, :] = jnp.where(sub_c <= sub_r + sb * sub, p, 0.0)

        @pl.when(jnp.logical_not(safe))
        def _():
            for hh in hs:
                a = a_ref[hh]
                kf = k_ref[hh]
                for sb in range(nsub):
                    rs = slice(sb * sub, (sb + 1) * sub)
                    top = a_start(hh, sb)
                    a_blk = a_ref[hh, rs, :]
                    q_blk = qf_ref[hh, rs, :]
                    q_t = q_blk * jnp.exp2(a_blk - top)
                    k_t = kf * jnp.exp2(top - a)
                    p_off = _dot_nt(q_t.astype(BF16), k_t.astype(BF16))
                    p_diag = jnp.zeros((sub, c), F32)
                    for s in range(sub):
                        row = sb * sub + s
                        e = jnp.exp2(a_blk - a_ref[hh, row:row + 1, :])
                        col = jnp.sum(q_blk * e * k_ref[hh, row:row + 1, :], axis=-1, keepdims=True)
                        p_diag = jnp.where((sub_c == row) & (sub_r >= s), col, p_diag)
                    p_ref[hh, rs, :] = jnp.where(sub_c < sb * sub, p_off, p_diag)

        for hh in hs:
            cols = slice(hh * HEAD_DIM, (hh + 1) * HEAD_DIM)
            a = a_ref[hh]
            kf = k_ref[hh]
            v = i_ref[pl.ds(r0, c), cols].astype(F32)
            gx = g_ref[pl.ds(r0, c), cols].astype(F32)
            vb = v.astype(BF16)
            st = st_ref[hh]
            o = (_dot(p_ref[hh].astype(BF16), vb)
                 + _dot_nt((qf_ref[hh] * jnp.exp2(a)).astype(BF16), st.astype(BF16)))
            a_last = a[c - 1:c, :]
            k_end = kf * jnp.exp2(a_last - a)
            st_ref[hh] = st * jnp.exp2(a_last) + _dot(v.T.astype(BF16), k_end.astype(BF16))
            o = o * lax.rsqrt(jnp.mean(o * o, axis=-1, keepdims=True) + NORM_EPS) * ng
            o_ref[pl.ds(r0, c), cols] = (o * (gx * jax.nn.sigmoid(gx))).astype(o_ref.dtype)
        return carry

    lax.fori_loop(0, tb // c, chunk, 0)


def hgrn2(proj, lay, lb, norm_g, batch, seq, heads, heads_per_step=8, tb=512):
    heads_per_step = min(heads_per_step, heads)
    assert heads % heads_per_step == 0
    hw = heads_per_step * HEAD_DIM
    nt = seq // tb
    ng = heads // heads_per_step

    def col(name):
        base = lay[name] // hw
        return lambda b, h, t: (b * nt + t, base + h)

    return pl.pallas_call(
        functools.partial(_hgrn_kernel, heads_per_step=heads_per_step, tb=tb),
        grid=(batch, ng, nt),
        in_specs=[pl.BlockSpec((tb, hw), col("q_h")),
                  pl.BlockSpec((tb, hw), col("f_h")),
                  pl.BlockSpec((tb, hw), col("i_h")),
                  pl.BlockSpec((tb, hw), col("g_h")),
                  pl.BlockSpec((1, hw), lambda b, h, t: (0, h)),
                  pl.BlockSpec((1, HEAD_DIM), lambda b, h, t: (0, 0))],
        out_specs=pl.BlockSpec((tb, hw), lambda b, h, t: (b * nt + t, h)),
        out_shape=jax.ShapeDtypeStruct((batch * seq, heads * HEAD_DIM), BF16),
        scratch_shapes=[pltpu.VMEM((heads_per_step, HEAD_DIM, HEAD_DIM), F32),
                        pltpu.VMEM((heads_per_step, HGRN_CHUNK, HEAD_DIM), F32),
                        pltpu.VMEM((heads_per_step, HGRN_CHUNK, HEAD_DIM), F32),
                        pltpu.VMEM((heads_per_step, HGRN_CHUNK, HEAD_DIM), F32),
                        pltpu.VMEM((heads_per_step, HGRN_CHUNK, HGRN_CHUNK), F32)],
        compiler_params=_cparams(("parallel", "parallel", "arbitrary")),
        name="hgrn2",
    )(proj, proj, proj, proj, lb.reshape(1, -1), norm_g.reshape(1, HEAD_DIM))


def _merge_kernel(a_ref, b_ref, wa_ref, wb_ref, ga_ref, gb_ref, o_ref):
    ya = _dot(a_ref[...], wa_ref[...])
    yb = _dot(b_ref[...], wb_ref[...])
    ga = jax.nn.sigmoid(ga_ref[...].astype(F32))
    gb = jax.nn.sigmoid(gb_ref[...].astype(F32))
    o_ref[...] = (ga * ya + gb * yb).astype(o_ref.dtype)


def gated_merge(attn, hg, w_pa, w_pb, proj, lay, tm=1024, tn=512):
    m, ka = attn.shape
    kb = hg.shape[1]
    d = w_pa.shape[1]
    ga0, gb0 = lay["gate_a"] // tn, lay["gate_b"] // tn
    return pl.pallas_call(
        _merge_kernel,
        grid=(m // tm, d // tn),
        in_specs=[pl.BlockSpec((tm, ka), lambda i, j: (i, 0)),
                  pl.BlockSpec((tm, kb), lambda i, j: (i, 0)),
                  pl.BlockSpec((ka, tn), lambda i, j: (0, j)),
                  pl.BlockSpec((kb, tn), lambda i, j: (0, j)),
                  pl.BlockSpec((tm, tn), lambda i, j: (i, ga0 + j)),
                  pl.BlockSpec((tm, tn), lambda i, j: (i, gb0 + j))],
        out_specs=pl.BlockSpec((tm, tn), lambda i, j: (i, j)),
        out_shape=jax.ShapeDtypeStruct((m, d), BF16),
        compiler_params=_cparams(("parallel", "parallel")),
        name="gated_merge",
    )(attn, hg, w_pa, w_pb, proj, proj)


def _norm_residual_kernel(x_ref, y_ref, gp_ref, gn_ref, xo_ref, ho_ref):
    y = y_ref[...]
    yn = y * lax.rsqrt(jnp.mean(y * y, axis=-1, keepdims=True) + NORM_EPS) * gp_ref[...]
    x = x_ref[...] + yn
    xo_ref[...] = x
    ho_ref[...] = (x * lax.rsqrt(jnp.mean(x * x, axis=-1, keepdims=True) + NORM_EPS)
                   * gn_ref[...]).astype(ho_ref.dtype)


def norm_residual(x, y, g_post, g_next, rows=256):
    n, d = x.shape
    row = lambda i: (i, 0)
    const = lambda i: (0, 0)
    return pl.pallas_call(
        _norm_residual_kernel,
        grid=(n // rows,),
        in_specs=[pl.BlockSpec((rows, d), row), pl.BlockSpec((rows, d), row),
                  pl.BlockSpec((1, d), const), pl.BlockSpec((1, d), const)],
        out_specs=[pl.BlockSpec((rows, d), row), pl.BlockSpec((rows, d), row)],
        out_shape=[jax.ShapeDtypeStruct((n, d), F32), jax.ShapeDtypeStruct((n, d), BF16)],
        compiler_params=_cparams(("parallel",)),
        name="norm_residual",
    )(x, y, g_post.reshape(1, d), g_next.reshape(1, d))


GELU_C = math.sqrt(2.0 / math.pi)


def _gelu_tanh(x):
    z = x * (x * x * (-2.0 * GELU_C * 0.044715 * LOG2E) + (-2.0 * GELU_C * LOG2E))
    return x / (1.0 + jnp.exp2(z))


def _ffn_up_kernel(x_ref, wg_ref, wv_ref, cwg_ref, cwv_ref, cbg_ref, cbv_ref, o_ref,
                   wb_ref, u_ref, *, tm, tn, seq):
    i = pl.program_id(1)
    starts_seq = (i * tm) % seq == 0

    @pl.when(i == 0)
    def _():
        wb_ref[:, :tn] = wg_ref[...].astype(BF16)
        wb_ref[:, tn:] = wv_ref[...].astype(BF16)

    @pl.when(starts_seq)
    def _():
        u_ref[0:8, :] = jnp.zeros((8, 2 * tn), F32)

    @pl.when(jnp.logical_not(starts_seq))
    def _():
        u_ref[0:8, :] = u_ref[tm:tm + 8, :]

    u_ref[8:8 + tm, :] = _dot(x_ref[...], wb_ref[...])

    def conv(lo, w_ref, b_ref):
        cols = slice(lo, lo + tn)
        return (b_ref[...] + w_ref[0:1, :] * u_ref[6:6 + tm, cols]
                + w_ref[1:2, :] * u_ref[7:7 + tm, cols] + w_ref[2:3, :] * u_ref[8:8 + tm, cols])

    gate = conv(0, cwg_ref, cbg_ref)
    val = conv(tn, cwv_ref, cbv_ref)
    o_ref[...] = (_gelu_tanh(gate) * val).astype(o_ref.dtype)


def ffn_up_conv_geglu(h, w_up, conv_w, conv_b, layer, seq, tm, tn):
    m, kdim = h.shape
    dff = w_up.shape[2] // 2
    assert m % tm == 0 and dff % tn == 0 and seq % tm == 0
    nc = dff // tn
    wspec = lambda off: pl.BlockSpec((None, kdim, tn), lambda j, i: (layer, 0, off + j))
    cspec = lambda r, off: pl.BlockSpec((None, r, tn), lambda j, i: (layer, 0, off + j))
    return pl.pallas_call(
        functools.partial(_ffn_up_kernel, tm=tm, tn=tn, seq=seq),
        grid=(nc, m // tm),
        in_specs=[pl.BlockSpec((tm, kdim), lambda j, i: (i, 0)),
                  wspec(0), wspec(nc),
                  cspec(CONV_WIDTH, 0), cspec(CONV_WIDTH, nc),
                  cspec(1, 0), cspec(1, nc)],
        out_specs=pl.BlockSpec((tm, tn), lambda j, i: (i, j)),
        out_shape=jax.ShapeDtypeStruct((m, dff), BF16),
        scratch_shapes=[pltpu.VMEM((kdim, 2 * tn), BF16), pltpu.VMEM((tm + 8, 2 * tn), F32)],
        compiler_params=_cparams(("parallel", "arbitrary")),
        name="ffn_up",
    )(h, w_up, w_up, conv_w, conv_w, conv_b[:, None, :], conv_b[:, None, :])


def _in_layout(d, aw, iw, hk, hv, kvl):
    segs = [("gate_a", d), ("gate_b", d), ("q_i", iw), ("q_a", aw), ("q_h", hk), ("f_h", hk),
            ("i_h", hv), ("g_h", hv), ("c_kv", kvl), ("k_i", HEAD_DIM), ("w_i", HEAD_DIM)]
    lay, off = {}, 0
    for name, w in segs:
        assert off % w == 0, (name, off, w)
        lay[name] = off
        off += w
    lay["pad"] = (-off) % IN_PROJ_TN
    lay["total"] = off + lay["pad"]
    return lay


def _reorder_w_in(w, lay, d, aw, iw, ih, hk, hv, kvl):
    splits = np.cumsum([aw, kvl, iw, HEAD_DIM, ih, hk, hk, hv, hv, d, d])[:-1].tolist()
    q_a, c_kv, q_i, k_i, w_i, q_h, f_h, i_h, g_h, gate_a, gate_b = jnp.split(w, splits, axis=1)
    w_i = jnp.pad(w_i, ((0, 0), (0, HEAD_DIM - ih + lay["pad"])))
    out = jnp.concatenate([gate_a, gate_b, q_i, q_a, q_h, f_h, i_h, g_h, c_kv, k_i, w_i], axis=1)
    assert out.shape[1] == lay["total"]
    return out.astype(BF16)


def _pick(n, cands):
    for c in cands:
        if n % c == 0:
            return c
    raise ValueError(f"no tile for {n}")


def kernel(x, rel_bias, hgrn_lb_logits, mix_pre_g, mix_post_g, w_in, kv_norm_g, w_uk, w_uv, idx_k_ln_g, idx_k_ln_b, hgrn_norm_g, w_proj_attn, w_proj_hgrn, w_out, ffn_pre_g, ffn_post_g, w_up, conv_w, conv_b, w_down):
    batch, seq, d = x.shape
    depth = w_in.shape[0]
    aw = w_proj_attn.shape[1]
    hv = w_proj_hgrn.shape[1]
    hk = hgrn_lb_logits.shape[1]
    kvl = w_uk.shape[1]
    dff = w_down.shape[1]
    heads = aw // HEAD_DIM
    hheads = hv // HEAD_DIM
    ih = w_in.shape[2] - (aw + kvl + HEAD_DIM + 2 * hk + 2 * hv + 2 * d)
    ih = ih // (HEAD_DIM + 1)
    iw = ih * HEAD_DIM
    lay = _in_layout(d, aw, iw, hk, hv, kvl)
    n = batch * seq

    lb_p = jax.nn.softmax(hgrn_lb_logits.astype(F32), axis=0)
    lb_all = jnp.cumsum(lb_p, axis=0) - lb_p[0:1]
    bias_tiles = rel_bias_tiles(rel_bias, heads)

    tm = _pick(n, (1024, 512, 256))
    xf = x.reshape(n, d)
    h = rmsnorm_cast(xf, mix_pre_g[0])
    for l in range(depth):
        w_in_l = _reorder_w_in(w_in[l], lay, d, aw, iw, ih, hk, hv, kvl)
        proj = matmul(h, w_in_l, tm, IN_PROJ_TN, name="in_proj")
        k, v, kidx, wi = dsa_prep(proj, lay, kv_norm_g[l], w_uk[l], w_uv[l],
                                  idx_k_ln_g[l], idx_k_ln_b[l], ih)
        attn = dsa_attention(proj, lay, wi, kidx, k, v, bias_tiles, batch, seq, heads, ih)
        hg = hgrn2(proj, lay, lb_all[l], hgrn_norm_g[l], batch, seq, hheads)
        merged = gated_merge(attn, hg, w_proj_attn[l].astype(BF16), w_proj_hgrn[l].astype(BF16),
                             proj, lay, tm=tm)
        y = matmul_wstat(merged, w_out, l, tm, 512, out_dtype=F32, name="out_proj")
        xf, h = norm_residual(xf, y, mix_post_g[l], ffn_pre_g[l])
        act = ffn_up_conv_geglu(h, w_up, conv_w, conv_b, l, seq, tm, _pick(dff, (256, 128)))
        y = matmul(act, w_down[l].astype(BF16), tm, 512, tk=_pick(dff, (dff // 2, dff)),
                   out_dtype=F32, name="ffn_down")
        g_next = mix_pre_g[l + 1] if l + 1 < depth else mix_pre_g[0]
        xf, h = norm_residual(xf, y, ffn_post_g[l], g_next)
    return xf.reshape(batch, seq, d)
```

```python
import functools
import math

import jax
import jax.numpy as jnp
from jax import lax
from jax.experimental import pallas as pl
from jax.experimental.pallas import tpu as pltpu

F32 = jnp.float32
BF16 = jnp.bfloat16

NORM_EPS = 1e-6
HEAD_DIM = 128
IDX_TOPK = 256
REL_BUCKETS = 32
REL_MAX_DIST = 128
CONV_WIDTH = 3
LANES = 128
VMEM_LIMIT_BYTES = 56 * 1024 * 1024

ATT_BLOCK = 256
QI_BLOCK = 512
HGRN_CHUNK = 64
HGRN_SUB = 16
HGRN_SAFE_LOG2 = 100.0
IN_PROJ_TN = 512
IN_PAD_ALIGN = 1024
INT_MIN = -2 ** 31
MASK_NEG = -1e30
M_INIT = -1e20
LOG2E = math.log2(math.e)


def _cparams(sem):
    return pltpu.CompilerParams(dimension_semantics=sem, vmem_limit_bytes=VMEM_LIMIT_BYTES)


def _dot(a, b):
    return jnp.dot(a, b, preferred_element_type=F32)


def _dot_nt(a, b):
    return lax.dot_general(a, b, (((1,), (1,)), ((), ())), preferred_element_type=F32)


def _rmsnorm_kernel(x_ref, g_ref, o_ref):
    x = x_ref[...]
    r = lax.rsqrt(jnp.mean(x * x, axis=-1, keepdims=True) + NORM_EPS)
    o_ref[...] = (x * r * g_ref[...]).astype(o_ref.dtype)


def rmsnorm_cast(x, g, rows=256):
    n, d = x.shape
    return pl.pallas_call(
        _rmsnorm_kernel,
        grid=(n // rows,),
        in_specs=[pl.BlockSpec((rows, d), lambda i: (i, 0)),
                  pl.BlockSpec((1, d), lambda i: (0, 0))],
        out_specs=pl.BlockSpec((rows, d), lambda i: (i, 0)),
        out_shape=jax.ShapeDtypeStruct((n, d), BF16),
        compiler_params=_cparams(("parallel",)),
        name="rmsnorm_cast",
    )(x, g.reshape(1, d))


def _mm_kernel(x_ref, w_ref, o_ref):
    o_ref[...] = _dot(x_ref[...], w_ref[...]).astype(o_ref.dtype)


def _mm_acc_kernel(x_ref, w_ref, o_ref, acc_ref, *, nk):
    k = pl.program_id(2)

    @pl.when(k == 0)
    def _():
        acc_ref[...] = jnp.zeros_like(acc_ref)

    acc_ref[...] += _dot(x_ref[...], w_ref[...])

    @pl.when(k == nk - 1)
    def _():
        o_ref[...] = acc_ref[...].astype(o_ref.dtype)


def matmul(x, w, tm, tn, tk=None, out_dtype=BF16, name="matmul"):
    m, kdim = x.shape
    n = w.shape[1]
    tk = kdim if tk is None else tk
    nk = kdim // tk
    assert m % tm == 0 and n % tn == 0 and kdim % tk == 0
    if nk == 1:
        return pl.pallas_call(
            _mm_kernel,
            grid=(m // tm, n // tn),
            in_specs=[pl.BlockSpec((tm, kdim), lambda i, j: (i, 0)),
                      pl.BlockSpec((kdim, tn), lambda i, j: (0, j))],
            out_specs=pl.BlockSpec((tm, tn), lambda i, j: (i, j)),
            out_shape=jax.ShapeDtypeStruct((m, n), out_dtype),
            compiler_params=_cparams(("parallel", "parallel")),
            name=name,
        )(x, w)
    return pl.pallas_call(
        functools.partial(_mm_acc_kernel, nk=nk),
        grid=(m // tm, n // tn, nk),
        in_specs=[pl.BlockSpec((tm, tk), lambda i, j, k: (i, k)),
                  pl.BlockSpec((tk, tn), lambda i, j, k: (k, j))],
        out_specs=pl.BlockSpec((tm, tn), lambda i, j, k: (i, j)),
        out_shape=jax.ShapeDtypeStruct((m, n), out_dtype),
        scratch_shapes=[pltpu.VMEM((tm, tn), F32)],
        compiler_params=_cparams(("parallel", "parallel", "arbitrary")),
        name=name,
    )(x, w)


def _mm_wstat_kernel(x_ref, w_ref, o_ref, wb_ref):
    @pl.when(pl.program_id(1) == 0)
    def _():
        wb_ref[...] = w_ref[...].astype(BF16)

    o_ref[...] = _dot(x_ref[...], wb_ref[...]).astype(o_ref.dtype)


def matmul_wstat(x, w, layer, tm, tn, out_dtype=BF16, name="matmul_wstat"):
    m, kdim = x.shape
    n = w.shape[2]
    assert m % tm == 0 and n % tn == 0
    return pl.pallas_call(
        _mm_wstat_kernel,
        grid=(n // tn, m // tm),
        in_specs=[pl.BlockSpec((tm, kdim), lambda j, i: (i, 0)),
                  pl.BlockSpec((None, kdim, tn), lambda j, i: (layer, 0, j))],
        out_specs=pl.BlockSpec((tm, tn), lambda j, i: (i, j)),
        out_shape=jax.ShapeDtypeStruct((m, n), out_dtype),
        scratch_shapes=[pltpu.VMEM((kdim, tn), BF16)],
        compiler_params=_cparams(("parallel", "arbitrary")),
        name=name,
    )(x, w)


def _dsa_prep_kernel(ckv_ref, kw_ref, kvg_ref, wuk_ref, wuv_ref, lng_ref, lnb_ref,
                     k_ref, v_ref, kidx_ref, wi_ref, *, k_scale, w_scale):
    c = ckv_ref[...].astype(F32)
    c = c * lax.rsqrt(jnp.mean(c * c, axis=-1, keepdims=True) + NORM_EPS) * kvg_ref[...]
    cb = c.astype(BF16)
    k_ref[...] = (_dot(cb, wuk_ref[...]) * k_scale).astype(k_ref.dtype)
    v_ref[:, :HEAD_DIM] = _dot(cb, wuv_ref[...]).astype(v_ref.dtype)
    v_ref[:, HEAD_DIM:] = jnp.ones((v_ref.shape[0], HEAD_DIM), v_ref.dtype)
    kw = kw_ref[...].astype(F32)
    ki = kw[:, :HEAD_DIM]
    mu = jnp.mean(ki, axis=-1, keepdims=True)
    kc = ki - mu
    kn = kc * lax.rsqrt(jnp.mean(kc * kc, axis=-1, keepdims=True) + NORM_EPS)
    kidx_ref[...] = (kn * lng_ref[...] + lnb_ref[...]).astype(kidx_ref.dtype)
    wi_ref[...] = kw[:, HEAD_DIM:] * w_scale


def dsa_prep(proj, lay, kv_norm_g, w_uk, w_uv, ln_g, ln_b, idx_heads, rows=512):
    n = proj.shape[0]
    kvl = w_uk.shape[0]
    row = lambda i: (i, 0)
    const = lambda i: (0, 0)
    out = jax.ShapeDtypeStruct((n, HEAD_DIM), BF16)
    return pl.pallas_call(
        functools.partial(_dsa_prep_kernel, k_scale=HEAD_DIM ** -0.5 * LOG2E,
                          w_scale=idx_heads ** -0.5 * HEAD_DIM ** -0.5),
        grid=(n // rows,),
        in_specs=[pl.BlockSpec((rows, kvl), lambda i: (i, lay["c_kv"] // kvl)),
                  pl.BlockSpec((rows, 2 * HEAD_DIM), lambda i: (i, lay["k_i"] // (2 * HEAD_DIM))),
                  pl.BlockSpec((1, kvl), const),
                  pl.BlockSpec((kvl, HEAD_DIM), const),
                  pl.BlockSpec((kvl, HEAD_DIM), const),
                  pl.BlockSpec((1, HEAD_DIM), const),
                  pl.BlockSpec((1, HEAD_DIM), const)],
        out_specs=[pl.BlockSpec((rows, HEAD_DIM), row), pl.BlockSpec((rows, 2 * HEAD_DIM), row),
                   pl.BlockSpec((rows, HEAD_DIM), row), pl.BlockSpec((rows, HEAD_DIM), row)],
        out_shape=[out, jax.ShapeDtypeStruct((n, 2 * HEAD_DIM), BF16), out,
                   jax.ShapeDtypeStruct((n, HEAD_DIM), F32)],
        compiler_params=_cparams(("parallel",)),
        name="dsa_prep",
    )(proj, proj, kv_norm_g.reshape(1, kvl), w_uk.astype(BF16), w_uv.astype(BF16),
      ln_g.reshape(1, HEAD_DIM), ln_b.reshape(1, HEAD_DIM))


def _sortable_key(x):
    u = lax.bitcast_convert_type(x, jnp.int32)
    return u ^ ((u >> 31) & jnp.int32(0x7FFFFFFF))


def _dsa_kernel(*refs, heads, idx_heads, topk, n_qi):
    qa_ref = refs[0]
    qi_refs = refs[1:1 + n_qi]
    (wi_ref, kidx_ref, k_ref, v_ref, bias_ref, o_ref,
     qas_ref, qis_ref, keys_ref, mask_ref, m_ref, acc_ref) = refs[1 + n_qi:]
    qb = ATT_BLOCK
    ck = ATT_BLOCK
    j = pl.program_id(1)

    for h in range(heads):
        qas_ref[h * qb:(h + 1) * qb, :] = qa_ref[:, h * HEAD_DIM:(h + 1) * HEAD_DIM]
    for h in range(idx_heads):
        src, off = divmod(h * HEAD_DIM, QI_BLOCK)
        qis_ref[h * qb:(h + 1) * qb, :] = qi_refs[src][:, off:off + HEAD_DIM]

    row_pos = j * qb + lax.broadcasted_iota(jnp.int32, (qb, ck), 0)
    lane = lax.broadcasted_iota(jnp.int32, (qb, ck), 1)
    hg = 8 if idx_heads % 8 == 0 else idx_heads

    def score_chunk(c, carry):
        kc = kidx_ref[pl.ds(pl.multiple_of(c * ck, ck), ck), :]
        acc = jnp.zeros((qb, ck), F32)
        for g in range(idx_heads // hg):
            s = _dot_nt(qis_ref[g * hg * qb:(g + 1) * hg * qb, :], kc)
            for hh in range(hg):
                h = g * hg + hh
                acc = acc + jnp.maximum(s[hh * qb:(hh + 1) * qb, :], 0.0) * wi_ref[:, h:h + 1]
        valid = (c * ck + lane) <= row_pos
        keys_ref[c] = jnp.where(valid, _sortable_key(acc), jnp.int32(INT_MIN))
        return carry

    lax.fori_loop(0, j + 1, score_chunk, 0)

    def bit_step(i, lo):
        cand = lo + lax.shift_left(jnp.int32(1), jnp.int32(31) - i)

        def count_chunk(c, part):
            ge = (keys_ref[c] >= cand).astype(jnp.int32)
            for t in range(ck // LANES):
                part = part + ge[:, t * LANES:(t + 1) * LANES]
            return part

        part = lax.fori_loop(0, j + 1, count_chunk, jnp.zeros((qb, LANES), jnp.int32))
        cnt = jnp.sum(part, axis=-1, keepdims=True)
        return jnp.where(cnt >= topk, cand, lo)

    lo = lax.fori_loop(0, 32, bit_step, jnp.full((qb, 1), INT_MIN, jnp.int32))
    thr = jnp.maximum(lo, jnp.int32(INT_MIN + 1))

    m_ref[...] = jnp.full_like(m_ref, M_INIT)
    acc_ref[...] = jnp.zeros_like(acc_ref)

    def attend(c, bias_idx):
        off = pl.multiple_of(c * ck, ck)
        kc = k_ref[pl.ds(off, ck), :]
        vc = v_ref[pl.ds(off, ck), :]
        mask_ref[...] = jnp.where(keys_ref[c] >= thr, 0.0, MASK_NEG)
        for h in range(heads):
            rows = slice(h * qb, (h + 1) * qb)
            s = _dot_nt(qas_ref[rows, :], kc) + mask_ref[...]
            if bias_idx is not None:
                s = s + bias_ref[bias_idx, rows, :]
            parts = [s[:, t * LANES:(t + 1) * LANES] for t in range(ck // LANES)]
            smax = functools.reduce(jnp.maximum, parts)
            m_prev = m_ref[rows, :]
            m_new = jnp.maximum(m_prev, jnp.max(smax, axis=-1, keepdims=True))
            p = jnp.concatenate([jnp.exp2(x - m_new) for x in parts], axis=1).astype(BF16)
            alpha = jnp.exp2(m_prev - m_new)
            pv = _dot(p, vc)
            acc_ref[rows, :HEAD_DIM] = alpha * acc_ref[rows, :HEAD_DIM] + pv[:, :HEAD_DIM]
            acc_ref[rows, HEAD_DIM:] = alpha * acc_ref[rows, HEAD_DIM:] + pv[:, HEAD_DIM:]
            m_ref[rows, :] = m_new

    def far_chunk(c, carry):
        attend(c, None)
        return carry

    lax.fori_loop(0, j - 1, far_chunk, 0)

    @pl.when(j >= 1)
    def _():
        attend(j - 1, 1)

    attend(j, 0)

    for h in range(heads):
        rows = slice(h * qb, (h + 1) * qb)
        o = acc_ref[rows, :HEAD_DIM] / acc_ref[rows, HEAD_DIM:]
        o_ref[:, h * HEAD_DIM:(h + 1) * HEAD_DIM] = o.astype(o_ref.dtype)


def dsa_attention(proj, lay, wi, kidx, k, v, bias_tiles, batch, seq, heads, idx_heads):
    qb = ATT_BLOCK
    nqb = seq // qb
    aw = heads * HEAD_DIM
    iw = idx_heads * HEAD_DIM
    topk = min(IDX_TOPK, seq // 4)
    assert lay["q_i"] % QI_BLOCK == 0 and iw % QI_BLOCK == 0 and lay["q_a"] % aw == 0
    n_qi = iw // QI_BLOCK
    return pl.pallas_call(
        functools.partial(_dsa_kernel, heads=heads, idx_heads=idx_heads, topk=topk, n_qi=n_qi),
        grid=(batch, nqb),
        in_specs=[pl.BlockSpec((qb, aw), lambda b, j: (b * nqb + j, lay["q_a"] // aw)),
                  *[pl.BlockSpec((qb, QI_BLOCK), lambda b, j, t=t: (b * nqb + j, lay["q_i"] // QI_BLOCK + t))
                    for t in range(n_qi)],
                  pl.BlockSpec((qb, HEAD_DIM), lambda b, j: (b * nqb + j, 0)),
                  pl.BlockSpec((seq, HEAD_DIM), lambda b, j: (b, 0)),
                  pl.BlockSpec((seq, HEAD_DIM), lambda b, j: (b, 0)),
                  pl.BlockSpec((seq, 2 * HEAD_DIM), lambda b, j: (b, 0)),
                  pl.BlockSpec((2, heads * qb, qb), lambda b, j: (0, 0, 0),
                               pipeline_mode=pl.Buffered(1))],
        out_specs=pl.BlockSpec((qb, aw), lambda b, j: (b * nqb + j, 0)),
        out_shape=jax.ShapeDtypeStruct((batch * seq, aw), BF16),
        scratch_shapes=[pltpu.VMEM((heads * qb, HEAD_DIM), BF16),
                        pltpu.VMEM((idx_heads * qb, HEAD_DIM), BF16),
                        pltpu.VMEM((seq // qb, qb, qb), jnp.int32),
                        pltpu.VMEM((qb, qb), F32),
                        pltpu.VMEM((heads * qb, LANES), F32),
                        pltpu.VMEM((heads * qb, 2 * HEAD_DIM), F32)],
        compiler_params=_cparams(("parallel", "arbitrary")),
        name="dsa_attention",
    )(proj, *([proj] * n_qi), wi, kidx, k, v, bias_tiles)


def _t5_bucket(dist):
    max_exact = REL_BUCKETS // 2
    d = jnp.maximum(dist, 0)
    df = jnp.maximum(d, 1).astype(F32)
    large = max_exact + (jnp.log(df / max_exact) / math.log(REL_MAX_DIST / max_exact)
                         * (REL_BUCKETS - max_exact)).astype(jnp.int32)
    large = jnp.minimum(large, REL_BUCKETS - 1)
    return jnp.where(d < max_exact, d, large)


def rel_bias_tiles(rel_bias, heads):
    qb = ATT_BLOCK
    assert _static_far_bucket(qb) == REL_BUCKETS - 1
    tbl = (rel_bias[_t5_bucket(jnp.arange(4 * qb))] - rel_bias[REL_BUCKETS - 1][None, :]).T * LOG2E
    y = jnp.arange(2 * qb)
    tiles = []
    for off in (0, qb):
        d = off + jnp.where(y < qb, -y, 2 * qb - y)
        u = tbl[:, jnp.maximum(d, 0)]
        flat = jnp.broadcast_to(u[:, None, :], (heads, qb, 2 * qb)).reshape(heads, 2 * qb * qb)
        skew = flat[:, :qb * (2 * qb - 1)].reshape(heads, qb, 2 * qb - 1)
        tiles.append(skew[:, :, :qb].reshape(heads * qb, qb))
    return jnp.stack(tiles).astype(F32)


def _static_far_bucket(d):
    max_exact = REL_BUCKETS // 2
    v = max_exact + int(math.log(d / max_exact) / math.log(REL_MAX_DIST / max_exact)
                        * (REL_BUCKETS - max_exact))
    return min(v, REL_BUCKETS - 1)


def _hgrn_kernel(q_ref, f_ref, i_ref, g_ref, lb_ref, ng_ref, o_ref,
                 st_ref, a_ref, k_ref, qf_ref, p_ref, *, heads_per_step, tb):
    c = HGRN_CHUNK
    sub = HGRN_SUB
    nsub = c // sub
    hs = range(heads_per_step)

    @pl.when(pl.program_id(2) == 0)
    def _():
        st_ref[...] = jnp.zeros_like(st_ref)

    r_io = lax.broadcasted_iota(jnp.int32, (c, c), 0)
    c_io = lax.broadcasted_iota(jnp.int32, (c, c), 1)
    tri = (c_io <= r_io).astype(F32)
    sub_r = lax.broadcasted_iota(jnp.int32, (sub, c), 0)
    sub_c = lax.broadcasted_iota(jnp.int32, (sub, c), 1)
    ng = ng_ref[...]

    def a_start(hh, sb):
        return a_ref[hh, sb * sub - 1:sb * sub, :] if sb > 0 else jnp.zeros((1, HEAD_DIM), F32)

    def chunk(ci, carry):
        r0 = pl.multiple_of(ci * c, c)

        worst = jnp.zeros((1, HEAD_DIM), F32)
        for hh in hs:
            cols = slice(hh * HEAD_DIM, (hh + 1) * HEAD_DIM)
            lb = lb_ref[:, cols]
            z = f_ref[pl.ds(r0, c), cols].astype(F32)
            qx = q_ref[pl.ds(r0, c), cols].astype(F32)
            sig = jax.nn.sigmoid(z)
            lf = jnp.log2(lb + (1.0 - lb) * sig)
            k_ref[hh] = (1.0 - lb) * jax.nn.sigmoid(-z)
            qf_ref[hh] = qx * jax.nn.sigmoid(qx)
            a = jnp.dot(tri, lf, preferred_element_type=F32, precision=lax.Precision.HIGHEST)
            a_ref[hh] = a
            for sb in range(nsub):
                top = a[sb * sub - 1:sb * sub, :] if sb > 0 else jnp.zeros((1, HEAD_DIM), F32)
                worst = jnp.maximum(worst, top - a[(sb + 1) * sub - 1:(sb + 1) * sub, :])
        safe = jnp.max(worst) <= HGRN_SAFE_LOG2

        @pl.when(safe)
        def _():
            for hh in hs:
                a = a_ref[hh]
                kf = k_ref[hh]
                for sb in range(nsub):
                    rs = slice(sb * sub, (sb + 1) * sub)
                    top = a_start(hh, sb)
                    q_t = qf_ref[hh, rs, :] * jnp.exp2(a_ref[hh, rs, :] - top)
                    k_t = kf * jnp.exp2(top - a)
                    p = _dot_nt(q_t.astype(BF16), k_t.astype(BF16))
                    p_ref[hh, rs, :] = jnp.where(sub_c <= sub_r + sb * sub, p, 0.0)

        @pl.when(jnp.logical_not(safe))
        def _():
            for hh in hs:
                a = a_ref[hh]
                kf = k_ref[hh]
                for sb in range(nsub):
                    rs = slice(sb * sub, (sb + 1) * sub)
                    top = a_start(hh, sb)
                    a_blk = a_ref[hh, rs, :]
                    q_blk = qf_ref[hh, rs, :]
                    q_t = q_blk * jnp.exp2(a_blk - top)
                    k_t = kf * jnp.exp2(top - a)
                    p_off = _dot_nt(q_t.astype(BF16), k_t.astype(BF16))
                    p_diag = jnp.zeros((sub, c), F32)
                    for s in range(sub):
                        row = sb * sub + s
                        e = jnp.exp2(a_blk - a_ref[hh, row:row + 1, :])
                        col = jnp.sum(q_blk * e * k_ref[hh, row:row + 1, :], axis=-1, keepdims=True)
                        p_diag = jnp.where((sub_c == row) & (sub_r >= s), col, p_diag)
                    p_ref[hh, rs, :] = jnp.where(sub_c < sb * sub, p_off, p_diag)

        for hh in hs:
            cols = slice(hh * HEAD_DIM, (hh + 1) * HEAD_DIM)
            a = a_ref[hh]
            kf = k_ref[hh]
            v = i_ref[pl.ds(r0, c), cols].astype(F32)
            gx = g_ref[pl.ds(r0, c), cols].astype(F32)
            vb = v.astype(BF16)
            st = st_ref[hh]
            o = (_dot(p_ref[hh].astype(BF16), vb)
                 + _dot_nt((qf_ref[hh] * jnp.exp2(a)).astype(BF16), st.astype(BF16)))
            a_last = a[c - 1:c, :]
            k_end = kf * jnp.exp2(a_last - a)
            st_ref[hh] = st * jnp.exp2(a_last) + _dot(v.T.astype(BF16), k_end.astype(BF16))
            o = o * lax.rsqrt(jnp.mean(o * o, axis=-1, keepdims=True) + NORM_EPS) * ng
            o_ref[pl.ds(r0, c), cols] = (o * (gx * jax.nn.sigmoid(gx))).astype(o_ref.dtype)
        return carry

    lax.fori_loop(0, tb // c, chunk, 0)


def hgrn2(proj, lay, lb, norm_g, batch, seq, heads, heads_per_step=8, tb=512):
    heads_per_step = min(heads_per_step, heads)
    assert heads % heads_per_step == 0
    hw = heads_per_step * HEAD_DIM
    nt = seq // tb
    ng = heads // heads_per_step

    def col(name):
        base = lay[name] // hw
        return lambda b, h, t: (b * nt + t, base + h)

    return pl.pallas_call(
        functools.partial(_hgrn_kernel, heads_per_step=heads_per_step, tb=tb),
        grid=(batch, ng, nt),
        in_specs=[pl.BlockSpec((tb, hw), col("q_h")),
                  pl.BlockSpec((tb, hw), col("f_h")),
                  pl.BlockSpec((tb, hw), col("i_h")),
                  pl.BlockSpec((tb, hw), col("g_h")),
                  pl.BlockSpec((1, hw), lambda b, h, t: (0, h)),
                  pl.BlockSpec((1, HEAD_DIM), lambda b, h, t: (0, 0))],
        out_specs=pl.BlockSpec((tb, hw), lambda b, h, t: (b * nt + t, h)),
        out_shape=jax.ShapeDtypeStruct((batch * seq, heads * HEAD_DIM), BF16),
        scratch_shapes=[pltpu.VMEM((heads_per_step, HEAD_DIM, HEAD_DIM), F32),
                        pltpu.VMEM((heads_per_step, HGRN_CHUNK, HEAD_DIM), F32),
                        pltpu.VMEM((heads_per_step, HGRN_CHUNK, HEAD_DIM), F32),
                        pltpu.VMEM((heads_per_step, HGRN_CHUNK, HEAD_DIM), F32),
                        pltpu.VMEM((heads_per_step, HGRN_CHUNK, HGRN_CHUNK), F32)],
        compiler_params=_cparams(("parallel", "parallel", "arbitrary")),
        name="hgrn2",
    )(proj, proj, proj, proj, lb.reshape(1, -1), norm_g.reshape(1, HEAD_DIM))


def _merge_kernel(a_ref, b_ref, wa_ref, wb_ref, ga_ref, gb_ref, o_ref, wab_ref, wbb_ref):
    @pl.when(pl.program_id(1) == 0)
    def _():
        wab_ref[...] = wa_ref[...].astype(BF16)
        wbb_ref[...] = wb_ref[...].astype(BF16)

    ya = _dot(a_ref[...], wab_ref[...])
    yb = _dot(b_ref[...], wbb_ref[...])
    ga = jax.nn.sigmoid(ga_ref[...].astype(F32))
    gb = jax.nn.sigmoid(gb_ref[...].astype(F32))
    o_ref[...] = (ga * ya + gb * yb).astype(o_ref.dtype)


def gated_merge(attn, hg, w_pa, w_pb, layer, proj, lay, tm=1024, tn=512):
    m, ka = attn.shape
    kb = hg.shape[1]
    d = w_pa.shape[2]
    assert lay["gate_a"] % tn == 0 and lay["gate_b"] % tn == 0
    ga0, gb0 = lay["gate_a"] // tn, lay["gate_b"] // tn
    return pl.pallas_call(
        _merge_kernel,
        grid=(d // tn, m // tm),
        in_specs=[pl.BlockSpec((tm, ka), lambda j, i: (i, 0)),
                  pl.BlockSpec((tm, kb), lambda j, i: (i, 0)),
                  pl.BlockSpec((None, ka, tn), lambda j, i: (layer, 0, j)),
                  pl.BlockSpec((None, kb, tn), lambda j, i: (layer, 0, j)),
                  pl.BlockSpec((tm, tn), lambda j, i: (i, ga0 + j)),
                  pl.BlockSpec((tm, tn), lambda j, i: (i, gb0 + j))],
        out_specs=pl.BlockSpec((tm, tn), lambda j, i: (i, j)),
        out_shape=jax.ShapeDtypeStruct((m, d), BF16),
        scratch_shapes=[pltpu.VMEM((ka, tn), BF16), pltpu.VMEM((kb, tn), BF16)],
        compiler_params=_cparams(("parallel", "arbitrary")),
        name="gated_merge",
    )(attn, hg, w_pa, w_pb, proj, proj)


def _norm_residual_kernel(x_ref, y_ref, gp_ref, gn_ref, xo_ref, ho_ref):
    y = y_ref[...].astype(F32)
    yn = y * lax.rsqrt(jnp.mean(y * y, axis=-1, keepdims=True) + NORM_EPS) * gp_ref[...]
    x = x_ref[...] + yn
    xo_ref[...] = x
    ho_ref[...] = (x * lax.rsqrt(jnp.mean(x * x, axis=-1, keepdims=True) + NORM_EPS)
                   * gn_ref[...]).astype(ho_ref.dtype)


def _norm_residual_last_kernel(x_ref, y_ref, gp_ref, xo_ref):
    y = y_ref[...].astype(F32)
    xo_ref[...] = x_ref[...] + y * lax.rsqrt(jnp.mean(y * y, axis=-1, keepdims=True) + NORM_EPS) * gp_ref[...]


def norm_residual(x, y, g_post, g_next=None, rows=256):
    n, d = x.shape
    row = pl.BlockSpec((rows, d), lambda i: (i, 0))
    vec = pl.BlockSpec((1, d), lambda i: (0, 0))
    if g_next is None:
        return pl.pallas_call(
            _norm_residual_last_kernel,
            grid=(n // rows,),
            in_specs=[row, row, vec],
            out_specs=row,
            out_shape=jax.ShapeDtypeStruct((n, d), F32),
            compiler_params=_cparams(("parallel",)),
            name="norm_residual_last",
        )(x, y, g_post.reshape(1, d)), None
    return pl.pallas_call(
        _norm_residual_kernel,
        grid=(n // rows,),
        in_specs=[row, row, vec, vec],
        out_specs=[row, row],
        out_shape=[jax.ShapeDtypeStruct((n, d), F32), jax.ShapeDtypeStruct((n, d), BF16)],
        compiler_params=_cparams(("parallel",)),
        name="norm_residual",
    )(x, y, g_post.reshape(1, d), g_next.reshape(1, d))


GELU_C = math.sqrt(2.0 / math.pi)


def _gelu_tanh(x):
    z = x * (x * x * (-2.0 * GELU_C * 0.044715 * LOG2E) + (-2.0 * GELU_C * LOG2E))
    return x / (1.0 + jnp.exp2(z))


def _ffn_up_kernel(x_ref, wg_ref, wv_ref, cwg_ref, cwv_ref, cbg_ref, cbv_ref, o_ref,
                   wb_ref, u_ref, *, tm, tn, seq):
    i = pl.program_id(1)
    starts_seq = (i * tm) % seq == 0

    @pl.when(i == 0)
    def _():
        wb_ref[:, :tn] = wg_ref[...].astype(BF16)
        wb_ref[:, tn:] = wv_ref[...].astype(BF16)

    @pl.when(starts_seq)
    def _():
        u_ref[0:8, :] = jnp.zeros((8, 2 * tn), F32)

    @pl.when(jnp.logical_not(starts_seq))
    def _():
        u_ref[0:8, :] = u_ref[tm:tm + 8, :]

    u_ref[8:8 + tm, :] = _dot(x_ref[...], wb_ref[...])

    def conv(lo, w_ref, b_ref):
        cols = slice(lo, lo + tn)
        return (b_ref[...] + w_ref[0:1, :] * u_ref[6:6 + tm, cols]
                + w_ref[1:2, :] * u_ref[7:7 + tm, cols] + w_ref[2:3, :] * u_ref[8:8 + tm, cols])

    gate = conv(0, cwg_ref, cbg_ref)
    val = conv(tn, cwv_ref, cbv_ref)
    o_ref[...] = (_gelu_tanh(gate) * val).astype(o_ref.dtype)


def ffn_up_conv_geglu(h, w_up, conv_w, conv_b, layer, seq, tm, tn):
    m, kdim = h.shape
    dff = w_up.shape[2] // 2
    assert m % tm == 0 and dff % tn == 0 and seq % tm == 0
    nc = dff // tn
    wspec = lambda off: pl.BlockSpec((None, kdim, tn), lambda j, i: (layer, 0, off + j))
    cspec = lambda r, off: pl.BlockSpec((None, r, tn), lambda j, i: (layer, 0, off + j))
    return pl.pallas_call(
        functools.partial(_ffn_up_kernel, tm=tm, tn=tn, seq=seq),
        grid=(nc, m // tm),
        in_specs=[pl.BlockSpec((tm, kdim), lambda j, i: (i, 0)),
                  wspec(0), wspec(nc),
                  cspec(CONV_WIDTH, 0), cspec(CONV_WIDTH, nc),
                  cspec(1, 0), cspec(1, nc)],
        out_specs=pl.BlockSpec((tm, tn), lambda j, i: (i, j)),
        out_shape=jax.ShapeDtypeStruct((m, dff), BF16),
        scratch_shapes=[pltpu.VMEM((kdim, 2 * tn), BF16), pltpu.VMEM((tm + 8, 2 * tn), F32)],
        compiler_params=_cparams(("parallel", "arbitrary")),
        name="ffn_up",
    )(h, w_up, w_up, conv_w, conv_w, conv_b[:, None, :], conv_b[:, None, :])


def _in_layout(d, aw, iw, ih, hk, hv, kvl):
    segs = [("q_a", aw), ("c_kv", kvl), ("q_i", iw), ("k_i", HEAD_DIM), ("w_i", ih), ("pad", None),
            ("q_h", hk), ("f_h", hk), ("i_h", hv), ("g_h", hv), ("gate_a", d), ("gate_b", d)]
    lay, off = {}, 0
    for name, w in segs:
        if name == "pad":
            w = (-off) % IN_PAD_ALIGN
            lay["pad_width"] = w
        lay[name] = off
        off += w
    assert off % IN_PROJ_TN == 0
    lay["total"] = off
    return lay


def _pad_w_in(w, lay):
    cut = lay["pad"]
    zeros = jnp.zeros((w.shape[0], lay["pad_width"]), BF16)
    return jnp.concatenate([w[:, :cut].astype(BF16), zeros, w[:, cut:].astype(BF16)], axis=1)


def _pick(n, cands):
    for c in cands:
        if n % c == 0:
            return c
    raise ValueError(f"no tile for {n}")


def kernel(x, rel_bias, hgrn_lb_logits, mix_pre_g, mix_post_g, w_in, kv_norm_g, w_uk, w_uv, idx_k_ln_g, idx_k_ln_b, hgrn_norm_g, w_proj_attn, w_proj_hgrn, w_out, ffn_pre_g, ffn_post_g, w_up, conv_w, conv_b, w_down):
    batch, seq, d = x.shape
    depth = w_in.shape[0]
    aw = w_proj_attn.shape[1]
    hv = w_proj_hgrn.shape[1]
    hk = hgrn_lb_logits.shape[1]
    kvl = w_uk.shape[1]
    dff = w_down.shape[1]
    heads = aw // HEAD_DIM
    hheads = hv // HEAD_DIM
    ih = w_in.shape[2] - (aw + kvl + HEAD_DIM + 2 * hk + 2 * hv + 2 * d)
    ih = ih // (HEAD_DIM + 1)
    iw = ih * HEAD_DIM
    lay = _in_layout(d, aw, iw, ih, hk, hv, kvl)
    n = batch * seq

    lb_p = jax.nn.softmax(hgrn_lb_logits.astype(F32), axis=0)
    lb_all = jnp.cumsum(lb_p, axis=0) - lb_p[0:1]
    bias_tiles = rel_bias_tiles(rel_bias, heads)

    tm = _pick(n, (1024, 512, 256))
    xf = x.reshape(n, d)
    h = rmsnorm_cast(xf, mix_pre_g[0])
    for l in range(depth):
        w_in_l = _pad_w_in(w_in[l], lay)
        proj = matmul(h, w_in_l, tm, IN_PROJ_TN, name="in_proj")
        k, v, kidx, wi = dsa_prep(proj, lay, kv_norm_g[l], w_uk[l], w_uv[l],
                                  idx_k_ln_g[l], idx_k_ln_b[l], ih)
        attn = dsa_attention(proj, lay, wi, kidx, k, v, bias_tiles, batch, seq, heads, ih)
        hg = hgrn2(proj, lay, lb_all[l], hgrn_norm_g[l], batch, seq, hheads)
        merged = gated_merge(attn, hg, w_proj_attn, w_proj_hgrn, l, proj, lay, tm=tm)
        y = matmul_wstat(merged, w_out, l, tm, 512, name="out_proj")
        xf, h = norm_residual(xf, y, mix_post_g[l], ffn_pre_g[l])
        act = ffn_up_conv_geglu(h, w_up, conv_w, conv_b, l, seq, tm, _pick(dff, (256, 128)))
        y = matmul(act, w_down[l].astype(BF16), tm, 512, tk=_pick(dff, (dff // 2, dff)),
                   name="ffn_down")
        xf, h = norm_residual(xf, y, ffn_post_g[l], mix_pre_g[l + 1] if l + 1 < depth else None)
    return xf.reshape(batch, seq, d)
```

```python
import functools
import math

import jax
import jax.numpy as jnp
from jax import lax
from jax.experimental import pallas as pl
from jax.experimental.pallas import tpu as pltpu

F32 = jnp.float32
BF16 = jnp.bfloat16

NORM_EPS = 1e-6
HEAD_DIM = 128
IDX_TOPK = 256
REL_BUCKETS = 32
REL_MAX_DIST = 128
CONV_WIDTH = 3
LANES = 128
VMEM_LIMIT_BYTES = 56 * 1024 * 1024

ATT_BLOCK = 256
QI_BLOCK = 512
HGRN_CHUNK = 64
HGRN_SUB = 16
HGRN_SAFE_LOG2 = 100.0
IN_PROJ_TN = 512
IN_PAD_ALIGN = 1024
INT_MIN = -2 ** 31
MASK_NEG = -1e30
M_INIT = -1e20
LOG2E = math.log2(math.e)


def _cparams(sem):
    return pltpu.CompilerParams(dimension_semantics=sem, vmem_limit_bytes=VMEM_LIMIT_BYTES)


def _dot(a, b):
    return jnp.dot(a, b, preferred_element_type=F32)


def _dot_nt(a, b):
    return lax.dot_general(a, b, (((1,), (1,)), ((), ())), preferred_element_type=F32)


def _rmsnorm_kernel(x_ref, g_ref, o_ref):
    x = x_ref[...]
    r = lax.rsqrt(jnp.mean(x * x, axis=-1, keepdims=True) + NORM_EPS)
    o_ref[...] = (x * r * g_ref[...]).astype(o_ref.dtype)


def rmsnorm_cast(x, g, rows=256):
    n, d = x.shape
    return pl.pallas_call(
        _rmsnorm_kernel,
        grid=(n // rows,),
        in_specs=[pl.BlockSpec((rows, d), lambda i: (i, 0)),
                  pl.BlockSpec((1, d), lambda i: (0, 0))],
        out_specs=pl.BlockSpec((rows, d), lambda i: (i, 0)),
        out_shape=jax.ShapeDtypeStruct((n, d), BF16),
        compiler_params=_cparams(("parallel",)),
        name="rmsnorm_cast",
    )(x, g.reshape(1, d))


def _mm_kernel(x_ref, w_ref, o_ref):
    o_ref[...] = _dot(x_ref[...], w_ref[...]).astype(o_ref.dtype)


def _mm_acc_kernel(x_ref, w_ref, o_ref, acc_ref, *, nk):
    k = pl.program_id(2)

    @pl.when(k == 0)
    def _():
        acc_ref[...] = jnp.zeros_like(acc_ref)

    acc_ref[...] += _dot(x_ref[...], w_ref[...])

    @pl.when(k == nk - 1)
    def _():
        o_ref[...] = acc_ref[...].astype(o_ref.dtype)


def matmul(x, w, tm, tn, tk=None, out_dtype=BF16, name="matmul"):
    m, kdim = x.shape
    n = w.shape[1]
    tk = kdim if tk is None else tk
    nk = kdim // tk
    assert m % tm == 0 and n % tn == 0 and kdim % tk == 0
    if nk == 1:
        return pl.pallas_call(
            _mm_kernel,
            grid=(m // tm, n // tn),
            in_specs=[pl.BlockSpec((tm, kdim), lambda i, j: (i, 0)),
                      pl.BlockSpec((kdim, tn), lambda i, j: (0, j))],
            out_specs=pl.BlockSpec((tm, tn), lambda i, j: (i, j)),
            out_shape=jax.ShapeDtypeStruct((m, n), out_dtype),
            compiler_params=_cparams(("parallel", "parallel")),
            name=name,
        )(x, w)
    return pl.pallas_call(
        functools.partial(_mm_acc_kernel, nk=nk),
        grid=(m // tm, n // tn, nk),
        in_specs=[pl.BlockSpec((tm, tk), lambda i, j, k: (i, k)),
                  pl.BlockSpec((tk, tn), lambda i, j, k: (k, j))],
        out_specs=pl.BlockSpec((tm, tn), lambda i, j, k: (i, j)),
        out_shape=jax.ShapeDtypeStruct((m, n), out_dtype),
        scratch_shapes=[pltpu.VMEM((tm, tn), F32)],
        compiler_params=_cparams(("parallel", "parallel", "arbitrary")),
        name=name,
    )(x, w)


def _mm_wstat_kernel(x_ref, w_ref, o_ref, wb_ref):
    @pl.when(pl.program_id(1) == 0)
    def _():
        wb_ref[...] = w_ref[...].astype(BF16)

    o_ref[...] = _dot(x_ref[...], wb_ref[...]).astype(o_ref.dtype)


def matmul_wstat(x, w, layer, tm, tn, out_dtype=BF16, name="matmul_wstat"):
    m, kdim = x.shape
    n = w.shape[2]
    assert m % tm == 0 and n % tn == 0
    return pl.pallas_call(
        _mm_wstat_kernel,
        grid=(n // tn, m // tm),
        in_specs=[pl.BlockSpec((tm, kdim), lambda j, i: (i, 0)),
                  pl.BlockSpec((None, kdim, tn), lambda j, i: (layer, 0, j))],
        out_specs=pl.BlockSpec((tm, tn), lambda j, i: (i, j)),
        out_shape=jax.ShapeDtypeStruct((m, n), out_dtype),
        scratch_shapes=[pltpu.VMEM((kdim, tn), BF16)],
        compiler_params=_cparams(("parallel", "arbitrary")),
        name=name,
    )(x, w)


def _dsa_prep_kernel(ckv_ref, kw_ref, kvg_ref, wuk_ref, wuv_ref, lng_ref, lnb_ref,
                     k_ref, v_ref, kidx_ref, wi_ref, *, k_scale, w_scale):
    c = ckv_ref[...].astype(F32)
    c = c * lax.rsqrt(jnp.mean(c * c, axis=-1, keepdims=True) + NORM_EPS) * kvg_ref[...]
    cb = c.astype(BF16)
    k_ref[...] = (_dot(cb, wuk_ref[...]) * k_scale).astype(k_ref.dtype)
    v_ref[:, :HEAD_DIM] = _dot(cb, wuv_ref[...]).astype(v_ref.dtype)
    v_ref[:, HEAD_DIM:] = jnp.ones((v_ref.shape[0], HEAD_DIM), v_ref.dtype)
    kw = kw_ref[...].astype(F32)
    ki = kw[:, :HEAD_DIM]
    mu = jnp.mean(ki, axis=-1, keepdims=True)
    kc = ki - mu
    kn = kc * lax.rsqrt(jnp.mean(kc * kc, axis=-1, keepdims=True) + NORM_EPS)
    kidx_ref[...] = (kn * lng_ref[...] + lnb_ref[...]).astype(kidx_ref.dtype)
    wi_ref[...] = kw[:, HEAD_DIM:] * w_scale


def dsa_prep(proj, lay, kv_norm_g, w_uk, w_uv, ln_g, ln_b, idx_heads, rows=512):
    n = proj.shape[0]
    kvl = w_uk.shape[0]
    row = lambda i: (i, 0)
    const = lambda i: (0, 0)
    out = jax.ShapeDtypeStruct((n, HEAD_DIM), BF16)
    return pl.pallas_call(
        functools.partial(_dsa_prep_kernel, k_scale=HEAD_DIM ** -0.5 * LOG2E,
                          w_scale=idx_heads ** -0.5 * HEAD_DIM ** -0.5),
        grid=(n // rows,),
        in_specs=[pl.BlockSpec((rows, kvl), lambda i: (i, lay["c_kv"] // kvl)),
                  pl.BlockSpec((rows, 2 * HEAD_DIM), lambda i: (i, lay["k_i"] // (2 * HEAD_DIM))),
                  pl.BlockSpec((1, kvl), const),
                  pl.BlockSpec((kvl, HEAD_DIM), const),
                  pl.BlockSpec((kvl, HEAD_DIM), const),
                  pl.BlockSpec((1, HEAD_DIM), const),
                  pl.BlockSpec((1, HEAD_DIM), const)],
        out_specs=[pl.BlockSpec((rows, HEAD_DIM), row), pl.BlockSpec((rows, 2 * HEAD_DIM), row),
                   pl.BlockSpec((rows, HEAD_DIM), row), pl.BlockSpec((rows, HEAD_DIM), row)],
        out_shape=[out, jax.ShapeDtypeStruct((n, 2 * HEAD_DIM), BF16), out,
                   jax.ShapeDtypeStruct((n, HEAD_DIM), F32)],
        compiler_params=_cparams(("parallel",)),
        name="dsa_prep",
    )(proj, proj, kv_norm_g.reshape(1, kvl), w_uk.astype(BF16), w_uv.astype(BF16),
      ln_g.reshape(1, HEAD_DIM), ln_b.reshape(1, HEAD_DIM))


def _sortable_key(x):
    u = lax.bitcast_convert_type(x, jnp.int32)
    return u ^ ((u >> 31) & jnp.int32(0x7FFFFFFF))


def _dsa_kernel(*refs, heads, idx_heads, topk, n_qi):
    qa_ref = refs[0]
    qi_refs = refs[1:1 + n_qi]
    (wi_ref, kidx_ref, k_ref, v_ref, bias_ref, o_ref,
     qas_ref, qis_ref, keys_ref, mask_ref, m_ref, acc_ref) = refs[1 + n_qi:]
    qb = ATT_BLOCK
    ck = ATT_BLOCK
    j = pl.program_id(1)

    for h in range(heads):
        qas_ref[h * qb:(h + 1) * qb, :] = qa_ref[:, h * HEAD_DIM:(h + 1) * HEAD_DIM]
    for h in range(idx_heads):
        src, off = divmod(h * HEAD_DIM, QI_BLOCK)
        qis_ref[h * qb:(h + 1) * qb, :] = qi_refs[src][:, off:off + HEAD_DIM]

    row_pos = j * qb + lax.broadcasted_iota(jnp.int32, (qb, ck), 0)
    lane = lax.broadcasted_iota(jnp.int32, (qb, ck), 1)
    hg = 8 if idx_heads % 8 == 0 else idx_heads

    def score_chunk(c, carry):
        kc = kidx_ref[pl.ds(pl.multiple_of(c * ck, ck), ck), :]
        acc = jnp.zeros((qb, ck), F32)
        for g in range(idx_heads // hg):
            s = _dot_nt(qis_ref[g * hg * qb:(g + 1) * hg * qb, :], kc)
            for hh in range(hg):
                h = g * hg + hh
                acc = acc + jnp.maximum(s[hh * qb:(hh + 1) * qb, :], 0.0) * wi_ref[:, h:h + 1]
        valid = (c * ck + lane) <= row_pos
        keys_ref[c] = jnp.where(valid, _sortable_key(acc), jnp.int32(INT_MIN))
        return carry

    lax.fori_loop(0, j + 1, score_chunk, 0)

    def bit_step(i, lo):
        cand = lo + lax.shift_left(jnp.int32(1), jnp.int32(31) - i)

        def count_chunk(c, part):
            ge = (keys_ref[c] >= cand).astype(jnp.int32)
            for t in range(ck // LANES):
                part = part + ge[:, t * LANES:(t + 1) * LANES]
            return part

        part = lax.fori_loop(0, j + 1, count_chunk, jnp.zeros((qb, LANES), jnp.int32))
        cnt = jnp.sum(part, axis=-1, keepdims=True)
        return jnp.where(cnt >= topk, cand, lo)

    lo = lax.fori_loop(0, 32, bit_step, jnp.full((qb, 1), INT_MIN, jnp.int32))
    thr = jnp.maximum(lo, jnp.int32(INT_MIN + 1))

    m_ref[...] = jnp.full_like(m_ref, M_INIT)
    acc_ref[...] = jnp.zeros_like(acc_ref)

    def attend(c, bias_idx):
        off = pl.multiple_of(c * ck, ck)
        kc = k_ref[pl.ds(off, ck), :]
        vc = v_ref[pl.ds(off, ck), :]
        mask_ref[...] = jnp.where(keys_ref[c] >= thr, 0.0, MASK_NEG)
        for h in range(heads):
            rows = slice(h * qb, (h + 1) * qb)
            s = _dot_nt(qas_ref[rows, :], kc) + mask_ref[...]
            if bias_idx is not None:
                s = s + bias_ref[bias_idx, rows, :]
            parts = [s[:, t * LANES:(t + 1) * LANES] for t in range(ck // LANES)]
            smax = functools.reduce(jnp.maximum, parts)
            m_prev = m_ref[rows, :]
            m_new = jnp.maximum(m_prev, jnp.max(smax, axis=-1, keepdims=True))
            p = jnp.concatenate([jnp.exp2(x - m_new) for x in parts], axis=1).astype(BF16)
            alpha = jnp.exp2(m_prev - m_new)
            pv = _dot(p, vc)
            acc_ref[rows, :HEAD_DIM] = alpha * acc_ref[rows, :HEAD_DIM] + pv[:, :HEAD_DIM]
            acc_ref[rows, HEAD_DIM:] = alpha * acc_ref[rows, HEAD_DIM:] + pv[:, HEAD_DIM:]
            m_ref[rows, :] = m_new

    def far_chunk(c, carry):
        attend(c, None)
        return carry

    lax.fori_loop(0, j - 1, far_chunk, 0)

    @pl.when(j >= 1)
    def _():
        attend(j - 1, 1)

    attend(j, 0)

    for h in range(heads):
        rows = slice(h * qb, (h + 1) * qb)
        o = acc_ref[rows, :HEAD_DIM] / acc_ref[rows, HEAD_DIM:]
        o_ref[:, h * HEAD_DIM:(h + 1) * HEAD_DIM] = o.astype(o_ref.dtype)


def dsa_attention(proj, lay, wi, kidx, k, v, bias_tiles, batch, seq, heads, idx_heads):
    qb = ATT_BLOCK
    nqb = seq // qb
    aw = heads * HEAD_DIM
    iw = idx_heads * HEAD_DIM
    topk = min(IDX_TOPK, seq // 4)
    assert lay["q_i"] % QI_BLOCK == 0 and iw % QI_BLOCK == 0 and lay["q_a"] % aw == 0
    n_qi = iw // QI_BLOCK
    return pl.pallas_call(
        functools.partial(_dsa_kernel, heads=heads, idx_heads=idx_heads, topk=topk, n_qi=n_qi),
        grid=(batch, nqb),
        in_specs=[pl.BlockSpec((qb, aw), lambda b, j: (b * nqb + j, lay["q_a"] // aw)),
                  *[pl.BlockSpec((qb, QI_BLOCK), lambda b, j, t=t: (b * nqb + j, lay["q_i"] // QI_BLOCK + t))
                    for t in range(n_qi)],
                  pl.BlockSpec((qb, HEAD_DIM), lambda b, j: (b * nqb + j, 0)),
                  pl.BlockSpec((seq, HEAD_DIM), lambda b, j: (b, 0)),
                  pl.BlockSpec((seq, HEAD_DIM), lambda b, j: (b, 0)),
                  pl.BlockSpec((seq, 2 * HEAD_DIM), lambda b, j: (b, 0)),
                  pl.BlockSpec((2, heads * qb, qb), lambda b, j: (0, 0, 0),
                               pipeline_mode=pl.Buffered(1))],
        out_specs=pl.BlockSpec((qb, aw), lambda b, j: (b * nqb + j, 0)),
        out_shape=jax.ShapeDtypeStruct((batch * seq, aw), BF16),
        scratch_shapes=[pltpu.VMEM((heads * qb, HEAD_DIM), BF16),
                        pltpu.VMEM((idx_heads * qb, HEAD_DIM), BF16),
                        pltpu.VMEM((seq // qb, qb, qb), jnp.int32),
                        pltpu.VMEM((qb, qb), F32),
                        pltpu.VMEM((heads * qb, LANES), F32),
                        pltpu.VMEM((heads * qb, 2 * HEAD_DIM), F32)],
        compiler_params=_cparams(("parallel", "arbitrary")),
        name="dsa_attention",
    )(proj, *([proj] * n_qi), wi, kidx, k, v, bias_tiles)


def _t5_bucket(dist):
    max_exact = REL_BUCKETS // 2
    d = jnp.maximum(dist, 0)
    df = jnp.maximum(d, 1).astype(F32)
    large = max_exact + (jnp.log(df / max_exact) / math.log(REL_MAX_DIST / max_exact)
                         * (REL_BUCKETS - max_exact)).astype(jnp.int32)
    large = jnp.minimum(large, REL_BUCKETS - 1)
    return jnp.where(d < max_exact, d, large)


def rel_bias_tiles(rel_bias, heads):
    qb = ATT_BLOCK
    assert _static_far_bucket(qb) == REL_BUCKETS - 1
    tbl = (rel_bias[_t5_bucket(jnp.arange(4 * qb))] - rel_bias[REL_BUCKETS - 1][None, :]).T * LOG2E
    y = jnp.arange(2 * qb)
    tiles = []
    for off in (0, qb):
        d = off + jnp.where(y < qb, -y, 2 * qb - y)
        u = tbl[:, jnp.maximum(d, 0)]
        flat = jnp.broadcast_to(u[:, None, :], (heads, qb, 2 * qb)).reshape(heads, 2 * qb * qb)
        skew = flat[:, :qb * (2 * qb - 1)].reshape(heads, qb, 2 * qb - 1)
        tiles.append(skew[:, :, :qb].reshape(heads * qb, qb))
    return jnp.stack(tiles).astype(F32)


def _static_far_bucket(d):
    max_exact = REL_BUCKETS // 2
    v = max_exact + int(math.log(d / max_exact) / math.log(REL_MAX_DIST / max_exact)
                        * (REL_BUCKETS - max_exact))
    return min(v, REL_BUCKETS - 1)


def _hgrn_kernel(q_ref, f_ref, i_ref, g_ref, lb_ref, ng_ref, o_ref,
                 st_ref, a_ref, k_ref, qf_ref, p_ref, *, heads_per_step, tb):
    c = HGRN_CHUNK
    sub = HGRN_SUB
    nsub = c // sub
    hs = range(heads_per_step)

    @pl.when(pl.program_id(2) == 0)
    def _():
        st_ref[...] = jnp.zeros_like(st_ref)

    r_io = lax.broadcasted_iota(jnp.int32, (c, c), 0)
    c_io = lax.broadcasted_iota(jnp.int32, (c, c), 1)
    tri = (c_io <= r_io).astype(F32)
    sub_r = lax.broadcasted_iota(jnp.int32, (sub, c), 0)
    sub_c = lax.broadcasted_iota(jnp.int32, (sub, c), 1)
    ng = ng_ref[...]

    def a_start(hh, sb):
        return a_ref[hh, sb * sub - 1:sb * sub, :] if sb > 0 else jnp.zeros((1, HEAD_DIM), F32)

    def chunk(ci, carry):
        r0 = pl.multiple_of(ci * c, c)

        worst = jnp.zeros((1, HEAD_DIM), F32)
        for hh in hs:
            cols = slice(hh * HEAD_DIM, (hh + 1) * HEAD_DIM)
            lb = lb_ref[:, cols]
            z = f_ref[pl.ds(r0, c), cols].astype(F32)
            qx = q_ref[pl.ds(r0, c), cols].astype(F32)
            sig = jax.nn.sigmoid(z)
            lf = jnp.log2(lb + (1.0 - lb) * sig)
            k_ref[hh] = (1.0 - lb) * jax.nn.sigmoid(-z)
            qf_ref[hh] = qx * jax.nn.sigmoid(qx)
            a = jnp.dot(tri, lf, preferred_element_type=F32, precision=lax.Precision.HIGHEST)
            a_ref[hh] = a
            for sb in range(nsub):
                top = a[sb * sub - 1:sb * sub, :] if sb > 0 else jnp.zeros((1, HEAD_DIM), F32)
                worst = jnp.maximum(worst, top - a[(sb + 1) * sub - 1:(sb + 1) * sub, :])
        safe = jnp.max(worst) <= HGRN_SAFE_LOG2

        @pl.when(safe)
        def _():
            for hh in hs:
                a = a_ref[hh]
                kf = k_ref[hh]
                for sb in range(nsub):
                    rs = slice(sb * sub, (sb + 1) * sub)
                    top = a_start(hh, sb)
                    q_t = qf_ref[hh, rs, :] * jnp.exp2(a_ref[hh, rs, :] - top)
                    k_t = kf * jnp.exp2(top - a)
                    p = _dot_nt(q_t.astype(BF16), k_t.astype(BF16))
                    p_ref[hh, rs, :] = jnp.where(sub_c <= sub_r + sb * sub, p, 0.0)

        @pl.when(jnp.logical_not(safe))
        def _():
            for hh in hs:
                a = a_ref[hh]
                kf = k_ref[hh]
                for sb in range(nsub):
                    rs = slice(sb * sub, (sb + 1) * sub)
                    top = a_start(hh, sb)
                    a_blk = a_ref[hh, rs, :]
                    q_blk = qf_ref[hh, rs, :]
                    q_t = q_blk * jnp.exp2(a_blk - top)
                    k_t = kf * jnp.exp2(top - a)
                    p_off = _dot_nt(q_t.astype(BF16), k_t.astype(BF16))
                    p_diag = jnp.zeros((sub, c), F32)
                    for s in range(sub):
                        row = sb * sub + s
                        e = jnp.exp2(a_blk - a_ref[hh, row:row + 1, :])
                        col = jnp.sum(q_blk * e * k_ref[hh, row:row + 1, :], axis=-1, keepdims=True)
                        p_diag = jnp.where((sub_c == row) & (sub_r >= s), col, p_diag)
                    p_ref[hh, rs, :] = jnp.where(sub_c < sb * sub, p_off, p_diag)

        for hh in hs:
            cols = slice(hh * HEAD_DIM, (hh + 1) * HEAD_DIM)
            a = a_ref[hh]
            kf = k_ref[hh]
            v = i_ref[pl.ds(r0, c), cols].astype(F32)
            gx = g_ref[pl.ds(r0, c), cols].astype(F32)
            vb = v.astype(BF16)
            st = st_ref[hh]
            o = (_dot(p_ref[hh].astype(BF16), vb)
                 + _dot_nt((qf_ref[hh] * jnp.exp2(a)).astype(BF16), st.astype(BF16)))
            a_last = a[c - 1:c, :]
            k_end = kf * jnp.exp2(a_last - a)
            st_ref[hh] = st * jnp.exp2(a_last) + _dot(v.T.astype(BF16), k_end.astype(BF16))
            o = o * lax.rsqrt(jnp.mean(o * o, axis=-1, keepdims=True) + NORM_EPS) * ng
            o_ref[pl.ds(r0, c), cols] = (o * (gx * jax.nn.sigmoid(gx))).astype(o_ref.dtype)
        return carry

    lax.fori_loop(0, tb // c, chunk, 0)


def hgrn2(proj, lay, lb, norm_g, batch, seq, heads, heads_per_step=8, tb=512):
    heads_per_step = min(heads_per_step, heads)
    assert heads % heads_per_step == 0
    hw = heads_per_step * HEAD_DIM
    nt = seq // tb
    ng = heads // heads_per_step

    def col(name):
        base = lay[name] // hw
        return lambda b, h, t: (b * nt + t, base + h)

    return pl.pallas_call(
        functools.partial(_hgrn_kernel, heads_per_step=heads_per_step, tb=tb),
        grid=(batch, ng, nt),
        in_specs=[pl.BlockSpec((tb, hw), col("q_h")),
                  pl.BlockSpec((tb, hw), col("f_h")),
                  pl.BlockSpec((tb, hw), col("i_h")),
                  pl.BlockSpec((tb, hw), col("g_h")),
                  pl.BlockSpec((1, hw), lambda b, h, t: (0, h)),
                  pl.BlockSpec((1, HEAD_DIM), lambda b, h, t: (0, 0))],
        out_specs=pl.BlockSpec((tb, hw), lambda b, h, t: (b * nt + t, h)),
        out_shape=jax.ShapeDtypeStruct((batch * seq, heads * HEAD_DIM), BF16),
        scratch_shapes=[pltpu.VMEM((heads_per_step, HEAD_DIM, HEAD_DIM), F32),
                        pltpu.VMEM((heads_per_step, HGRN_CHUNK, HEAD_DIM), F32),
                        pltpu.VMEM((heads_per_step, HGRN_CHUNK, HEAD_DIM), F32),
                        pltpu.VMEM((heads_per_step, HGRN_CHUNK, HEAD_DIM), F32),
                        pltpu.VMEM((heads_per_step, HGRN_CHUNK, HGRN_CHUNK), F32)],
        compiler_params=_cparams(("parallel", "parallel", "arbitrary")),
        name="hgrn2",
    )(proj, proj, proj, proj, lb.reshape(1, -1), norm_g.reshape(1, HEAD_DIM))


def _merge_kernel(a_ref, b_ref, wa_ref, wb_ref, ga_ref, gb_ref, o_ref, wab_ref, wbb_ref):
    @pl.when(pl.program_id(1) == 0)
    def _():
        wab_ref[...] = wa_ref[...].astype(BF16)
        wbb_ref[...] = wb_ref[...].astype(BF16)

    ya = _dot(a_ref[...], wab_ref[...])
    yb = _dot(b_ref[...], wbb_ref[...])
    ga = jax.nn.sigmoid(ga_ref[...].astype(F32))
    gb = jax.nn.sigmoid(gb_ref[...].astype(F32))
    o_ref[...] = (ga * ya + gb * yb).astype(o_ref.dtype)


def gated_merge(attn, hg, w_pa, w_pb, layer, proj, lay, tm=1024, tn=512):
    m, ka = attn.shape
    kb = hg.shape[1]
    d = w_pa.shape[2]
    assert lay["gate_a"] % tn == 0 and lay["gate_b"] % tn == 0
    ga0, gb0 = lay["gate_a"] // tn, lay["gate_b"] // tn
    return pl.pallas_call(
        _merge_kernel,
        grid=(d // tn, m // tm),
        in_specs=[pl.BlockSpec((tm, ka), lambda j, i: (i, 0)),
                  pl.BlockSpec((tm, kb), lambda j, i: (i, 0)),
                  pl.BlockSpec((None, ka, tn), lambda j, i: (layer, 0, j)),
                  pl.BlockSpec((None, kb, tn), lambda j, i: (layer, 0, j)),
                  pl.BlockSpec((tm, tn), lambda j, i: (i, ga0 + j)),
                  pl.BlockSpec((tm, tn), lambda j, i: (i, gb0 + j))],
        out_specs=pl.BlockSpec((tm, tn), lambda j, i: (i, j)),
        out_shape=jax.ShapeDtypeStruct((m, d), BF16),
        scratch_shapes=[pltpu.VMEM((ka, tn), BF16), pltpu.VMEM((kb, tn), BF16)],
        compiler_params=_cparams(("parallel", "arbitrary")),
        name="gated_merge",
    )(attn, hg, w_pa, w_pb, proj, proj)


def _norm_residual_kernel(x_ref, y_ref, gp_ref, gn_ref, xo_ref, ho_ref):
    y = y_ref[...].astype(F32)
    yn = y * lax.rsqrt(jnp.mean(y * y, axis=-1, keepdims=True) + NORM_EPS) * gp_ref[...]
    x = x_ref[...] + yn
    xo_ref[...] = x
    ho_ref[...] = (x * lax.rsqrt(jnp.mean(x * x, axis=-1, keepdims=True) + NORM_EPS)
                   * gn_ref[...]).astype(ho_ref.dtype)


def _norm_residual_last_kernel(x_ref, y_ref, gp_ref, xo_ref):
    y = y_ref[...].astype(F32)
    xo_ref[...] = x_ref[...] + y * lax.rsqrt(jnp.mean(y * y, axis=-1, keepdims=True) + NORM_EPS) * gp_ref[...]


def norm_residual(x, y, g_post, g_next=None, rows=256):
    n, d = x.shape
    row = pl.BlockSpec((rows, d), lambda i: (i, 0))
    vec = pl.BlockSpec((1, d), lambda i: (0, 0))
    if g_next is None:
        return pl.pallas_call(
            _norm_residual_last_kernel,
            grid=(n // rows,),
            in_specs=[row, row, vec],
            out_specs=row,
            out_shape=jax.ShapeDtypeStruct((n, d), F32),
            compiler_params=_cparams(("parallel",)),
            name="norm_residual_last",
        )(x, y, g_post.reshape(1, d)), None
    return pl.pallas_call(
        _norm_residual_kernel,
        grid=(n // rows,),
        in_specs=[row, row, vec, vec],
        out_specs=[row, row],
        out_shape=[jax.ShapeDtypeStruct((n, d), F32), jax.ShapeDtypeStruct((n, d), BF16)],
        compiler_params=_cparams(("parallel",)),
        name="norm_residual",
    )(x, y, g_post.reshape(1, d), g_next.reshape(1, d))


GELU_C = math.sqrt(2.0 / math.pi)


def _gelu_tanh(x):
    z = x * (x * x * (-2.0 * GELU_C * 0.044715 * LOG2E) + (-2.0 * GELU_C * LOG2E))
    return x / (1.0 + jnp.exp2(z))


def _ffn_up_kernel(x_ref, wg_ref, wv_ref, cwg_ref, cwv_ref, cbg_ref, cbv_ref, o_ref,
                   wb_ref, u_ref, *, tm, tn, seq):
    i = pl.program_id(1)
    starts_seq = (i * tm) % seq == 0

    @pl.when(i == 0)
    def _():
        wb_ref[:, :tn] = wg_ref[...].astype(BF16)
        wb_ref[:, tn:] = wv_ref[...].astype(BF16)

    @pl.when(starts_seq)
    def _():
        u_ref[0:8, :] = jnp.zeros((8, 2 * tn), F32)

    @pl.when(jnp.logical_not(starts_seq))
    def _():
        u_ref[0:8, :] = u_ref[tm:tm + 8, :]

    u_ref[8:8 + tm, :] = _dot(x_ref[...], wb_ref[...])

    def conv(lo, w_ref, b_ref):
        cols = slice(lo, lo + tn)
        return (b_ref[...] + w_ref[0:1, :] * u_ref[6:6 + tm, cols]
                + w_ref[1:2, :] * u_ref[7:7 + tm, cols] + w_ref[2:3, :] * u_ref[8:8 + tm, cols])

    gate = conv(0, cwg_ref, cbg_ref)
    val = conv(tn, cwv_ref, cbv_ref)
    o_ref[...] = (_gelu_tanh(gate) * val).astype(o_ref.dtype)


def ffn_up_conv_geglu(h, w_up, conv_w, conv_b, layer, seq, tm, tn):
    m, kdim = h.shape
    dff = w_up.shape[2] // 2
    assert m % tm == 0 and dff % tn == 0 and seq % tm == 0
    nc = dff // tn
    wspec = lambda off: pl.BlockSpec((None, kdim, tn), lambda j, i: (layer, 0, off + j))
    cspec = lambda r, off: pl.BlockSpec((None, r, tn), lambda j, i: (layer, 0, off + j))
    return pl.pallas_call(
        functools.partial(_ffn_up_kernel, tm=tm, tn=tn, seq=seq),
        grid=(nc, m // tm),
        in_specs=[pl.BlockSpec((tm, kdim), lambda j, i: (i, 0)),
                  wspec(0), wspec(nc),
                  cspec(CONV_WIDTH, 0), cspec(CONV_WIDTH, nc),
                  cspec(1, 0), cspec(1, nc)],
        out_specs=pl.BlockSpec((tm, tn), lambda j, i: (i, j)),
        out_shape=jax.ShapeDtypeStruct((m, dff), BF16),
        scratch_shapes=[pltpu.VMEM((kdim, 2 * tn), BF16), pltpu.VMEM((tm + 8, 2 * tn), F32)],
        compiler_params=_cparams(("parallel", "arbitrary")),
        name="ffn_up",
    )(h, w_up, w_up, conv_w, conv_w, conv_b[:, None, :], conv_b[:, None, :])


def _in_layout(d, aw, iw, ih, hk, hv, kvl):
    segs = [("q_a", aw), ("c_kv", kvl), ("q_i", iw), ("k_i", HEAD_DIM), ("w_i", ih), ("pad", None),
            ("q_h", hk), ("f_h", hk), ("i_h", hv), ("g_h", hv), ("gate_a", d), ("gate_b", d)]
    lay, off = {}, 0
    for name, w in segs:
        if name == "pad":
            w = (-off) % IN_PAD_ALIGN
            lay["pad_width"] = w
        lay[name] = off
        off += w
    assert off % IN_PROJ_TN == 0
    lay["total"] = off
    return lay


def _pad_cast_kernel(w0, w1, w2, w3, w4, o_ref, *, n_pad, n_tail, ih):
    n = pl.program_id(0)
    blocks = (w0, w1, w2, w3, w4)
    lane = lax.broadcasted_iota(jnp.int32, (1, LANES), 1)

    @pl.when(n < n_pad)
    def _():
        for d in range(4):
            o_ref[:, d * LANES:(d + 1) * LANES] = blocks[d][...].astype(BF16)

    @pl.when(n == n_pad)
    def _():
        o_ref[:, :LANES] = w0[...].astype(BF16)
        o_ref[:, LANES:2 * LANES] = jnp.where(lane < ih, w1[...], 0.0).astype(BF16)
        o_ref[:, 2 * LANES:] = jnp.zeros((o_ref.shape[0], 2 * LANES), BF16)

    @pl.when((n > n_pad) & (n < n_tail))
    def _():
        o_ref[...] = jnp.zeros_like(o_ref)

    @pl.when(n >= n_tail)
    def _():
        for d in range(4):
            stitched = jnp.concatenate([blocks[d][:, ih:], blocks[d + 1][:, :ih]], axis=1)
            o_ref[:, d * LANES:(d + 1) * LANES] = stitched.astype(BF16)


def pad_cast_w_in(w_in, layer, lay, ih):
    kdim, n_src = w_in.shape[1], w_in.shape[2]
    per = IN_PROJ_TN // LANES
    assert per == 4 and 0 < ih < LANES and lay["k_i"] % IN_PROJ_TN == 0 and lay["q_h"] % IN_PROJ_TN == 0
    n_pad = lay["k_i"] // IN_PROJ_TN
    n_tail = lay["q_h"] // IN_PROJ_TN
    back = (lay["q_h"] - lay["k_i"] - HEAD_DIM) // LANES
    last = (n_src - 1) // LANES

    def src(d):
        def index(n):
            blk = jnp.where(n <= n_pad, per * n + d, per * n - back + d)
            return (layer, 0, jnp.clip(blk, 0, last))
        return pl.BlockSpec((None, kdim, LANES), index)

    return pl.pallas_call(
        functools.partial(_pad_cast_kernel, n_pad=n_pad, n_tail=n_tail, ih=ih),
        grid=(lay["total"] // IN_PROJ_TN,),
        in_specs=[src(d) for d in range(per + 1)],
        out_specs=pl.BlockSpec((kdim, IN_PROJ_TN), lambda n: (0, n)),
        out_shape=jax.ShapeDtypeStruct((kdim, lay["total"]), BF16),
        compiler_params=_cparams(("parallel",)),
        name="pad_cast_w_in",
    )(*([w_in] * (per + 1)))


def _pick(n, cands):
    for c in cands:
        if n % c == 0:
            return c
    raise ValueError(f"no tile for {n}")


def kernel(x, rel_bias, hgrn_lb_logits, mix_pre_g, mix_post_g, w_in, kv_norm_g, w_uk, w_uv, idx_k_ln_g, idx_k_ln_b, hgrn_norm_g, w_proj_attn, w_proj_hgrn, w_out, ffn_pre_g, ffn_post_g, w_up, conv_w, conv_b, w_down):
    batch, seq, d = x.shape
    depth = w_in.shape[0]
    aw = w_proj_attn.shape[1]
    hv = w_proj_hgrn.shape[1]
    hk = hgrn_lb_logits.shape[1]
    kvl = w_uk.shape[1]
    dff = w_down.shape[1]
    heads = aw // HEAD_DIM
    hheads = hv // HEAD_DIM
    ih = w_in.shape[2] - (aw + kvl + HEAD_DIM + 2 * hk + 2 * hv + 2 * d)
    ih = ih // (HEAD_DIM + 1)
    iw = ih * HEAD_DIM
    lay = _in_layout(d, aw, iw, ih, hk, hv, kvl)
    n = batch * seq

    lb_p = jax.nn.softmax(hgrn_lb_logits.astype(F32), axis=0)
    lb_all = jnp.cumsum(lb_p, axis=0) - lb_p[0:1]
    bias_tiles = rel_bias_tiles(rel_bias, heads)

    tm = _pick(n, (1024, 512, 256))
    xf = x.reshape(n, d)
    h = rmsnorm_cast(xf, mix_pre_g[0])
    for l in range(depth):
        w_in_l = pad_cast_w_in(w_in, l, lay, ih)
        proj = matmul(h, w_in_l, tm, IN_PROJ_TN, name="in_proj")
        k, v, kidx, wi = dsa_prep(proj, lay, kv_norm_g[l], w_uk[l], w_uv[l],
                                  idx_k_ln_g[l], idx_k_ln_b[l], ih)
        attn = dsa_attention(proj, lay, wi, kidx, k, v, bias_tiles, batch, seq, heads, ih)
        hg = hgrn2(proj, lay, lb_all[l], hgrn_norm_g[l], batch, seq, hheads)
        merged = gated_merge(attn, hg, w_proj_attn, w_proj_hgrn, l, proj, lay, tm=tm)
        y = matmul_wstat(merged, w_out, l, tm, 512, name="out_proj")
        xf, h = norm_residual(xf, y, mix_post_g[l], ffn_pre_g[l])
        act = ffn_up_conv_geglu(h, w_up, conv_w, conv_b, l, seq, tm, _pick(dff, (256, 128)))
        y = matmul(act, w_down[l].astype(BF16), tm, 512, tk=_pick(dff, (dff // 2, dff)),
                   name="ffn_down")
        xf, h = norm_residual(xf, y, ffn_post_g[l], mix_pre_g[l + 1] if l + 1 < depth else None)
    return xf.reshape(batch, seq, d)
```

```python
import functools
import math

import jax
import jax.numpy as jnp
from jax import lax
from jax.experimental import pallas as pl
from jax.experimental.pallas import tpu as pltpu

F32 = jnp.float32
BF16 = jnp.bfloat16

NORM_EPS = 1e-6
HEAD_DIM = 128
IDX_TOPK = 256
REL_BUCKETS = 32
REL_MAX_DIST = 128
CONV_WIDTH = 3
LANES = 128
VMEM_LIMIT_BYTES = 56 * 1024 * 1024

ATT_BLOCK = 256
QI_BLOCK = 512
HGRN_CHUNK = 64
HGRN_SUB = 16
HGRN_SAFE_LOG2 = 100.0
IN_PROJ_TN = 512
IN_PAD_ALIGN = 1024
INT_MIN = -2 ** 31
MASK_NEG = -1e30
M_INIT = -1e20
LOG2E = math.log2(math.e)


def _cparams(sem):
    return pltpu.CompilerParams(dimension_semantics=sem, vmem_limit_bytes=VMEM_LIMIT_BYTES)


def _dot(a, b):
    return jnp.dot(a, b, preferred_element_type=F32)


def _dot_nt(a, b):
    return lax.dot_general(a, b, (((1,), (1,)), ((), ())), preferred_element_type=F32)


def _rmsnorm_kernel(x_ref, g_ref, o_ref):
    x = x_ref[...]
    r = lax.rsqrt(jnp.mean(x * x, axis=-1, keepdims=True) + NORM_EPS)
    o_ref[...] = (x * r * g_ref[...]).astype(o_ref.dtype)


def rmsnorm_cast(x, g, rows=256):
    n, d = x.shape
    return pl.pallas_call(
        _rmsnorm_kernel,
        grid=(n // rows,),
        in_specs=[pl.BlockSpec((rows, d), lambda i: (i, 0)),
                  pl.BlockSpec((1, d), lambda i: (0, 0))],
        out_specs=pl.BlockSpec((rows, d), lambda i: (i, 0)),
        out_shape=jax.ShapeDtypeStruct((n, d), BF16),
        compiler_params=_cparams(("parallel",)),
        name="rmsnorm_cast",
    )(x, g.reshape(1, d))


def _mm_kernel(x_ref, w_ref, o_ref):
    o_ref[...] = _dot(x_ref[...], w_ref[...]).astype(o_ref.dtype)


def _mm_acc_kernel(x_ref, w_ref, o_ref, acc_ref, *, nk):
    k = pl.program_id(2)

    @pl.when(k == 0)
    def _():
        acc_ref[...] = jnp.zeros_like(acc_ref)

    acc_ref[...] += _dot(x_ref[...], w_ref[...])

    @pl.when(k == nk - 1)
    def _():
        o_ref[...] = acc_ref[...].astype(o_ref.dtype)


def matmul(x, w, tm, tn, tk=None, out_dtype=BF16, name="matmul"):
    m, kdim = x.shape
    n = w.shape[1]
    tk = kdim if tk is None else tk
    nk = kdim // tk
    assert m % tm == 0 and n % tn == 0 and kdim % tk == 0
    if nk == 1:
        return pl.pallas_call(
            _mm_kernel,
            grid=(m // tm, n // tn),
            in_specs=[pl.BlockSpec((tm, kdim), lambda i, j: (i, 0)),
                      pl.BlockSpec((kdim, tn), lambda i, j: (0, j))],
            out_specs=pl.BlockSpec((tm, tn), lambda i, j: (i, j)),
            out_shape=jax.ShapeDtypeStruct((m, n), out_dtype),
            compiler_params=_cparams(("parallel", "parallel")),
            name=name,
        )(x, w)
    return pl.pallas_call(
        functools.partial(_mm_acc_kernel, nk=nk),
        grid=(m // tm, n // tn, nk),
        in_specs=[pl.BlockSpec((tm, tk), lambda i, j, k: (i, k)),
                  pl.BlockSpec((tk, tn), lambda i, j, k: (k, j))],
        out_specs=pl.BlockSpec((tm, tn), lambda i, j, k: (i, j)),
        out_shape=jax.ShapeDtypeStruct((m, n), out_dtype),
        scratch_shapes=[pltpu.VMEM((tm, tn), F32)],
        compiler_params=_cparams(("parallel", "parallel", "arbitrary")),
        name=name,
    )(x, w)


def _mm_wstat_kernel(x_ref, w_ref, o_ref, wb_ref):
    @pl.when(pl.program_id(1) == 0)
    def _():
        wb_ref[...] = w_ref[...].astype(BF16)

    o_ref[...] = _dot(x_ref[...], wb_ref[...]).astype(o_ref.dtype)


def matmul_wstat(x, w, layer, tm, tn, out_dtype=BF16, name="matmul_wstat"):
    m, kdim = x.shape
    n = w.shape[2]
    assert m % tm == 0 and n % tn == 0
    return pl.pallas_call(
        _mm_wstat_kernel,
        grid=(n // tn, m // tm),
        in_specs=[pl.BlockSpec((tm, kdim), lambda j, i: (i, 0)),
                  pl.BlockSpec((None, kdim, tn), lambda j, i: (layer, 0, j))],
        out_specs=pl.BlockSpec((tm, tn), lambda j, i: (i, j)),
        out_shape=jax.ShapeDtypeStruct((m, n), out_dtype),
        scratch_shapes=[pltpu.VMEM((kdim, tn), BF16)],
        compiler_params=_cparams(("parallel", "arbitrary")),
        name=name,
    )(x, w)


def _dsa_prep_kernel(ckv_ref, kw_ref, kvg_ref, wuk_ref, wuv_ref, lng_ref, lnb_ref,
                     k_ref, v_ref, kidx_ref, wi_ref, *, k_scale, w_scale):
    c = ckv_ref[...].astype(F32)
    c = c * lax.rsqrt(jnp.mean(c * c, axis=-1, keepdims=True) + NORM_EPS) * kvg_ref[...]
    cb = c.astype(BF16)
    k_ref[...] = (_dot(cb, wuk_ref[...]) * k_scale).astype(k_ref.dtype)
    v_ref[:, :HEAD_DIM] = _dot(cb, wuv_ref[...]).astype(v_ref.dtype)
    v_ref[:, HEAD_DIM:] = jnp.ones((v_ref.shape[0], HEAD_DIM), v_ref.dtype)
    kw = kw_ref[...].astype(F32)
    ki = kw[:, :HEAD_DIM]
    mu = jnp.mean(ki, axis=-1, keepdims=True)
    kc = ki - mu
    kn = kc * lax.rsqrt(jnp.mean(kc * kc, axis=-1, keepdims=True) + NORM_EPS)
    kidx_ref[...] = (kn * lng_ref[...] + lnb_ref[...]).astype(kidx_ref.dtype)
    wi_ref[...] = kw[:, HEAD_DIM:] * w_scale


def dsa_prep(proj, lay, kv_norm_g, w_uk, w_uv, ln_g, ln_b, idx_heads, rows=512):
    n = proj.shape[0]
    kvl = w_uk.shape[0]
    row = lambda i: (i, 0)
    const = lambda i: (0, 0)
    out = jax.ShapeDtypeStruct((n, HEAD_DIM), BF16)
    return pl.pallas_call(
        functools.partial(_dsa_prep_kernel, k_scale=HEAD_DIM ** -0.5 * LOG2E,
                          w_scale=idx_heads ** -0.5 * HEAD_DIM ** -0.5),
        grid=(n // rows,),
        in_specs=[pl.BlockSpec((rows, kvl), lambda i: (i, lay["c_kv"] // kvl)),
                  pl.BlockSpec((rows, 2 * HEAD_DIM), lambda i: (i, lay["k_i"] // (2 * HEAD_DIM))),
                  pl.BlockSpec((1, kvl), const),
                  pl.BlockSpec((kvl, HEAD_DIM), const),
                  pl.BlockSpec((kvl, HEAD_DIM), const),
                  pl.BlockSpec((1, HEAD_DIM), const),
                  pl.BlockSpec((1, HEAD_DIM), const)],
        out_specs=[pl.BlockSpec((rows, HEAD_DIM), row), pl.BlockSpec((rows, 2 * HEAD_DIM), row),
                   pl.BlockSpec((rows, HEAD_DIM), row), pl.BlockSpec((rows, HEAD_DIM), row)],
        out_shape=[out, jax.ShapeDtypeStruct((n, 2 * HEAD_DIM), BF16), out,
                   jax.ShapeDtypeStruct((n, HEAD_DIM), F32)],
        compiler_params=_cparams(("parallel",)),
        name="dsa_prep",
    )(proj, proj, kv_norm_g.reshape(1, kvl), w_uk.astype(BF16), w_uv.astype(BF16),
      ln_g.reshape(1, HEAD_DIM), ln_b.reshape(1, HEAD_DIM))


def _sortable_key(x):
    u = lax.bitcast_convert_type(x, jnp.int32)
    return u ^ ((u >> 31) & jnp.int32(0x7FFFFFFF))


def _dsa_kernel(*refs, heads, idx_heads, topk, n_qi):
    qa_ref = refs[0]
    qi_refs = refs[1:1 + n_qi]
    (wi_ref, kidx_ref, k_ref, v_ref, bias_ref, o_ref,
     qas_ref, qis_ref, keys_ref, mask_ref, m_ref, acc_ref) = refs[1 + n_qi:]
    qb = ATT_BLOCK
    ck = ATT_BLOCK
    j = pl.program_id(1)

    for h in range(heads):
        qas_ref[h * qb:(h + 1) * qb, :] = qa_ref[:, h * HEAD_DIM:(h + 1) * HEAD_DIM]
    for h in range(idx_heads):
        src, off = divmod(h * HEAD_DIM, QI_BLOCK)
        qis_ref[h * qb:(h + 1) * qb, :] = qi_refs[src][:, off:off + HEAD_DIM]

    row_pos = j * qb + lax.broadcasted_iota(jnp.int32, (qb, ck), 0)
    lane = lax.broadcasted_iota(jnp.int32, (qb, ck), 1)
    hg = 8 if idx_heads % 8 == 0 else idx_heads

    def score_chunk(c, carry):
        kc = kidx_ref[pl.ds(pl.multiple_of(c * ck, ck), ck), :]
        acc = jnp.zeros((qb, ck), F32)
        for g in range(idx_heads // hg):
            s = _dot_nt(qis_ref[g * hg * qb:(g + 1) * hg * qb, :], kc)
            for hh in range(hg):
                h = g * hg + hh
                acc = acc + jnp.maximum(s[hh * qb:(hh + 1) * qb, :], 0.0) * wi_ref[:, h:h + 1]
        valid = (c * ck + lane) <= row_pos
        keys_ref[c] = jnp.where(valid, _sortable_key(acc), jnp.int32(INT_MIN))
        return carry

    lax.fori_loop(0, j + 1, score_chunk, 0)

    def bit_step(i, lo):
        cand = lo + lax.shift_left(jnp.int32(1), jnp.int32(31) - i)

        def count_chunk(c, part):
            ge = (keys_ref[c] >= cand).astype(jnp.int32)
            for t in range(ck // LANES):
                part = part + ge[:, t * LANES:(t + 1) * LANES]
            return part

        part = lax.fori_loop(0, j + 1, count_chunk, jnp.zeros((qb, LANES), jnp.int32))
        cnt = jnp.sum(part, axis=-1, keepdims=True)
        return jnp.where(cnt >= topk, cand, lo)

    lo = lax.fori_loop(0, 32, bit_step, jnp.full((qb, 1), INT_MIN, jnp.int32))
    thr = jnp.maximum(lo, jnp.int32(INT_MIN + 1))

    m_ref[...] = jnp.full_like(m_ref, M_INIT)
    acc_ref[...] = jnp.zeros_like(acc_ref)

    def attend(c, bias_idx):
        off = pl.multiple_of(c * ck, ck)
        kc = k_ref[pl.ds(off, ck), :]
        vc = v_ref[pl.ds(off, ck), :]
        mask_ref[...] = jnp.where(keys_ref[c] >= thr, 0.0, MASK_NEG)
        for h in range(heads):
            rows = slice(h * qb, (h + 1) * qb)
            s = _dot_nt(qas_ref[rows, :], kc) + mask_ref[...]
            if bias_idx is not None:
                s = s + bias_ref[bias_idx, rows, :]
            parts = [s[:, t * LANES:(t + 1) * LANES] for t in range(ck // LANES)]
            smax = functools.reduce(jnp.maximum, parts)
            m_prev = m_ref[rows, :]
            m_new = jnp.maximum(m_prev, jnp.max(smax, axis=-1, keepdims=True))
            p = jnp.concatenate([jnp.exp2(x - m_new) for x in parts], axis=1).astype(BF16)
            alpha = jnp.exp2(m_prev - m_new)
            pv = _dot(p, vc)
            acc_ref[rows, :HEAD_DIM] = alpha * acc_ref[rows, :HEAD_DIM] + pv[:, :HEAD_DIM]
            acc_ref[rows, HEAD_DIM:] = alpha * acc_ref[rows, HEAD_DIM:] + pv[:, HEAD_DIM:]
            m_ref[rows, :] = m_new

    def far_chunk(c, carry):
        attend(c, None)
        return carry

    lax.fori_loop(0, j - 1, far_chunk, 0)

    @pl.when(j >= 1)
    def _():
        attend(j - 1, 1)

    attend(j, 0)

    for h in range(heads):
        rows = slice(h * qb, (h + 1) * qb)
        o = acc_ref[rows, :HEAD_DIM] / acc_ref[rows, HEAD_DIM:]
        o_ref[:, h * HEAD_DIM:(h + 1) * HEAD_DIM] = o.astype(o_ref.dtype)


def dsa_attention(proj, lay, wi, kidx, k, v, bias_tiles, batch, seq, heads, idx_heads):
    qb = ATT_BLOCK
    nqb = seq // qb
    aw = heads * HEAD_DIM
    iw = idx_heads * HEAD_DIM
    topk = min(IDX_TOPK, seq // 4)
    assert lay["q_i"] % QI_BLOCK == 0 and iw % QI_BLOCK == 0 and lay["q_a"] % aw == 0
    n_qi = iw // QI_BLOCK
    return pl.pallas_call(
        functools.partial(_dsa_kernel, heads=heads, idx_heads=idx_heads, topk=topk, n_qi=n_qi),
        grid=(batch, nqb),
        in_specs=[pl.BlockSpec((qb, aw), lambda b, j: (b * nqb + j, lay["q_a"] // aw)),
                  *[pl.BlockSpec((qb, QI_BLOCK), lambda b, j, t=t: (b * nqb + j, lay["q_i"] // QI_BLOCK + t))
                    for t in range(n_qi)],
                  pl.BlockSpec((qb, HEAD_DIM), lambda b, j: (b * nqb + j, 0)),
                  pl.BlockSpec((seq, HEAD_DIM), lambda b, j: (b, 0)),
                  pl.BlockSpec((seq, HEAD_DIM), lambda b, j: (b, 0)),
                  pl.BlockSpec((seq, 2 * HEAD_DIM), lambda b, j: (b, 0)),
                  pl.BlockSpec((2, heads * qb, qb), lambda b, j: (0, 0, 0),
                               pipeline_mode=pl.Buffered(1))],
        out_specs=pl.BlockSpec((qb, aw), lambda b, j: (b * nqb + j, 0)),
        out_shape=jax.ShapeDtypeStruct((batch * seq, aw), BF16),
        scratch_shapes=[pltpu.VMEM((heads * qb, HEAD_DIM), BF16),
                        pltpu.VMEM((idx_heads * qb, HEAD_DIM), BF16),
                        pltpu.VMEM((seq // qb, qb, qb), jnp.int32),
                        pltpu.VMEM((qb, qb), F32),
                        pltpu.VMEM((heads * qb, LANES), F32),
                        pltpu.VMEM((heads * qb, 2 * HEAD_DIM), F32)],
        compiler_params=_cparams(("parallel", "arbitrary")),
        name="dsa_attention",
    )(proj, *([proj] * n_qi), wi, kidx, k, v, bias_tiles)


def _t5_bucket(dist):
    max_exact = REL_BUCKETS // 2
    d = jnp.maximum(dist, 0)
    df = jnp.maximum(d, 1).astype(F32)
    large = max_exact + (jnp.log(df / max_exact) / math.log(REL_MAX_DIST / max_exact)
                         * (REL_BUCKETS - max_exact)).astype(jnp.int32)
    large = jnp.minimum(large, REL_BUCKETS - 1)
    return jnp.where(d < max_exact, d, large)


def rel_bias_tiles(rel_bias, heads):
    qb = ATT_BLOCK
    assert _static_far_bucket(qb) == REL_BUCKETS - 1
    tbl = (rel_bias[_t5_bucket(jnp.arange(4 * qb))] - rel_bias[REL_BUCKETS - 1][None, :]).T * LOG2E
    y = jnp.arange(2 * qb)
    tiles = []
    for off in (0, qb):
        d = off + jnp.where(y < qb, -y, 2 * qb - y)
        u = tbl[:, jnp.maximum(d, 0)]
        flat = jnp.broadcast_to(u[:, None, :], (heads, qb, 2 * qb)).reshape(heads, 2 * qb * qb)
        skew = flat[:, :qb * (2 * qb - 1)].reshape(heads, qb, 2 * qb - 1)
        tiles.append(skew[:, :, :qb].reshape(heads * qb, qb))
    return jnp.stack(tiles).astype(F32)


def _static_far_bucket(d):
    max_exact = REL_BUCKETS // 2
    v = max_exact + int(math.log(d / max_exact) / math.log(REL_MAX_DIST / max_exact)
                        * (REL_BUCKETS - max_exact))
    return min(v, REL_BUCKETS - 1)


def _hgrn_kernel(q_ref, f_ref, i_ref, g_ref, lb_ref, ng_ref, o_ref,
                 st_ref, a_ref, k_ref, qf_ref, p_ref, *, heads_per_step, tb):
    c = HGRN_CHUNK
    sub = HGRN_SUB
    nsub = c // sub
    hs = range(heads_per_step)

    @pl.when(pl.program_id(2) == 0)
    def _():
        st_ref[...] = jnp.zeros_like(st_ref)

    r_io = lax.broadcasted_iota(jnp.int32, (c, c), 0)
    c_io = lax.broadcasted_iota(jnp.int32, (c, c), 1)
    tri = (c_io <= r_io).astype(F32)
    sub_r = lax.broadcasted_iota(jnp.int32, (sub, c), 0)
    sub_c = lax.broadcasted_iota(jnp.int32, (sub, c), 1)
    ng = ng_ref[...]

    def a_start(hh, sb):
        return a_ref[hh, sb * sub - 1:sb * sub, :] if sb > 0 else jnp.zeros((1, HEAD_DIM), F32)

    def chunk(ci, carry):
        r0 = pl.multiple_of(ci * c, c)

        worst = jnp.zeros((1, HEAD_DIM), F32)
        for hh in hs:
            cols = slice(hh * HEAD_DIM, (hh + 1) * HEAD_DIM)
            lb = lb_ref[:, cols]
            z = f_ref[pl.ds(r0, c), cols].astype(F32)
            qx = q_ref[pl.ds(r0, c), cols].astype(F32)
            sig = jax.nn.sigmoid(z)
            lf = jnp.log2(lb + (1.0 - lb) * sig)
            k_ref[hh] = (1.0 - lb) * jax.nn.sigmoid(-z)
            qf_ref[hh] = qx * jax.nn.sigmoid(qx)
            a = jnp.dot(tri, lf, preferred_element_type=F32, precision=lax.Precision.HIGHEST)
            a_ref[hh] = a
            for sb in range(nsub):
                top = a[sb * sub - 1:sb * sub, :] if sb > 0 else jnp.zeros((1, HEAD_DIM), F32)
                worst = jnp.maximum(worst, top - a[(sb + 1) * sub - 1:(sb + 1) * sub, :])
        safe = jnp.max(worst) <= HGRN_SAFE_LOG2

        @pl.when(safe)
        def _():
            for hh in hs:
                a = a_ref[hh]
                kf = k_ref[hh]
                for sb in range(nsub):
                    rs = slice(sb * sub, (sb + 1) * sub)
                    top = a_start(hh, sb)
                    q_t = qf_ref[hh, rs, :] * jnp.exp2(a_ref[hh, rs, :] - top)
                    k_t = kf * jnp.exp2(top - a)
                    p = _dot_nt(q_t.astype(BF16), k_t.astype(BF16))
                    p_ref[hh, rs, :] = jnp.where(sub_c <= sub_r + sb * sub, p, 0.0)

        @pl.when(jnp.logical_not(safe))
        def _():
            for hh in hs:
                a = a_ref[hh]
                kf = k_ref[hh]
                for sb in range(nsub):
                    rs = slice(sb * sub, (sb + 1) * sub)
                    top = a_start(hh, sb)
                    a_blk = a_ref[hh, rs, :]
                    q_blk = qf_ref[hh, rs, :]
                    q_t = q_blk * jnp.exp2(a_blk - top)
                    k_t = kf * jnp.exp2(top - a)
                    p_off = _dot_nt(q_t.astype(BF16), k_t.astype(BF16))
                    p_diag = jnp.zeros((sub, c), F32)
                    for s in range(sub):
                        row = sb * sub + s
                        e = jnp.exp2(a_blk - a_ref[hh, row:row + 1, :])
                        col = jnp.sum(q_blk * e * k_ref[hh, row:row + 1, :], axis=-1, keepdims=True)
                        p_diag = jnp.where((sub_c == row) & (sub_r >= s), col, p_diag)
                    p_ref[hh, rs, :] = jnp.where(sub_c < sb * sub, p_off, p_diag)

        for hh in hs:
            cols = slice(hh * HEAD_DIM, (hh + 1) * HEAD_DIM)
            a = a_ref[hh]
            kf = k_ref[hh]
            v = i_ref[pl.ds(r0, c), cols].astype(F32)
            gx = g_ref[pl.ds(r0, c), cols].astype(F32)
            vb = v.astype(BF16)
            st = st_ref[hh]
            o = (_dot(p_ref[hh].astype(BF16), vb)
                 + _dot_nt((qf_ref[hh] * jnp.exp2(a)).astype(BF16), st.astype(BF16)))
            a_last = a[c - 1:c, :]
            k_end = kf * jnp.exp2(a_last - a)
            st_ref[hh] = st * jnp.exp2(a_last) + _dot(v.T.astype(BF16), k_end.astype(BF16))
            o = o * lax.rsqrt(jnp.mean(o * o, axis=-1, keepdims=True) + NORM_EPS) * ng
            o_ref[pl.ds(r0, c), cols] = (o * (gx * jax.nn.sigmoid(gx))).astype(o_ref.dtype)
        return carry

    lax.fori_loop(0, tb // c, chunk, 0)


def hgrn2(proj, lay, lb, norm_g, batch, seq, heads, heads_per_step=8, tb=512):
    heads_per_step = min(heads_per_step, heads)
    assert heads % heads_per_step == 0
    hw = heads_per_step * HEAD_DIM
    nt = seq // tb
    ng = heads // heads_per_step

    def col(name):
        base = lay[name] // hw
        return lambda b, h, t: (b * nt + t, base + h)

    return pl.pallas_call(
        functools.partial(_hgrn_kernel, heads_per_step=heads_per_step, tb=tb),
        grid=(batch, ng, nt),
        in_specs=[pl.BlockSpec((tb, hw), col("q_h")),
                  pl.BlockSpec((tb, hw), col("f_h")),
                  pl.BlockSpec((tb, hw), col("i_h")),
                  pl.BlockSpec((tb, hw), col("g_h")),
                  pl.BlockSpec((1, hw), lambda b, h, t: (0, h)),
                  pl.BlockSpec((1, HEAD_DIM), lambda b, h, t: (0, 0))],
        out_specs=pl.BlockSpec((tb, hw), lambda b, h, t: (b * nt + t, h)),
        out_shape=jax.ShapeDtypeStruct((batch * seq, heads * HEAD_DIM), BF16),
        scratch_shapes=[pltpu.VMEM((heads_per_step, HEAD_DIM, HEAD_DIM), F32),
                        pltpu.VMEM((heads_per_step, HGRN_CHUNK, HEAD_DIM), F32),
                        pltpu.VMEM((heads_per_step, HGRN_CHUNK, HEAD_DIM), F32),
                        pltpu.VMEM((heads_per_step, HGRN_CHUNK, HEAD_DIM), F32),
                        pltpu.VMEM((heads_per_step, HGRN_CHUNK, HGRN_CHUNK), F32)],
        compiler_params=_cparams(("parallel", "parallel", "arbitrary")),
        name="hgrn2",
    )(proj, proj, proj, proj, lb.reshape(1, -1), norm_g.reshape(1, HEAD_DIM))


def _merge_kernel(a_ref, b_ref, wa_ref, wb_ref, ga_ref, gb_ref, o_ref, wab_ref, wbb_ref):
    @pl.when(pl.program_id(1) == 0)
    def _():
        wab_ref[...] = wa_ref[...].astype(BF16)
        wbb_ref[...] = wb_ref[...].astype(BF16)

    ya = _dot(a_ref[...], wab_ref[...])
    yb = _dot(b_ref[...], wbb_ref[...])
    ga = jax.nn.sigmoid(ga_ref[...].astype(F32))
    gb = jax.nn.sigmoid(gb_ref[...].astype(F32))
    o_ref[...] = (ga * ya + gb * yb).astype(o_ref.dtype)


def gated_merge(attn, hg, w_pa, w_pb, layer, proj, lay, tm=1024, tn=512):
    m, ka = attn.shape
    kb = hg.shape[1]
    d = w_pa.shape[2]
    assert lay["gate_a"] % tn == 0 and lay["gate_b"] % tn == 0
    ga0, gb0 = lay["gate_a"] // tn, lay["gate_b"] // tn
    return pl.pallas_call(
        _merge_kernel,
        grid=(d // tn, m // tm),
        in_specs=[pl.BlockSpec((tm, ka), lambda j, i: (i, 0)),
                  pl.BlockSpec((tm, kb), lambda j, i: (i, 0)),
                  pl.BlockSpec((None, ka, tn), lambda j, i: (layer, 0, j)),
                  pl.BlockSpec((None, kb, tn), lambda j, i: (layer, 0, j)),
                  pl.BlockSpec((tm, tn), lambda j, i: (i, ga0 + j)),
                  pl.BlockSpec((tm, tn), lambda j, i: (i, gb0 + j))],
        out_specs=pl.BlockSpec((tm, tn), lambda j, i: (i, j)),
        out_shape=jax.ShapeDtypeStruct((m, d), BF16),
        scratch_shapes=[pltpu.VMEM((ka, tn), BF16), pltpu.VMEM((kb, tn), BF16)],
        compiler_params=_cparams(("parallel", "arbitrary")),
        name="gated_merge",
    )(attn, hg, w_pa, w_pb, proj, proj)


def _norm_residual_kernel(x_ref, y_ref, gp_ref, gn_ref, xo_ref, ho_ref):
    y = y_ref[...].astype(F32)
    yn = y * lax.rsqrt(jnp.mean(y * y, axis=-1, keepdims=True) + NORM_EPS) * gp_ref[...]
    x = x_ref[...] + yn
    xo_ref[...] = x
    ho_ref[...] = (x * lax.rsqrt(jnp.mean(x * x, axis=-1, keepdims=True) + NORM_EPS)
                   * gn_ref[...]).astype(ho_ref.dtype)


def _norm_residual_last_kernel(x_ref, y_ref, gp_ref, xo_ref):
    y = y_ref[...].astype(F32)
    xo_ref[...] = x_ref[...] + y * lax.rsqrt(jnp.mean(y * y, axis=-1, keepdims=True) + NORM_EPS) * gp_ref[...]


def norm_residual(x, y, g_post, g_next=None, rows=256):
    n, d = x.shape
    row = pl.BlockSpec((rows, d), lambda i: (i, 0))
    vec = pl.BlockSpec((1, d), lambda i: (0, 0))
    if g_next is None:
        return pl.pallas_call(
            _norm_residual_last_kernel,
            grid=(n // rows,),
            in_specs=[row, row, vec],
            out_specs=row,
            out_shape=jax.ShapeDtypeStruct((n, d), F32),
            compiler_params=_cparams(("parallel",)),
            name="norm_residual_last",
        )(x, y, g_post.reshape(1, d)), None
    return pl.pallas_call(
        _norm_residual_kernel,
        grid=(n // rows,),
        in_specs=[row, row, vec, vec],
        out_specs=[row, row],
        out_shape=[jax.ShapeDtypeStruct((n, d), F32), jax.ShapeDtypeStruct((n, d), BF16)],
        compiler_params=_cparams(("parallel",)),
        name="norm_residual",
    )(x, y, g_post.reshape(1, d), g_next.reshape(1, d))


GELU_C = math.sqrt(2.0 / math.pi)


def _gelu_tanh(x):
    z = x * (x * x * (-2.0 * GELU_C * 0.044715 * LOG2E) + (-2.0 * GELU_C * LOG2E))
    return x / (1.0 + jnp.exp2(z))


def _ffn_up_kernel(x_ref, wg_ref, wv_ref, cwg_ref, cwv_ref, cbg_ref, cbv_ref, o_ref,
                   wb_ref, u_ref, *, tm, tn, seq):
    i = pl.program_id(1)
    starts_seq = (i * tm) % seq == 0

    @pl.when(i == 0)
    def _():
        wb_ref[:, :tn] = wg_ref[...].astype(BF16)
        wb_ref[:, tn:] = wv_ref[...].astype(BF16)

    @pl.when(starts_seq)
    def _():
        u_ref[0:8, :] = jnp.zeros((8, 2 * tn), F32)

    @pl.when(jnp.logical_not(starts_seq))
    def _():
        u_ref[0:8, :] = u_ref[tm:tm + 8, :]

    u_ref[8:8 + tm, :] = _dot(x_ref[...], wb_ref[...])

    def conv(lo, w_ref, b_ref):
        cols = slice(lo, lo + tn)
        return (b_ref[...] + w_ref[0:1, :] * u_ref[6:6 + tm, cols]
                + w_ref[1:2, :] * u_ref[7:7 + tm, cols] + w_ref[2:3, :] * u_ref[8:8 + tm, cols])

    gate = conv(0, cwg_ref, cbg_ref)
    val = conv(tn, cwv_ref, cbv_ref)
    o_ref[...] = (_gelu_tanh(gate) * val).astype(o_ref.dtype)


def ffn_up_conv_geglu(h, w_up, conv_w, conv_b, layer, seq, tm, tn):
    m, kdim = h.shape
    dff = w_up.shape[2] // 2
    assert m % tm == 0 and dff % tn == 0 and seq % tm == 0
    nc = dff // tn
    wspec = lambda off: pl.BlockSpec((None, kdim, tn), lambda j, i: (layer, 0, off + j))
    cspec = lambda r, off: pl.BlockSpec((None, r, tn), lambda j, i: (layer, 0, off + j))
    return pl.pallas_call(
        functools.partial(_ffn_up_kernel, tm=tm, tn=tn, seq=seq),
        grid=(nc, m // tm),
        in_specs=[pl.BlockSpec((tm, kdim), lambda j, i: (i, 0)),
                  wspec(0), wspec(nc),
                  cspec(CONV_WIDTH, 0), cspec(CONV_WIDTH, nc),
                  cspec(1, 0), cspec(1, nc)],
        out_specs=pl.BlockSpec((tm, tn), lambda j, i: (i, j)),
        out_shape=jax.ShapeDtypeStruct((m, dff), BF16),
        scratch_shapes=[pltpu.VMEM((kdim, 2 * tn), BF16), pltpu.VMEM((tm + 8, 2 * tn), F32)],
        compiler_params=_cparams(("parallel", "arbitrary")),
        name="ffn_up",
    )(h, w_up, w_up, conv_w, conv_w, conv_b[:, None, :], conv_b[:, None, :])


def _in_layout(d, aw, iw, ih, hk, hv, kvl):
    segs = [("q_a", aw), ("c_kv", kvl), ("q_i", iw), ("k_i", HEAD_DIM), ("w_i", ih), ("pad", None),
            ("q_h", hk), ("f_h", hk), ("i_h", hv), ("g_h", hv), ("gate_a", d), ("gate_b", d)]
    lay, off = {}, 0
    for name, w in segs:
        if name == "pad":
            w = (-off) % IN_PAD_ALIGN
            lay["pad_width"] = w
        lay[name] = off
        off += w
    assert off % IN_PROJ_TN == 0
    lay["total"] = off
    return lay


def _pad_cast_kernel(lo_ref, hi_ref, o_ref, *, n_pad, n_tail, off, keep):
    n = pl.program_id(0)
    tn = o_ref.shape[1]

    @pl.when(n < n_pad)
    def _():
        o_ref[...] = lo_ref[...].T.astype(BF16)

    @pl.when(n == n_pad)
    def _():
        row = lax.broadcasted_iota(jnp.int32, (tn, 1), 0)
        o_ref[...] = jnp.where(row < keep, lo_ref[...], 0.0).T.astype(BF16)

    @pl.when((n > n_pad) & (n < n_tail))
    def _():
        o_ref[...] = jnp.zeros_like(o_ref)

    @pl.when(n >= n_tail)
    def _():
        o_ref[:, :tn - off] = lo_ref[off:, :].T.astype(BF16)
        o_ref[:, tn - off:] = hi_ref[:off, :].T.astype(BF16)


def pad_cast_w_in(w_in_t, layer, lay, ih):
    n_src, kdim = w_in_t.shape[1], w_in_t.shape[2]
    tn = IN_PROJ_TN
    assert lay["k_i"] % tn == 0 and lay["q_h"] % tn == 0 and ih % 8 == 0
    n_pad = lay["k_i"] // tn
    n_tail = lay["q_h"] // tn
    back = -(-lay["pad_width"] // tn)
    off = back * tn - lay["pad_width"]
    assert 0 < off < tn and off % 8 == 0
    last = (n_src - 1) // tn

    def src(extra):
        def index(n):
            blk = jnp.where(n <= n_pad, n, n - back + extra)
            return (layer, jnp.clip(blk, 0, last), 0)
        return pl.BlockSpec((None, tn, kdim), index)

    return pl.pallas_call(
        functools.partial(_pad_cast_kernel, n_pad=n_pad, n_tail=n_tail, off=off, keep=HEAD_DIM + ih),
        grid=(lay["total"] // tn,),
        in_specs=[src(0), src(1)],
        out_specs=pl.BlockSpec((kdim, tn), lambda n: (0, n)),
        out_shape=jax.ShapeDtypeStruct((kdim, lay["total"]), BF16),
        compiler_params=_cparams(("parallel",)),
        name="pad_cast_w_in",
    )(w_in_t, w_in_t)


def _pick(n, cands):
    for c in cands:
        if n % c == 0:
            return c
    raise ValueError(f"no tile for {n}")


def kernel(x, rel_bias, hgrn_lb_logits, mix_pre_g, mix_post_g, w_in, kv_norm_g, w_uk, w_uv, idx_k_ln_g, idx_k_ln_b, hgrn_norm_g, w_proj_attn, w_proj_hgrn, w_out, ffn_pre_g, ffn_post_g, w_up, conv_w, conv_b, w_down):
    batch, seq, d = x.shape
    depth = w_in.shape[0]
    aw = w_proj_attn.shape[1]
    hv = w_proj_hgrn.shape[1]
    hk = hgrn_lb_logits.shape[1]
    kvl = w_uk.shape[1]
    dff = w_down.shape[1]
    heads = aw // HEAD_DIM
    hheads = hv // HEAD_DIM
    ih = w_in.shape[2] - (aw + kvl + HEAD_DIM + 2 * hk + 2 * hv + 2 * d)
    ih = ih // (HEAD_DIM + 1)
    iw = ih * HEAD_DIM
    lay = _in_layout(d, aw, iw, ih, hk, hv, kvl)
    n = batch * seq

    lb_p = jax.nn.softmax(hgrn_lb_logits.astype(F32), axis=0)
    lb_all = jnp.cumsum(lb_p, axis=0) - lb_p[0:1]
    bias_tiles = rel_bias_tiles(rel_bias, heads)

    tm = _pick(n, (1024, 512, 256))
    w_in_t = jnp.swapaxes(w_in, 1, 2)
    xf = x.reshape(n, d)
    h = rmsnorm_cast(xf, mix_pre_g[0])
    for l in range(depth):
        w_in_l = pad_cast_w_in(w_in_t, l, lay, ih)
        proj = matmul(h, w_in_l, tm, IN_PROJ_TN, name="in_proj")
        k, v, kidx, wi = dsa_prep(proj, lay, kv_norm_g[l], w_uk[l], w_uv[l],
                                  idx_k_ln_g[l], idx_k_ln_b[l], ih)
        attn = dsa_attention(proj, lay, wi, kidx, k, v, bias_tiles, batch, seq, heads, ih)
        hg = hgrn2(proj, lay, lb_all[l], hgrn_norm_g[l], batch, seq, hheads)
        merged = gated_merge(attn, hg, w_proj_attn, w_proj_hgrn, l, proj, lay, tm=tm)
        y = matmul_wstat(merged, w_out, l, tm, 512, name="out_proj")
        xf, h = norm_residual(xf, y, mix_post_g[l], ffn_pre_g[l])
        act = ffn_up_conv_geglu(h, w_up, conv_w, conv_b, l, seq, tm, _pick(dff, (256, 128)))
        y = matmul(act, w_down[l].astype(BF16), tm, 512, tk=_pick(dff, (dff // 2, dff)),
                   name="ffn_down")
        xf, h = norm_residual(xf, y, ffn_post_g[l], mix_pre_g[l + 1] if l + 1 < depth else None)
    return xf.reshape(batch, seq, d)
```

```python
import functools
import math

import jax
import jax.numpy as jnp
from jax import lax
from jax.experimental import pallas as pl
from jax.experimental.pallas import tpu as pltpu

F32 = jnp.float32
BF16 = jnp.bfloat16

NORM_EPS = 1e-6
HEAD_DIM = 128
IDX_TOPK = 256
REL_BUCKETS = 32
REL_MAX_DIST = 128
CONV_WIDTH = 3
LANES = 128
VMEM_LIMIT_BYTES = 56 * 1024 * 1024

ATT_BLOCK = 256
QI_BLOCK = 512
HGRN_CHUNK = 64
HGRN_SUB = 16
HGRN_SAFE_LOG2 = 100.0
IN_PROJ_TN = 512
IN_PAD_ALIGN = 1024
INT_MIN = -2 ** 31
MASK_NEG = -1e30
M_INIT = -1e20
LOG2E = math.log2(math.e)


def _cparams(sem):
    return pltpu.CompilerParams(dimension_semantics=sem, vmem_limit_bytes=VMEM_LIMIT_BYTES)


def _dot(a, b):
    return jnp.dot(a, b, preferred_element_type=F32)


def _dot_nt(a, b):
    return lax.dot_general(a, b, (((1,), (1,)), ((), ())), preferred_element_type=F32)


def _rmsnorm_kernel(x_ref, g_ref, o_ref):
    x = x_ref[...]
    r = lax.rsqrt(jnp.mean(x * x, axis=-1, keepdims=True) + NORM_EPS)
    o_ref[...] = (x * r * g_ref[...]).astype(o_ref.dtype)


def rmsnorm_cast(x, g, rows=256):
    n, d = x.shape
    return pl.pallas_call(
        _rmsnorm_kernel,
        grid=(n // rows,),
        in_specs=[pl.BlockSpec((rows, d), lambda i: (i, 0)),
                  pl.BlockSpec((1, d), lambda i: (0, 0))],
        out_specs=pl.BlockSpec((rows, d), lambda i: (i, 0)),
        out_shape=jax.ShapeDtypeStruct((n, d), BF16),
        compiler_params=_cparams(("parallel",)),
        name="rmsnorm_cast",
    )(x, g.reshape(1, d))


def _mm_kernel(x_ref, w_ref, o_ref):
    o_ref[...] = _dot(x_ref[...], w_ref[...]).astype(o_ref.dtype)


def _mm_acc_kernel(x_ref, w_ref, o_ref, acc_ref, *, nk):
    k = pl.program_id(2)

    @pl.when(k == 0)
    def _():
        acc_ref[...] = jnp.zeros_like(acc_ref)

    acc_ref[...] += _dot(x_ref[...], w_ref[...])

    @pl.when(k == nk - 1)
    def _():
        o_ref[...] = acc_ref[...].astype(o_ref.dtype)


def matmul(x, w, tm, tn, tk=None, out_dtype=BF16, name="matmul"):
    m, kdim = x.shape
    n = w.shape[1]
    tk = kdim if tk is None else tk
    nk = kdim // tk
    assert m % tm == 0 and n % tn == 0 and kdim % tk == 0
    if nk == 1:
        return pl.pallas_call(
            _mm_kernel,
            grid=(m // tm, n // tn),
            in_specs=[pl.BlockSpec((tm, kdim), lambda i, j: (i, 0)),
                      pl.BlockSpec((kdim, tn), lambda i, j: (0, j))],
            out_specs=pl.BlockSpec((tm, tn), lambda i, j: (i, j)),
            out_shape=jax.ShapeDtypeStruct((m, n), out_dtype),
            compiler_params=_cparams(("parallel", "parallel")),
            name=name,
        )(x, w)
    return pl.pallas_call(
        functools.partial(_mm_acc_kernel, nk=nk),
        grid=(m // tm, n // tn, nk),
        in_specs=[pl.BlockSpec((tm, tk), lambda i, j, k: (i, k)),
                  pl.BlockSpec((tk, tn), lambda i, j, k: (k, j))],
        out_specs=pl.BlockSpec((tm, tn), lambda i, j, k: (i, j)),
        out_shape=jax.ShapeDtypeStruct((m, n), out_dtype),
        scratch_shapes=[pltpu.VMEM((tm, tn), F32)],
        compiler_params=_cparams(("parallel", "parallel", "arbitrary")),
        name=name,
    )(x, w)


def _mm_nt_kernel(x_ref, wt_ref, o_ref):
    o_ref[...] = _dot_nt(x_ref[...], wt_ref[...]).astype(o_ref.dtype)


def matmul_nt(x, wt, tm, tn, out_dtype=BF16, name="matmul_nt"):
    m, kdim = x.shape
    n = wt.shape[0]
    assert m % tm == 0 and n % tn == 0
    return pl.pallas_call(
        _mm_nt_kernel,
        grid=(m // tm, n // tn),
        in_specs=[pl.BlockSpec((tm, kdim), lambda i, j: (i, 0)),
                  pl.BlockSpec((tn, kdim), lambda i, j: (j, 0))],
        out_specs=pl.BlockSpec((tm, tn), lambda i, j: (i, j)),
        out_shape=jax.ShapeDtypeStruct((m, n), out_dtype),
        compiler_params=_cparams(("parallel", "parallel")),
        name=name,
    )(x, wt)


def _mm_wstat_kernel(x_ref, w_ref, o_ref, wb_ref):
    @pl.when(pl.program_id(1) == 0)
    def _():
        wb_ref[...] = w_ref[...].astype(BF16)

    o_ref[...] = _dot(x_ref[...], wb_ref[...]).astype(o_ref.dtype)


def matmul_wstat(x, w, layer, tm, tn, out_dtype=BF16, name="matmul_wstat"):
    m, kdim = x.shape
    n = w.shape[2]
    assert m % tm == 0 and n % tn == 0
    return pl.pallas_call(
        _mm_wstat_kernel,
        grid=(n // tn, m // tm),
        in_specs=[pl.BlockSpec((tm, kdim), lambda j, i: (i, 0)),
                  pl.BlockSpec((None, kdim, tn), lambda j, i: (layer, 0, j))],
        out_specs=pl.BlockSpec((tm, tn), lambda j, i: (i, j)),
        out_shape=jax.ShapeDtypeStruct((m, n), out_dtype),
        scratch_shapes=[pltpu.VMEM((kdim, tn), BF16)],
        compiler_params=_cparams(("parallel", "arbitrary")),
        name=name,
    )(x, w)


def _dsa_prep_kernel(ckv_ref, kw_ref, kvg_ref, wuk_ref, wuv_ref, lng_ref, lnb_ref,
                     k_ref, v_ref, kidx_ref, wi_ref, *, k_scale, w_scale):
    c = ckv_ref[...].astype(F32)
    c = c * lax.rsqrt(jnp.mean(c * c, axis=-1, keepdims=True) + NORM_EPS) * kvg_ref[...]
    cb = c.astype(BF16)
    k_ref[...] = (_dot(cb, wuk_ref[...]) * k_scale).astype(k_ref.dtype)
    v_ref[:, :HEAD_DIM] = _dot(cb, wuv_ref[...]).astype(v_ref.dtype)
    v_ref[:, HEAD_DIM:] = jnp.ones((v_ref.shape[0], HEAD_DIM), v_ref.dtype)
    kw = kw_ref[...].astype(F32)
    ki = kw[:, :HEAD_DIM]
    mu = jnp.mean(ki, axis=-1, keepdims=True)
    kc = ki - mu
    kn = kc * lax.rsqrt(jnp.mean(kc * kc, axis=-1, keepdims=True) + NORM_EPS)
    kidx_ref[...] = (kn * lng_ref[...] + lnb_ref[...]).astype(kidx_ref.dtype)
    wi_ref[...] = kw[:, HEAD_DIM:] * w_scale


def dsa_prep(proj, lay, kv_norm_g, w_uk, w_uv, ln_g, ln_b, idx_heads, rows=512):
    n = proj.shape[0]
    kvl = w_uk.shape[0]
    row = lambda i: (i, 0)
    const = lambda i: (0, 0)
    out = jax.ShapeDtypeStruct((n, HEAD_DIM), BF16)
    return pl.pallas_call(
        functools.partial(_dsa_prep_kernel, k_scale=HEAD_DIM ** -0.5 * LOG2E,
                          w_scale=idx_heads ** -0.5 * HEAD_DIM ** -0.5),
        grid=(n // rows,),
        in_specs=[pl.BlockSpec((rows, kvl), lambda i: (i, lay["c_kv"] // kvl)),
                  pl.BlockSpec((rows, 2 * HEAD_DIM), lambda i: (i, lay["k_i"] // (2 * HEAD_DIM))),
                  pl.BlockSpec((1, kvl), const),
                  pl.BlockSpec((kvl, HEAD_DIM), const),
                  pl.BlockSpec((kvl, HEAD_DIM), const),
                  pl.BlockSpec((1, HEAD_DIM), const),
                  pl.BlockSpec((1, HEAD_DIM), const)],
        out_specs=[pl.BlockSpec((rows, HEAD_DIM), row), pl.BlockSpec((rows, 2 * HEAD_DIM), row),
                   pl.BlockSpec((rows, HEAD_DIM), row), pl.BlockSpec((rows, HEAD_DIM), row)],
        out_shape=[out, jax.ShapeDtypeStruct((n, 2 * HEAD_DIM), BF16), out,
                   jax.ShapeDtypeStruct((n, HEAD_DIM), F32)],
        compiler_params=_cparams(("parallel",)),
        name="dsa_prep",
    )(proj, proj, kv_norm_g.reshape(1, kvl), w_uk.astype(BF16), w_uv.astype(BF16),
      ln_g.reshape(1, HEAD_DIM), ln_b.reshape(1, HEAD_DIM))


def _sortable_key(x):
    u = lax.bitcast_convert_type(x, jnp.int32)
    return u ^ ((u >> 31) & jnp.int32(0x7FFFFFFF))


def _dsa_kernel(*refs, heads, idx_heads, topk, n_qi):
    qa_ref = refs[0]
    qi_refs = refs[1:1 + n_qi]
    (wi_ref, kidx_ref, k_ref, v_ref, bias_ref, o_ref,
     qas_ref, qis_ref, keys_ref, mask_ref, m_ref, acc_ref) = refs[1 + n_qi:]
    qb = ATT_BLOCK
    ck = ATT_BLOCK
    j = pl.program_id(1)

    for h in range(heads):
        qas_ref[h * qb:(h + 1) * qb, :] = qa_ref[:, h * HEAD_DIM:(h + 1) * HEAD_DIM]
    for h in range(idx_heads):
        src, off = divmod(h * HEAD_DIM, QI_BLOCK)
        qis_ref[h * qb:(h + 1) * qb, :] = qi_refs[src][:, off:off + HEAD_DIM]

    row_pos = j * qb + lax.broadcasted_iota(jnp.int32, (qb, ck), 0)
    lane = lax.broadcasted_iota(jnp.int32, (qb, ck), 1)
    hg = 8 if idx_heads % 8 == 0 else idx_heads

    def score_chunk(c, carry):
        kc = kidx_ref[pl.ds(pl.multiple_of(c * ck, ck), ck), :]
        acc = jnp.zeros((qb, ck), F32)
        for g in range(idx_heads // hg):
            s = _dot_nt(qis_ref[g * hg * qb:(g + 1) * hg * qb, :], kc)
            for hh in range(hg):
                h = g * hg + hh
                acc = acc + jnp.maximum(s[hh * qb:(hh + 1) * qb, :], 0.0) * wi_ref[:, h:h + 1]
        valid = (c * ck + lane) <= row_pos
        keys_ref[c] = jnp.where(valid, _sortable_key(acc), jnp.int32(INT_MIN))
        return carry

    lax.fori_loop(0, j + 1, score_chunk, 0)

    def bit_step(i, lo):
        cand = lo + lax.shift_left(jnp.int32(1), jnp.int32(31) - i)

        def count_chunk(c, part):
            ge = (keys_ref[c] >= cand).astype(jnp.int32)
            for t in range(ck // LANES):
                part = part + ge[:, t * LANES:(t + 1) * LANES]
            return part

        part = lax.fori_loop(0, j + 1, count_chunk, jnp.zeros((qb, LANES), jnp.int32))
        cnt = jnp.sum(part, axis=-1, keepdims=True)
        return jnp.where(cnt >= topk, cand, lo)

    lo = lax.fori_loop(0, 32, bit_step, jnp.full((qb, 1), INT_MIN, jnp.int32))
    thr = jnp.maximum(lo, jnp.int32(INT_MIN + 1))

    m_ref[...] = jnp.full_like(m_ref, M_INIT)
    acc_ref[...] = jnp.zeros_like(acc_ref)

    def attend(c, bias_idx):
        off = pl.multiple_of(c * ck, ck)
        kc = k_ref[pl.ds(off, ck), :]
        vc = v_ref[pl.ds(off, ck), :]
        mask_ref[...] = jnp.where(keys_ref[c] >= thr, 0.0, MASK_NEG)
        for h in range(heads):
            rows = slice(h * qb, (h + 1) * qb)
            s = _dot_nt(qas_ref[rows, :], kc) + mask_ref[...]
            if bias_idx is not None:
                s = s + bias_ref[bias_idx, rows, :]
            parts = [s[:, t * LANES:(t + 1) * LANES] for t in range(ck // LANES)]
            smax = functools.reduce(jnp.maximum, parts)
            m_prev = m_ref[rows, :]
            m_new = jnp.maximum(m_prev, jnp.max(smax, axis=-1, keepdims=True))
            p = jnp.concatenate([jnp.exp2(x - m_new) for x in parts], axis=1).astype(BF16)
            alpha = jnp.exp2(m_prev - m_new)
            pv = _dot(p, vc)
            acc_ref[rows, :HEAD_DIM] = alpha * acc_ref[rows, :HEAD_DIM] + pv[:, :HEAD_DIM]
            acc_ref[rows, HEAD_DIM:] = alpha * acc_ref[rows, HEAD_DIM:] + pv[:, HEAD_DIM:]
            m_ref[rows, :] = m_new

    def far_chunk(c, carry):
        attend(c, None)
        return carry

    lax.fori_loop(0, j - 1, far_chunk, 0)

    @pl.when(j >= 1)
    def _():
        attend(j - 1, 1)

    attend(j, 0)

    for h in range(heads):
        rows = slice(h * qb, (h + 1) * qb)
        o = acc_ref[rows, :HEAD_DIM] / acc_ref[rows, HEAD_DIM:]
        o_ref[:, h * HEAD_DIM:(h + 1) * HEAD_DIM] = o.astype(o_ref.dtype)


def dsa_attention(proj, lay, wi, kidx, k, v, bias_tiles, batch, seq, heads, idx_heads):
    qb = ATT_BLOCK
    nqb = seq // qb
    aw = heads * HEAD_DIM
    iw = idx_heads * HEAD_DIM
    topk = min(IDX_TOPK, seq // 4)
    assert lay["q_i"] % QI_BLOCK == 0 and iw % QI_BLOCK == 0 and lay["q_a"] % aw == 0
    n_qi = iw // QI_BLOCK
    return pl.pallas_call(
        functools.partial(_dsa_kernel, heads=heads, idx_heads=idx_heads, topk=topk, n_qi=n_qi),
        grid=(batch, nqb),
        in_specs=[pl.BlockSpec((qb, aw), lambda b, j: (b * nqb + j, lay["q_a"] // aw)),
                  *[pl.BlockSpec((qb, QI_BLOCK), lambda b, j, t=t: (b * nqb + j, lay["q_i"] // QI_BLOCK + t))
                    for t in range(n_qi)],
                  pl.BlockSpec((qb, HEAD_DIM), lambda b, j: (b * nqb + j, 0)),
                  pl.BlockSpec((seq, HEAD_DIM), lambda b, j: (b, 0)),
                  pl.BlockSpec((seq, HEAD_DIM), lambda b, j: (b, 0)),
                  pl.BlockSpec((seq, 2 * HEAD_DIM), lambda b, j: (b, 0)),
                  pl.BlockSpec((2, heads * qb, qb), lambda b, j: (0, 0, 0),
                               pipeline_mode=pl.Buffered(1))],
        out_specs=pl.BlockSpec((qb, aw), lambda b, j: (b * nqb + j, 0)),
        out_shape=jax.ShapeDtypeStruct((batch * seq, aw), BF16),
        scratch_shapes=[pltpu.VMEM((heads * qb, HEAD_DIM), BF16),
                        pltpu.VMEM((idx_heads * qb, HEAD_DIM), BF16),
                        pltpu.VMEM((seq // qb, qb, qb), jnp.int32),
                        pltpu.VMEM((qb, qb), F32),
                        pltpu.VMEM((heads * qb, LANES), F32),
                        pltpu.VMEM((heads * qb, 2 * HEAD_DIM), F32)],
        compiler_params=_cparams(("parallel", "arbitrary")),
        name="dsa_attention",
    )(proj, *([proj] * n_qi), wi, kidx, k, v, bias_tiles)


def _t5_bucket(dist):
    max_exact = REL_BUCKETS // 2
    d = jnp.maximum(dist, 0)
    df = jnp.maximum(d, 1).astype(F32)
    large = max_exact + (jnp.log(df / max_exact) / math.log(REL_MAX_DIST / max_exact)
                         * (REL_BUCKETS - max_exact)).astype(jnp.int32)
    large = jnp.minimum(large, REL_BUCKETS - 1)
    return jnp.where(d < max_exact, d, large)


def rel_bias_tiles(rel_bias, heads):
    qb = ATT_BLOCK
    assert _static_far_bucket(qb) == REL_BUCKETS - 1
    tbl = (rel_bias[_t5_bucket(jnp.arange(4 * qb))] - rel_bias[REL_BUCKETS - 1][None, :]).T * LOG2E
    y = jnp.arange(2 * qb)
    tiles = []
    for off in (0, qb):
        d = off + jnp.where(y < qb, -y, 2 * qb - y)
        u = tbl[:, jnp.maximum(d, 0)]
        flat = jnp.broadcast_to(u[:, None, :], (heads, qb, 2 * qb)).reshape(heads, 2 * qb * qb)
        skew = flat[:, :qb * (2 * qb - 1)].reshape(heads, qb, 2 * qb - 1)
        tiles.append(skew[:, :, :qb].reshape(heads * qb, qb))
    return jnp.stack(tiles).astype(F32)


def _static_far_bucket(d):
    max_exact = REL_BUCKETS // 2
    v = max_exact + int(math.log(d / max_exact) / math.log(REL_MAX_DIST / max_exact)
                        * (REL_BUCKETS - max_exact))
    return min(v, REL_BUCKETS - 1)


def _hgrn_kernel(q_ref, f_ref, i_ref, g_ref, lb_ref, ng_ref, o_ref,
                 st_ref, a_ref, k_ref, qf_ref, p_ref, *, heads_per_step, tb):
    c = HGRN_CHUNK
    sub = HGRN_SUB
    nsub = c // sub
    hs = range(heads_per_step)

    @pl.when(pl.program_id(2) == 0)
    def _():
        st_ref[...] = jnp.zeros_like(st_ref)

    r_io = lax.broadcasted_iota(jnp.int32, (c, c), 0)
    c_io = lax.broadcasted_iota(jnp.int32, (c, c), 1)
    tri = (c_io <= r_io).astype(F32)
    sub_r = lax.broadcasted_iota(jnp.int32, (sub, c), 0)
    sub_c = lax.broadcasted_iota(jnp.int32, (sub, c), 1)
    ng = ng_ref[...]

    def a_start(hh, sb):
        return a_ref[hh, sb * sub - 1:sb * sub, :] if sb > 0 else jnp.zeros((1, HEAD_DIM), F32)

    def chunk(ci, carry):
        r0 = pl.multiple_of(ci * c, c)

        worst = jnp.zeros((1, HEAD_DIM), F32)
        for hh in hs:
            cols = slice(hh * HEAD_DIM, (hh + 1) * HEAD_DIM)
            lb = lb_ref[:, cols]
            z = f_ref[pl.ds(r0, c), cols].astype(F32)
            qx = q_ref[pl.ds(r0, c), cols].astype(F32)
            sig = jax.nn.sigmoid(z)
            lf = jnp.log2(lb + (1.0 - lb) * sig)
            k_ref[hh] = (1.0 - lb) * jax.nn.sigmoid(-z)
            qf_ref[hh] = qx * jax.nn.sigmoid(qx)
            a = jnp.dot(tri, lf, preferred_element_type=F32, precision=lax.Precision.HIGHEST)
            a_ref[hh] = a
            for sb in range(nsub):
                top = a[sb * sub - 1:sb * sub, :] if sb > 0 else jnp.zeros((1, HEAD_DIM), F32)
                worst = jnp.maximum(worst, top - a[(sb + 1) * sub - 1:(sb + 1) * sub, :])
        safe = jnp.max(worst) <= HGRN_SAFE_LOG2

        @pl.when(safe)
        def _():
            for hh in hs:
                a = a_ref[hh]
                kf = k_ref[hh]
                for sb in range(nsub):
                    rs = slice(sb * sub, (sb + 1) * sub)
                    top = a_start(hh, sb)
                    q_t = qf_ref[hh, rs, :] * jnp.exp2(a_ref[hh, rs, :] - top)
                    k_t = kf * jnp.exp2(top - a)
                    p = _dot_nt(q_t.astype(BF16), k_t.astype(BF16))
                    p_ref[hh, rs, :] = jnp.where(sub_c <= sub_r + sb * sub, p, 0.0)

        @pl.when(jnp.logical_not(safe))
        def _():
            for hh in hs:
                a = a_ref[hh]
                kf = k_ref[hh]
                for sb in range(nsub):
                    rs = slice(sb * sub, (sb + 1) * sub)
                    top = a_start(hh, sb)
                    a_blk = a_ref[hh, rs, :]
                    q_blk = qf_ref[hh, rs, :]
                    q_t = q_blk * jnp.exp2(a_blk - top)
                    k_t = kf * jnp.exp2(top - a)
                    p_off = _dot_nt(q_t.astype(BF16), k_t.astype(BF16))
                    p_diag = jnp.zeros((sub, c), F32)
                    for s in range(sub):
                        row = sb * sub + s
                        e = jnp.exp2(a_blk - a_ref[hh, row:row + 1, :])
                        col = jnp.sum(q_blk * e * k_ref[hh, row:row + 1, :], axis=-1, keepdims=True)
                        p_diag = jnp.where((sub_c == row) & (sub_r >= s), col, p_diag)
                    p_ref[hh, rs, :] = jnp.where(sub_c < sb * sub, p_off, p_diag)

        for hh in hs:
            cols = slice(hh * HEAD_DIM, (hh + 1) * HEAD_DIM)
            a = a_ref[hh]
            kf = k_ref[hh]
            v = i_ref[pl.ds(r0, c), cols].astype(F32)
            gx = g_ref[pl.ds(r0, c), cols].astype(F32)
            vb = v.astype(BF16)
            st = st_ref[hh]
            o = (_dot(p_ref[hh].astype(BF16), vb)
                 + _dot_nt((qf_ref[hh] * jnp.exp2(a)).astype(BF16), st.astype(BF16)))
            a_last = a[c - 1:c, :]
            k_end = kf * jnp.exp2(a_last - a)
            st_ref[hh] = st * jnp.exp2(a_last) + _dot(v.T.astype(BF16), k_end.astype(BF16))
            o = o * lax.rsqrt(jnp.mean(o * o, axis=-1, keepdims=True) + NORM_EPS) * ng
            o_ref[pl.ds(r0, c), cols] = (o * (gx * jax.nn.sigmoid(gx))).astype(o_ref.dtype)
        return carry

    lax.fori_loop(0, tb // c, chunk, 0)


def hgrn2(proj, lay, lb, norm_g, batch, seq, heads, heads_per_step=8, tb=512):
    heads_per_step = min(heads_per_step, heads)
    assert heads % heads_per_step == 0
    hw = heads_per_step * HEAD_DIM
    nt = seq // tb
    ng = heads // heads_per_step

    def col(name):
        base = lay[name] // hw
        return lambda b, h, t: (b * nt + t, base + h)

    return pl.pallas_call(
        functools.partial(_hgrn_kernel, heads_per_step=heads_per_step, tb=tb),
        grid=(batch, ng, nt),
        in_specs=[pl.BlockSpec((tb, hw), col("q_h")),
                  pl.BlockSpec((tb, hw), col("f_h")),
                  pl.BlockSpec((tb, hw), col("i_h")),
                  pl.BlockSpec((tb, hw), col("g_h")),
                  pl.BlockSpec((1, hw), lambda b, h, t: (0, h)),
                  pl.BlockSpec((1, HEAD_DIM), lambda b, h, t: (0, 0))],
        out_specs=pl.BlockSpec((tb, hw), lambda b, h, t: (b * nt + t, h)),
        out_shape=jax.ShapeDtypeStruct((batch * seq, heads * HEAD_DIM), BF16),
        scratch_shapes=[pltpu.VMEM((heads_per_step, HEAD_DIM, HEAD_DIM), F32),
                        pltpu.VMEM((heads_per_step, HGRN_CHUNK, HEAD_DIM), F32),
                        pltpu.VMEM((heads_per_step, HGRN_CHUNK, HEAD_DIM), F32),
                        pltpu.VMEM((heads_per_step, HGRN_CHUNK, HEAD_DIM), F32),
                        pltpu.VMEM((heads_per_step, HGRN_CHUNK, HGRN_CHUNK), F32)],
        compiler_params=_cparams(("parallel", "parallel", "arbitrary")),
        name="hgrn2",
    )(proj, proj, proj, proj, lb.reshape(1, -1), norm_g.reshape(1, HEAD_DIM))


def _merge_kernel(a_ref, b_ref, wa_ref, wb_ref, ga_ref, gb_ref, o_ref, wab_ref, wbb_ref):
    @pl.when(pl.program_id(1) == 0)
    def _():
        wab_ref[...] = wa_ref[...].astype(BF16)
        wbb_ref[...] = wb_ref[...].astype(BF16)

    ya = _dot(a_ref[...], wab_ref[...])
    yb = _dot(b_ref[...], wbb_ref[...])
    ga = jax.nn.sigmoid(ga_ref[...].astype(F32))
    gb = jax.nn.sigmoid(gb_ref[...].astype(F32))
    o_ref[...] = (ga * ya + gb * yb).astype(o_ref.dtype)


def gated_merge(attn, hg, w_pa, w_pb, layer, proj, lay, tm=1024, tn=512):
    m, ka = attn.shape
    kb = hg.shape[1]
    d = w_pa.shape[2]
    assert lay["gate_a"] % tn == 0 and lay["gate_b"] % tn == 0
    ga0, gb0 = lay["gate_a"] // tn, lay["gate_b"] // tn
    return pl.pallas_call(
        _merge_kernel,
        grid=(d // tn, m // tm),
        in_specs=[pl.BlockSpec((tm, ka), lambda j, i: (i, 0)),
                  pl.BlockSpec((tm, kb), lambda j, i: (i, 0)),
                  pl.BlockSpec((None, ka, tn), lambda j, i: (layer, 0, j)),
                  pl.BlockSpec((None, kb, tn), lambda j, i: (layer, 0, j)),
                  pl.BlockSpec((tm, tn), lambda j, i: (i, ga0 + j)),
                  pl.BlockSpec((tm, tn), lambda j, i: (i, gb0 + j))],
        out_specs=pl.BlockSpec((tm, tn), lambda j, i: (i, j)),
        out_shape=jax.ShapeDtypeStruct((m, d), BF16),
        scratch_shapes=[pltpu.VMEM((ka, tn), BF16), pltpu.VMEM((kb, tn), BF16)],
        compiler_params=_cparams(("parallel", "arbitrary")),
        name="gated_merge",
    )(attn, hg, w_pa, w_pb, proj, proj)


def _norm_residual_kernel(x_ref, y_ref, gp_ref, gn_ref, xo_ref, ho_ref):
    y = y_ref[...].astype(F32)
    yn = y * lax.rsqrt(jnp.mean(y * y, axis=-1, keepdims=True) + NORM_EPS) * gp_ref[...]
    x = x_ref[...] + yn
    xo_ref[...] = x
    ho_ref[...] = (x * lax.rsqrt(jnp.mean(x * x, axis=-1, keepdims=True) + NORM_EPS)
                   * gn_ref[...]).astype(ho_ref.dtype)


def _norm_residual_last_kernel(x_ref, y_ref, gp_ref, xo_ref):
    y = y_ref[...].astype(F32)
    xo_ref[...] = x_ref[...] + y * lax.rsqrt(jnp.mean(y * y, axis=-1, keepdims=True) + NORM_EPS) * gp_ref[...]


def norm_residual(x, y, g_post, g_next=None, rows=256):
    n, d = x.shape
    row = pl.BlockSpec((rows, d), lambda i: (i, 0))
    vec = pl.BlockSpec((1, d), lambda i: (0, 0))
    if g_next is None:
        return pl.pallas_call(
            _norm_residual_last_kernel,
            grid=(n // rows,),
            in_specs=[row, row, vec],
            out_specs=row,
            out_shape=jax.ShapeDtypeStruct((n, d), F32),
            compiler_params=_cparams(("parallel",)),
            name="norm_residual_last",
        )(x, y, g_post.reshape(1, d)), None
    return pl.pallas_call(
        _norm_residual_kernel,
        grid=(n // rows,),
        in_specs=[row, row, vec, vec],
        out_specs=[row, row],
        out_shape=[jax.ShapeDtypeStruct((n, d), F32), jax.ShapeDtypeStruct((n, d), BF16)],
        compiler_params=_cparams(("parallel",)),
        name="norm_residual",
    )(x, y, g_post.reshape(1, d), g_next.reshape(1, d))


GELU_C = math.sqrt(2.0 / math.pi)


def _gelu_tanh(x):
    z = x * (x * x * (-2.0 * GELU_C * 0.044715 * LOG2E) + (-2.0 * GELU_C * LOG2E))
    return x / (1.0 + jnp.exp2(z))


def _ffn_up_kernel(x_ref, wg_ref, wv_ref, cwg_ref, cwv_ref, cbg_ref, cbv_ref, o_ref,
                   wb_ref, u_ref, *, tm, tn, seq):
    i = pl.program_id(1)
    starts_seq = (i * tm) % seq == 0

    @pl.when(i == 0)
    def _():
        wb_ref[:, :tn] = wg_ref[...].astype(BF16)
        wb_ref[:, tn:] = wv_ref[...].astype(BF16)

    @pl.when(starts_seq)
    def _():
        u_ref[0:8, :] = jnp.zeros((8, 2 * tn), F32)

    @pl.when(jnp.logical_not(starts_seq))
    def _():
        u_ref[0:8, :] = u_ref[tm:tm + 8, :]

    u_ref[8:8 + tm, :] = _dot(x_ref[...], wb_ref[...])

    def conv(lo, w_ref, b_ref):
        cols = slice(lo, lo + tn)
        return (b_ref[...] + w_ref[0:1, :] * u_ref[6:6 + tm, cols]
                + w_ref[1:2, :] * u_ref[7:7 + tm, cols] + w_ref[2:3, :] * u_ref[8:8 + tm, cols])

    gate = conv(0, cwg_ref, cbg_ref)
    val = conv(tn, cwv_ref, cbv_ref)
    o_ref[...] = (_gelu_tanh(gate) * val).astype(o_ref.dtype)


def ffn_up_conv_geglu(h, w_up, conv_w, conv_b, layer, seq, tm, tn):
    m, kdim = h.shape
    dff = w_up.shape[2] // 2
    assert m % tm == 0 and dff % tn == 0 and seq % tm == 0
    nc = dff // tn
    wspec = lambda off: pl.BlockSpec((None, kdim, tn), lambda j, i: (layer, 0, off + j))
    cspec = lambda r, off: pl.BlockSpec((None, r, tn), lambda j, i: (layer, 0, off + j))
    return pl.pallas_call(
        functools.partial(_ffn_up_kernel, tm=tm, tn=tn, seq=seq),
        grid=(nc, m // tm),
        in_specs=[pl.BlockSpec((tm, kdim), lambda j, i: (i, 0)),
                  wspec(0), wspec(nc),
                  cspec(CONV_WIDTH, 0), cspec(CONV_WIDTH, nc),
                  cspec(1, 0), cspec(1, nc)],
        out_specs=pl.BlockSpec((tm, tn), lambda j, i: (i, j)),
        out_shape=jax.ShapeDtypeStruct((m, dff), BF16),
        scratch_shapes=[pltpu.VMEM((kdim, 2 * tn), BF16), pltpu.VMEM((tm + 8, 2 * tn), F32)],
        compiler_params=_cparams(("parallel", "arbitrary")),
        name="ffn_up",
    )(h, w_up, w_up, conv_w, conv_w, conv_b[:, None, :], conv_b[:, None, :])


def _in_layout(d, aw, iw, ih, hk, hv, kvl):
    segs = [("q_a", aw), ("c_kv", kvl), ("q_i", iw), ("k_i", HEAD_DIM), ("w_i", ih), ("pad", None),
            ("q_h", hk), ("f_h", hk), ("i_h", hv), ("g_h", hv), ("gate_a", d), ("gate_b", d)]
    lay, off = {}, 0
    for name, w in segs:
        if name == "pad":
            w = (-off) % IN_PAD_ALIGN
            lay["pad_width"] = w
        lay[name] = off
        off += w
    assert off % IN_PROJ_TN == 0
    lay["total"] = off
    return lay


def _pad_cast_kernel(lo_ref, hi_ref, o_ref, *, n_pad, n_tail, off, keep):
    n = pl.program_id(0)
    tn = o_ref.shape[0]

    @pl.when(n < n_pad)
    def _():
        o_ref[...] = lo_ref[...].astype(BF16)

    @pl.when(n == n_pad)
    def _():
        row = lax.broadcasted_iota(jnp.int32, (tn, 1), 0)
        o_ref[...] = jnp.where(row < keep, lo_ref[...], 0.0).astype(BF16)

    @pl.when((n > n_pad) & (n < n_tail))
    def _():
        o_ref[...] = jnp.zeros_like(o_ref)

    @pl.when(n >= n_tail)
    def _():
        o_ref[...] = jnp.concatenate([lo_ref[off:, :], hi_ref[:off, :]], axis=0).astype(BF16)


def pad_cast_w_in(w_in_t, layer, lay, ih):
    n_src, kdim = w_in_t.shape[1], w_in_t.shape[2]
    tn = IN_PROJ_TN
    assert lay["k_i"] % tn == 0 and lay["q_h"] % tn == 0 and ih % 8 == 0
    n_pad = lay["k_i"] // tn
    n_tail = lay["q_h"] // tn
    back = -(-lay["pad_width"] // tn)
    off = back * tn - lay["pad_width"]
    assert 0 < off < tn and off % 8 == 0
    last = (n_src - 1) // tn

    def src(extra):
        def index(n):
            blk = jnp.where(n <= n_pad, n, n - back + extra)
            return (layer, jnp.clip(blk, 0, last), 0)
        return pl.BlockSpec((None, tn, kdim), index)

    return pl.pallas_call(
        functools.partial(_pad_cast_kernel, n_pad=n_pad, n_tail=n_tail, off=off, keep=HEAD_DIM + ih),
        grid=(lay["total"] // tn,),
        in_specs=[src(0), src(1)],
        out_specs=pl.BlockSpec((tn, kdim), lambda n: (n, 0)),
        out_shape=jax.ShapeDtypeStruct((lay["total"], kdim), BF16),
        compiler_params=_cparams(("parallel",)),
        name="pad_cast_w_in",
    )(w_in_t, w_in_t)


def _pick(n, cands):
    for c in cands:
        if n % c == 0:
            return c
    raise ValueError(f"no tile for {n}")


def kernel(x, rel_bias, hgrn_lb_logits, mix_pre_g, mix_post_g, w_in, kv_norm_g, w_uk, w_uv, idx_k_ln_g, idx_k_ln_b, hgrn_norm_g, w_proj_attn, w_proj_hgrn, w_out, ffn_pre_g, ffn_post_g, w_up, conv_w, conv_b, w_down):
    batch, seq, d = x.shape
    depth = w_in.shape[0]
    aw = w_proj_attn.shape[1]
    hv = w_proj_hgrn.shape[1]
    hk = hgrn_lb_logits.shape[1]
    kvl = w_uk.shape[1]
    dff = w_down.shape[1]
    heads = aw // HEAD_DIM
    hheads = hv // HEAD_DIM
    ih = w_in.shape[2] - (aw + kvl + HEAD_DIM + 2 * hk + 2 * hv + 2 * d)
    ih = ih // (HEAD_DIM + 1)
    iw = ih * HEAD_DIM
    lay = _in_layout(d, aw, iw, ih, hk, hv, kvl)
    n = batch * seq

    lb_p = jax.nn.softmax(hgrn_lb_logits.astype(F32), axis=0)
    lb_all = jnp.cumsum(lb_p, axis=0) - lb_p[0:1]
    bias_tiles = rel_bias_tiles(rel_bias, heads)

    tm = _pick(n, (1024, 512, 256))
    w_in_t = jnp.swapaxes(w_in, 1, 2)
    xf = x.reshape(n, d)
    h = rmsnorm_cast(xf, mix_pre_g[0])
    for l in range(depth):
        w_in_l = pad_cast_w_in(w_in_t, l, lay, ih)
        proj = matmul_nt(h, w_in_l, tm, IN_PROJ_TN, name="in_proj")
        k, v, kidx, wi = dsa_prep(proj, lay, kv_norm_g[l], w_uk[l], w_uv[l],
                                  idx_k_ln_g[l], idx_k_ln_b[l], ih)
        attn = dsa_attention(proj, lay, wi, kidx, k, v, bias_tiles, batch, seq, heads, ih)
        hg = hgrn2(proj, lay, lb_all[l], hgrn_norm_g[l], batch, seq, hheads)
        merged = gated_merge(attn, hg, w_proj_attn, w_proj_hgrn, l, proj, lay, tm=tm)
        y = matmul_wstat(merged, w_out, l, tm, 512, name="out_proj")
        xf, h = norm_residual(xf, y, mix_post_g[l], ffn_pre_g[l])
        act = ffn_up_conv_geglu(h, w_up, conv_w, conv_b, l, seq, tm, _pick(dff, (256, 128)))
        y = matmul(act, w_down[l].astype(BF16), tm, 512, tk=_pick(dff, (dff // 2, dff)),
                   name="ffn_down")
        xf, h = norm_residual(xf, y, ffn_post_g[l], mix_pre_g[l + 1] if l + 1 < depth else None)
    return xf.reshape(batch, seq, d)
```

```python
import functools
import math

import jax
import jax.numpy as jnp
from jax import lax
from jax.experimental import pallas as pl
from jax.experimental.pallas import tpu as pltpu

F32 = jnp.float32
BF16 = jnp.bfloat16

NORM_EPS = 1e-6
HEAD_DIM = 128
IDX_TOPK = 256
REL_BUCKETS = 32
REL_MAX_DIST = 128
CONV_WIDTH = 3
LANES = 128
VMEM_LIMIT_BYTES = 56 * 1024 * 1024

ATT_BLOCK = 256
QI_BLOCK = 512
HGRN_CHUNK = 64
HGRN_SUB = 16
HGRN_SAFE_LOG2 = 100.0
IN_PROJ_TN = 512
IN_PAD_ALIGN = 1024
INT_MIN = -2 ** 31
MASK_NEG = -1e30
M_INIT = -1e20
LOG2E = math.log2(math.e)


def _cparams(sem):
    return pltpu.CompilerParams(dimension_semantics=sem, vmem_limit_bytes=VMEM_LIMIT_BYTES)


def _dot(a, b):
    return jnp.dot(a, b, preferred_element_type=F32)


def _dot_nt(a, b):
    return lax.dot_general(a, b, (((1,), (1,)), ((), ())), preferred_element_type=F32)


def _rmsnorm_kernel(x_ref, g_ref, o_ref):
    x = x_ref[...]
    r = lax.rsqrt(jnp.mean(x * x, axis=-1, keepdims=True) + NORM_EPS)
    o_ref[...] = (x * r * g_ref[...]).astype(o_ref.dtype)


def rmsnorm_cast(x, g, rows=256):
    n, d = x.shape
    return pl.pallas_call(
        _rmsnorm_kernel,
        grid=(n // rows,),
        in_specs=[pl.BlockSpec((rows, d), lambda i: (i, 0)),
                  pl.BlockSpec((1, d), lambda i: (0, 0))],
        out_specs=pl.BlockSpec((rows, d), lambda i: (i, 0)),
        out_shape=jax.ShapeDtypeStruct((n, d), BF16),
        compiler_params=_cparams(("parallel",)),
        name="rmsnorm_cast",
    )(x, g.reshape(1, d))


def _mm_kernel(x_ref, w_ref, o_ref):
    o_ref[...] = _dot(x_ref[...], w_ref[...]).astype(o_ref.dtype)


def _mm_acc_kernel(x_ref, w_ref, o_ref, acc_ref, *, nk):
    k = pl.program_id(2)

    @pl.when(k == 0)
    def _():
        acc_ref[...] = jnp.zeros_like(acc_ref)

    acc_ref[...] += _dot(x_ref[...], w_ref[...])

    @pl.when(k == nk - 1)
    def _():
        o_ref[...] = acc_ref[...].astype(o_ref.dtype)


def matmul(x, w, tm, tn, tk=None, out_dtype=BF16, name="matmul"):
    m, kdim = x.shape
    n = w.shape[1]
    tk = kdim if tk is None else tk
    nk = kdim // tk
    assert m % tm == 0 and n % tn == 0 and kdim % tk == 0
    if nk == 1:
        return pl.pallas_call(
            _mm_kernel,
            grid=(m // tm, n // tn),
            in_specs=[pl.BlockSpec((tm, kdim), lambda i, j: (i, 0)),
                      pl.BlockSpec((kdim, tn), lambda i, j: (0, j))],
            out_specs=pl.BlockSpec((tm, tn), lambda i, j: (i, j)),
            out_shape=jax.ShapeDtypeStruct((m, n), out_dtype),
            compiler_params=_cparams(("parallel", "parallel")),
            name=name,
        )(x, w)
    return pl.pallas_call(
        functools.partial(_mm_acc_kernel, nk=nk),
        grid=(m // tm, n // tn, nk),
        in_specs=[pl.BlockSpec((tm, tk), lambda i, j, k: (i, k)),
                  pl.BlockSpec((tk, tn), lambda i, j, k: (k, j))],
        out_specs=pl.BlockSpec((tm, tn), lambda i, j, k: (i, j)),
        out_shape=jax.ShapeDtypeStruct((m, n), out_dtype),
        scratch_shapes=[pltpu.VMEM((tm, tn), F32)],
        compiler_params=_cparams(("parallel", "parallel", "arbitrary")),
        name=name,
    )(x, w)


def _mm_wstat_kernel(x_ref, w_ref, o_ref, wb_ref):
    @pl.when(pl.program_id(1) == 0)
    def _():
        wb_ref[...] = w_ref[...].astype(BF16)

    o_ref[...] = _dot(x_ref[...], wb_ref[...]).astype(o_ref.dtype)


def matmul_wstat(x, w, layer, tm, tn, out_dtype=BF16, name="matmul_wstat"):
    m, kdim = x.shape
    n = w.shape[2]
    assert m % tm == 0 and n % tn == 0
    return pl.pallas_call(
        _mm_wstat_kernel,
        grid=(n // tn, m // tm),
        in_specs=[pl.BlockSpec((tm, kdim), lambda j, i: (i, 0)),
                  pl.BlockSpec((None, kdim, tn), lambda j, i: (layer, 0, j))],
        out_specs=pl.BlockSpec((tm, tn), lambda j, i: (i, j)),
        out_shape=jax.ShapeDtypeStruct((m, n), out_dtype),
        scratch_shapes=[pltpu.VMEM((kdim, tn), BF16)],
        compiler_params=_cparams(("parallel", "arbitrary")),
        name=name,
    )(x, w)


def _dsa_prep_kernel(ckv_ref, kw_ref, kvg_ref, wuk_ref, wuv_ref, lng_ref, lnb_ref,
                     k_ref, v_ref, kidx_ref, wi_ref, *, k_scale, w_scale):
    c = ckv_ref[...].astype(F32)
    c = c * lax.rsqrt(jnp.mean(c * c, axis=-1, keepdims=True) + NORM_EPS) * kvg_ref[...]
    cb = c.astype(BF16)
    k_ref[...] = (_dot(cb, wuk_ref[...]) * k_scale).astype(k_ref.dtype)
    v_ref[:, :HEAD_DIM] = _dot(cb, wuv_ref[...]).astype(v_ref.dtype)
    v_ref[:, HEAD_DIM:] = jnp.ones((v_ref.shape[0], HEAD_DIM), v_ref.dtype)
    kw = kw_ref[...].astype(F32)
    ki = kw[:, :HEAD_DIM]
    mu = jnp.mean(ki, axis=-1, keepdims=True)
    kc = ki - mu
    kn = kc * lax.rsqrt(jnp.mean(kc * kc, axis=-1, keepdims=True) + NORM_EPS)
    kidx_ref[...] = (kn * lng_ref[...] + lnb_ref[...]).astype(kidx_ref.dtype)
    wi_ref[...] = kw[:, HEAD_DIM:] * w_scale


def dsa_prep(proj, lay, kv_norm_g, w_uk, w_uv, ln_g, ln_b, idx_heads, rows=512):
    n = proj.shape[0]
    kvl = w_uk.shape[0]
    row = lambda i: (i, 0)
    const = lambda i: (0, 0)
    out = jax.ShapeDtypeStruct((n, HEAD_DIM), BF16)
    return pl.pallas_call(
        functools.partial(_dsa_prep_kernel, k_scale=HEAD_DIM ** -0.5 * LOG2E,
                          w_scale=idx_heads ** -0.5 * HEAD_DIM ** -0.5),
        grid=(n // rows,),
        in_specs=[pl.BlockSpec((rows, kvl), lambda i: (i, lay["c_kv"] // kvl)),
                  pl.BlockSpec((rows, 2 * HEAD_DIM), lambda i: (i, lay["k_i"] // (2 * HEAD_DIM))),
                  pl.BlockSpec((1, kvl), const),
                  pl.BlockSpec((kvl, HEAD_DIM), const),
                  pl.BlockSpec((kvl, HEAD_DIM), const),
                  pl.BlockSpec((1, HEAD_DIM), const),
                  pl.BlockSpec((1, HEAD_DIM), const)],
        out_specs=[pl.BlockSpec((rows, HEAD_DIM), row), pl.BlockSpec((rows, 2 * HEAD_DIM), row),
                   pl.BlockSpec((rows, HEAD_DIM), row), pl.BlockSpec((rows, HEAD_DIM), row)],
        out_shape=[out, jax.ShapeDtypeStruct((n, 2 * HEAD_DIM), BF16), out,
                   jax.ShapeDtypeStruct((n, HEAD_DIM), F32)],
        compiler_params=_cparams(("parallel",)),
        name="dsa_prep",
    )(proj, proj, kv_norm_g.reshape(1, kvl), w_uk.astype(BF16), w_uv.astype(BF16),
      ln_g.reshape(1, HEAD_DIM), ln_b.reshape(1, HEAD_DIM))


def _sortable_key(x):
    u = lax.bitcast_convert_type(x, jnp.int32)
    return u ^ ((u >> 31) & jnp.int32(0x7FFFFFFF))


def _dsa_kernel(*refs, heads, idx_heads, topk, n_qi):
    qa_ref = refs[0]
    qi_refs = refs[1:1 + n_qi]
    (wi_ref, kidx_ref, k_ref, v_ref, bias_ref, o_ref,
     qas_ref, qis_ref, keys_ref, mask_ref, m_ref, acc_ref) = refs[1 + n_qi:]
    qb = ATT_BLOCK
    ck = ATT_BLOCK
    j = pl.program_id(1)

    for h in range(heads):
        qas_ref[h * qb:(h + 1) * qb, :] = qa_ref[:, h * HEAD_DIM:(h + 1) * HEAD_DIM]
    for h in range(idx_heads):
        src, off = divmod(h * HEAD_DIM, QI_BLOCK)
        qis_ref[h * qb:(h + 1) * qb, :] = qi_refs[src][:, off:off + HEAD_DIM]

    row_pos = j * qb + lax.broadcasted_iota(jnp.int32, (qb, ck), 0)
    lane = lax.broadcasted_iota(jnp.int32, (qb, ck), 1)
    hg = 8 if idx_heads % 8 == 0 else idx_heads

    def score_chunk(c, carry):
        kc = kidx_ref[pl.ds(pl.multiple_of(c * ck, ck), ck), :]
        acc = jnp.zeros((qb, ck), F32)
        for g in range(idx_heads // hg):
            s = _dot_nt(qis_ref[g * hg * qb:(g + 1) * hg * qb, :], kc)
            for hh in range(hg):
                h = g * hg + hh
                acc = acc + jnp.maximum(s[hh * qb:(hh + 1) * qb, :], 0.0) * wi_ref[:, h:h + 1]
        valid = (c * ck + lane) <= row_pos
        keys_ref[c] = jnp.where(valid, _sortable_key(acc), jnp.int32(INT_MIN))
        return carry

    lax.fori_loop(0, j + 1, score_chunk, 0)

    def bit_step(i, lo):
        cand = lo + lax.shift_left(jnp.int32(1), jnp.int32(31) - i)

        def count_chunk(c, part):
            ge = (keys_ref[c] >= cand).astype(jnp.int32)
            for t in range(ck // LANES):
                part = part + ge[:, t * LANES:(t + 1) * LANES]
            return part

        part = lax.fori_loop(0, j + 1, count_chunk, jnp.zeros((qb, LANES), jnp.int32))
        cnt = jnp.sum(part, axis=-1, keepdims=True)
        return jnp.where(cnt >= topk, cand, lo)

    lo = lax.fori_loop(0, 32, bit_step, jnp.full((qb, 1), INT_MIN, jnp.int32))
    thr = jnp.maximum(lo, jnp.int32(INT_MIN + 1))

    m_ref[...] = jnp.full_like(m_ref, M_INIT)
    acc_ref[...] = jnp.zeros_like(acc_ref)

    def attend(c, bias_idx):
        off = pl.multiple_of(c * ck, ck)
        kc = k_ref[pl.ds(off, ck), :]
        vc = v_ref[pl.ds(off, ck), :]
        mask_ref[...] = jnp.where(keys_ref[c] >= thr, 0.0, MASK_NEG)
        for h in range(heads):
            rows = slice(h * qb, (h + 1) * qb)
            s = _dot_nt(qas_ref[rows, :], kc) + mask_ref[...]
            if bias_idx is not None:
                s = s + bias_ref[bias_idx, rows, :]
            parts = [s[:, t * LANES:(t + 1) * LANES] for t in range(ck // LANES)]
            smax = functools.reduce(jnp.maximum, parts)
            m_prev = m_ref[rows, :]
            m_new = jnp.maximum(m_prev, jnp.max(smax, axis=-1, keepdims=True))
            p = jnp.concatenate([jnp.exp2(x - m_new) for x in parts], axis=1).astype(BF16)
            alpha = jnp.exp2(m_prev - m_new)
            pv = _dot(p, vc)
            acc_ref[rows, :HEAD_DIM] = alpha * acc_ref[rows, :HEAD_DIM] + pv[:, :HEAD_DIM]
            acc_ref[rows, HEAD_DIM:] = alpha * acc_ref[rows, HEAD_DIM:] + pv[:, HEAD_DIM:]
            m_ref[rows, :] = m_new

    def far_chunk(c, carry):
        attend(c, None)
        return carry

    lax.fori_loop(0, j - 1, far_chunk, 0)

    @pl.when(j >= 1)
    def _():
        attend(j - 1, 1)

    attend(j, 0)

    for h in range(heads):
        rows = slice(h * qb, (h + 1) * qb)
        o = acc_ref[rows, :HEAD_DIM] / acc_ref[rows, HEAD_DIM:]
        o_ref[:, h * HEAD_DIM:(h + 1) * HEAD_DIM] = o.astype(o_ref.dtype)


def dsa_attention(proj, lay, wi, kidx, k, v, bias_tiles, batch, seq, heads, idx_heads):
    qb = ATT_BLOCK
    nqb = seq // qb
    aw = heads * HEAD_DIM
    iw = idx_heads * HEAD_DIM
    topk = min(IDX_TOPK, seq // 4)
    assert lay["q_i"] % QI_BLOCK == 0 and iw % QI_BLOCK == 0 and lay["q_a"] % aw == 0
    n_qi = iw // QI_BLOCK
    return pl.pallas_call(
        functools.partial(_dsa_kernel, heads=heads, idx_heads=idx_heads, topk=topk, n_qi=n_qi),
        grid=(batch, nqb),
        in_specs=[pl.BlockSpec((qb, aw), lambda b, j: (b * nqb + j, lay["q_a"] // aw)),
                  *[pl.BlockSpec((qb, QI_BLOCK), lambda b, j, t=t: (b * nqb + j, lay["q_i"] // QI_BLOCK + t))
                    for t in range(n_qi)],
                  pl.BlockSpec((qb, HEAD_DIM), lambda b, j: (b * nqb + j, 0)),
                  pl.BlockSpec((seq, HEAD_DIM), lambda b, j: (b, 0)),
                  pl.BlockSpec((seq, HEAD_DIM), lambda b, j: (b, 0)),
                  pl.BlockSpec((seq, 2 * HEAD_DIM), lambda b, j: (b, 0)),
                  pl.BlockSpec((2, heads * qb, qb), lambda b, j: (0, 0, 0),
                               pipeline_mode=pl.Buffered(1))],
        out_specs=pl.BlockSpec((qb, aw), lambda b, j: (b * nqb + j, 0)),
        out_shape=jax.ShapeDtypeStruct((batch * seq, aw), BF16),
        scratch_shapes=[pltpu.VMEM((heads * qb, HEAD_DIM), BF16),
                        pltpu.VMEM((idx_heads * qb, HEAD_DIM), BF16),
                        pltpu.VMEM((seq // qb, qb, qb), jnp.int32),
                        pltpu.VMEM((qb, qb), F32),
                        pltpu.VMEM((heads * qb, LANES), F32),
                        pltpu.VMEM((heads * qb, 2 * HEAD_DIM), F32)],
        compiler_params=_cparams(("parallel", "arbitrary")),
        name="dsa_attention",
    )(proj, *([proj] * n_qi), wi, kidx, k, v, bias_tiles)


def _t5_bucket(dist):
    max_exact = REL_BUCKETS // 2
    d = jnp.maximum(dist, 0)
    df = jnp.maximum(d, 1).astype(F32)
    large = max_exact + (jnp.log(df / max_exact) / math.log(REL_MAX_DIST / max_exact)
                         * (REL_BUCKETS - max_exact)).astype(jnp.int32)
    large = jnp.minimum(large, REL_BUCKETS - 1)
    return jnp.where(d < max_exact, d, large)


def rel_bias_tiles(rel_bias, heads):
    qb = ATT_BLOCK
    assert _static_far_bucket(qb) == REL_BUCKETS - 1
    tbl = (rel_bias[_t5_bucket(jnp.arange(4 * qb))] - rel_bias[REL_BUCKETS - 1][None, :]).T * LOG2E
    y = jnp.arange(2 * qb)
    tiles = []
    for off in (0, qb):
        d = off + jnp.where(y < qb, -y, 2 * qb - y)
        u = tbl[:, jnp.maximum(d, 0)]
        flat = jnp.broadcast_to(u[:, None, :], (heads, qb, 2 * qb)).reshape(heads, 2 * qb * qb)
        skew = flat[:, :qb * (2 * qb - 1)].reshape(heads, qb, 2 * qb - 1)
        tiles.append(skew[:, :, :qb].reshape(heads * qb, qb))
    return jnp.stack(tiles).astype(F32)


def _static_far_bucket(d):
    max_exact = REL_BUCKETS // 2
    v = max_exact + int(math.log(d / max_exact) / math.log(REL_MAX_DIST / max_exact)
                        * (REL_BUCKETS - max_exact))
    return min(v, REL_BUCKETS - 1)


def _hgrn_kernel(q_ref, f_ref, i_ref, g_ref, lb_ref, ng_ref, o_ref,
                 st_ref, a_ref, k_ref, qf_ref, p_ref, *, heads_per_step, tb):
    c = HGRN_CHUNK
    sub = HGRN_SUB
    nsub = c // sub
    hs = range(heads_per_step)

    @pl.when(pl.program_id(2) == 0)
    def _():
        st_ref[...] = jnp.zeros_like(st_ref)

    r_io = lax.broadcasted_iota(jnp.int32, (c, c), 0)
    c_io = lax.broadcasted_iota(jnp.int32, (c, c), 1)
    tri = (c_io <= r_io).astype(F32)
    sub_r = lax.broadcasted_iota(jnp.int32, (sub, c), 0)
    sub_c = lax.broadcasted_iota(jnp.int32, (sub, c), 1)
    ng = ng_ref[...]

    def a_start(hh, sb):
        return a_ref[hh, sb * sub - 1:sb * sub, :] if sb > 0 else jnp.zeros((1, HEAD_DIM), F32)

    def chunk(ci, carry):
        r0 = pl.multiple_of(ci * c, c)

        worst = jnp.zeros((1, HEAD_DIM), F32)
        for hh in hs:
            cols = slice(hh * HEAD_DIM, (hh + 1) * HEAD_DIM)
            lb = lb_ref[:, cols]
            z = f_ref[pl.ds(r0, c), cols].astype(F32)
            qx = q_ref[pl.ds(r0, c), cols].astype(F32)
            sig = jax.nn.sigmoid(z)
            lf = jnp.log2(lb + (1.0 - lb) * sig)
            k_ref[hh] = (1.0 - lb) * jax.nn.sigmoid(-z)
            qf_ref[hh] = qx * jax.nn.sigmoid(qx)
            a = jnp.dot(tri, lf, preferred_element_type=F32, precision=lax.Precision.HIGHEST)
            a_ref[hh] = a
            for sb in range(nsub):
                top = a[sb * sub - 1:sb * sub, :] if sb > 0 else jnp.zeros((1, HEAD_DIM), F32)
                worst = jnp.maximum(worst, top - a[(sb + 1) * sub - 1:(sb + 1) * sub, :])
        safe = jnp.max(worst) <= HGRN_SAFE_LOG2

        @pl.when(safe)
        def _():
            for hh in hs:
                a = a_ref[hh]
                kf = k_ref[hh]
                for sb in range(nsub):
                    rs = slice(sb * sub, (sb + 1) * sub)
                    top = a_start(hh, sb)
                    q_t = qf_ref[hh, rs, :] * jnp.exp2(a_ref[hh, rs, :] - top)
                    k_t = kf * jnp.exp2(top - a)
                    p = _dot_nt(q_t.astype(BF16), k_t.astype(BF16))
                    p_ref[hh, rs, :] = jnp.where(sub_c <= sub_r + sb * sub, p, 0.0)

        @pl.when(jnp.logical_not(safe))
        def _():
            for hh in hs:
                a = a_ref[hh]
                kf = k_ref[hh]
                for sb in range(nsub):
                    rs = slice(sb * sub, (sb + 1) * sub)
                    top = a_start(hh, sb)
                    a_blk = a_ref[hh, rs, :]
                    q_blk = qf_ref[hh, rs, :]
                    q_t = q_blk * jnp.exp2(a_blk - top)
                    k_t = kf * jnp.exp2(top - a)
                    p_off = _dot_nt(q_t.astype(BF16), k_t.astype(BF16))
                    p_diag = jnp.zeros((sub, c), F32)
                    for s in range(sub):
                        row = sb * sub + s
                        e = jnp.exp2(a_blk - a_ref[hh, row:row + 1, :])
                        col = jnp.sum(q_blk * e * k_ref[hh, row:row + 1, :], axis=-1, keepdims=True)
                        p_diag = jnp.where((sub_c == row) & (sub_r >= s), col, p_diag)
                    p_ref[hh, rs, :] = jnp.where(sub_c < sb * sub, p_off, p_diag)

        for hh in hs:
            cols = slice(hh * HEAD_DIM, (hh + 1) * HEAD_DIM)
            a = a_ref[hh]
            kf = k_ref[hh]
            v = i_ref[pl.ds(r0, c), cols].astype(F32)
            gx = g_ref[pl.ds(r0, c), cols].astype(F32)
            vb = v.astype(BF16)
            st = st_ref[hh]
            o = (_dot(p_ref[hh].astype(BF16), vb)
                 + _dot_nt((qf_ref[hh] * jnp.exp2(a)).astype(BF16), st.astype(BF16)))
            a_last = a[c - 1:c, :]
            k_end = kf * jnp.exp2(a_last - a)
            st_ref[hh] = st * jnp.exp2(a_last) + _dot(v.T.astype(BF16), k_end.astype(BF16))
            o = o * lax.rsqrt(jnp.mean(o * o, axis=-1, keepdims=True) + NORM_EPS) * ng
            o_ref[pl.ds(r0, c), cols] = (o * (gx * jax.nn.sigmoid(gx))).astype(o_ref.dtype)
        return carry

    lax.fori_loop(0, tb // c, chunk, 0)


def hgrn2(proj, lay, lb, norm_g, batch, seq, heads, heads_per_step=8, tb=512):
    heads_per_step = min(heads_per_step, heads)
    assert heads % heads_per_step == 0
    hw = heads_per_step * HEAD_DIM
    nt = seq // tb
    ng = heads // heads_per_step

    def col(name):
        base = lay[name] // hw
        return lambda b, h, t: (b * nt + t, base + h)

    return pl.pallas_call(
        functools.partial(_hgrn_kernel, heads_per_step=heads_per_step, tb=tb),
        grid=(batch, ng, nt),
        in_specs=[pl.BlockSpec((tb, hw), col("q_h")),
                  pl.BlockSpec((tb, hw), col("f_h")),
                  pl.BlockSpec((tb, hw), col("i_h")),
                  pl.BlockSpec((tb, hw), col("g_h")),
                  pl.BlockSpec((1, hw), lambda b, h, t: (0, h)),
                  pl.BlockSpec((1, HEAD_DIM), lambda b, h, t: (0, 0))],
        out_specs=pl.BlockSpec((tb, hw), lambda b, h, t: (b * nt + t, h)),
        out_shape=jax.ShapeDtypeStruct((batch * seq, heads * HEAD_DIM), BF16),
        scratch_shapes=[pltpu.VMEM((heads_per_step, HEAD_DIM, HEAD_DIM), F32),
                        pltpu.VMEM((heads_per_step, HGRN_CHUNK, HEAD_DIM), F32),
                        pltpu.VMEM((heads_per_step, HGRN_CHUNK, HEAD_DIM), F32),
                        pltpu.VMEM((heads_per_step, HGRN_CHUNK, HEAD_DIM), F32),
                        pltpu.VMEM((heads_per_step, HGRN_CHUNK, HGRN_CHUNK), F32)],
        compiler_params=_cparams(("parallel", "parallel", "arbitrary")),
        name="hgrn2",
    )(proj, proj, proj, proj, lb.reshape(1, -1), norm_g.reshape(1, HEAD_DIM))


def _merge_kernel(a_ref, b_ref, wa_ref, wb_ref, ga_ref, gb_ref, o_ref, wab_ref, wbb_ref):
    @pl.when(pl.program_id(1) == 0)
    def _():
        wab_ref[...] = wa_ref[...].astype(BF16)
        wbb_ref[...] = wb_ref[...].astype(BF16)

    ya = _dot(a_ref[...], wab_ref[...])
    yb = _dot(b_ref[...], wbb_ref[...])
    ga = jax.nn.sigmoid(ga_ref[...].astype(F32))
    gb = jax.nn.sigmoid(gb_ref[...].astype(F32))
    o_ref[...] = (ga * ya + gb * yb).astype(o_ref.dtype)


def gated_merge(attn, hg, w_pa, w_pb, layer, proj, lay, tm=1024, tn=512):
    m, ka = attn.shape
    kb = hg.shape[1]
    d = w_pa.shape[2]
    assert lay["gate_a"] % tn == 0 and lay["gate_b"] % tn == 0
    ga0, gb0 = lay["gate_a"] // tn, lay["gate_b"] // tn
    return pl.pallas_call(
        _merge_kernel,
        grid=(d // tn, m // tm),
        in_specs=[pl.BlockSpec((tm, ka), lambda j, i: (i, 0)),
                  pl.BlockSpec((tm, kb), lambda j, i: (i, 0)),
                  pl.BlockSpec((None, ka, tn), lambda j, i: (layer, 0, j)),
                  pl.BlockSpec((None, kb, tn), lambda j, i: (layer, 0, j)),
                  pl.BlockSpec((tm, tn), lambda j, i: (i, ga0 + j)),
                  pl.BlockSpec((tm, tn), lambda j, i: (i, gb0 + j))],
        out_specs=pl.BlockSpec((tm, tn), lambda j, i: (i, j)),
        out_shape=jax.ShapeDtypeStruct((m, d), BF16),
        scratch_shapes=[pltpu.VMEM((ka, tn), BF16), pltpu.VMEM((kb, tn), BF16)],
        compiler_params=_cparams(("parallel", "arbitrary")),
        name="gated_merge",
    )(attn, hg, w_pa, w_pb, proj, proj)


def _norm_residual_kernel(x_ref, y_ref, gp_ref, gn_ref, xo_ref, ho_ref):
    y = y_ref[...].astype(F32)
    yn = y * lax.rsqrt(jnp.mean(y * y, axis=-1, keepdims=True) + NORM_EPS) * gp_ref[...]
    x = x_ref[...] + yn
    xo_ref[...] = x
    ho_ref[...] = (x * lax.rsqrt(jnp.mean(x * x, axis=-1, keepdims=True) + NORM_EPS)
                   * gn_ref[...]).astype(ho_ref.dtype)


def _norm_residual_last_kernel(x_ref, y_ref, gp_ref, xo_ref):
    y = y_ref[...].astype(F32)
    xo_ref[...] = x_ref[...] + y * lax.rsqrt(jnp.mean(y * y, axis=-1, keepdims=True) + NORM_EPS) * gp_ref[...]


def norm_residual(x, y, g_post, g_next=None, rows=256):
    n, d = x.shape
    row = pl.BlockSpec((rows, d), lambda i: (i, 0))
    vec = pl.BlockSpec((1, d), lambda i: (0, 0))
    if g_next is None:
        return pl.pallas_call(
            _norm_residual_last_kernel,
            grid=(n // rows,),
            in_specs=[row, row, vec],
            out_specs=row,
            out_shape=jax.ShapeDtypeStruct((n, d), F32),
            compiler_params=_cparams(("parallel",)),
            name="norm_residual_last",
        )(x, y, g_post.reshape(1, d)), None
    return pl.pallas_call(
        _norm_residual_kernel,
        grid=(n // rows,),
        in_specs=[row, row, vec, vec],
        out_specs=[row, row],
        out_shape=[jax.ShapeDtypeStruct((n, d), F32), jax.ShapeDtypeStruct((n, d), BF16)],
        compiler_params=_cparams(("parallel",)),
        name="norm_residual",
    )(x, y, g_post.reshape(1, d), g_next.reshape(1, d))


GELU_C = math.sqrt(2.0 / math.pi)


def _gelu_tanh(x):
    z = x * (x * x * (-2.0 * GELU_C * 0.044715 * LOG2E) + (-2.0 * GELU_C * LOG2E))
    return x / (1.0 + jnp.exp2(z))


def _ffn_up_kernel(x_ref, wg_ref, wv_ref, cwg_ref, cwv_ref, cbg_ref, cbv_ref, o_ref,
                   wb_ref, u_ref, *, tm, tn, seq):
    i = pl.program_id(1)
    starts_seq = (i * tm) % seq == 0

    @pl.when(i == 0)
    def _():
        wb_ref[:, :tn] = wg_ref[...].astype(BF16)
        wb_ref[:, tn:] = wv_ref[...].astype(BF16)

    @pl.when(starts_seq)
    def _():
        u_ref[0:8, :] = jnp.zeros((8, 2 * tn), F32)

    @pl.when(jnp.logical_not(starts_seq))
    def _():
        u_ref[0:8, :] = u_ref[tm:tm + 8, :]

    u_ref[8:8 + tm, :] = _dot(x_ref[...], wb_ref[...])

    def conv(lo, w_ref, b_ref):
        cols = slice(lo, lo + tn)
        return (b_ref[...] + w_ref[0:1, :] * u_ref[6:6 + tm, cols]
                + w_ref[1:2, :] * u_ref[7:7 + tm, cols] + w_ref[2:3, :] * u_ref[8:8 + tm, cols])

    gate = conv(0, cwg_ref, cbg_ref)
    val = conv(tn, cwv_ref, cbv_ref)
    o_ref[...] = (_gelu_tanh(gate) * val).astype(o_ref.dtype)


def ffn_up_conv_geglu(h, w_up, conv_w, conv_b, layer, seq, tm, tn):
    m, kdim = h.shape
    dff = w_up.shape[2] // 2
    assert m % tm == 0 and dff % tn == 0 and seq % tm == 0
    nc = dff // tn
    wspec = lambda off: pl.BlockSpec((None, kdim, tn), lambda j, i: (layer, 0, off + j))
    cspec = lambda r, off: pl.BlockSpec((None, r, tn), lambda j, i: (layer, 0, off + j))
    return pl.pallas_call(
        functools.partial(_ffn_up_kernel, tm=tm, tn=tn, seq=seq),
        grid=(nc, m // tm),
        in_specs=[pl.BlockSpec((tm, kdim), lambda j, i: (i, 0)),
                  wspec(0), wspec(nc),
                  cspec(CONV_WIDTH, 0), cspec(CONV_WIDTH, nc),
                  cspec(1, 0), cspec(1, nc)],
        out_specs=pl.BlockSpec((tm, tn), lambda j, i: (i, j)),
        out_shape=jax.ShapeDtypeStruct((m, dff), BF16),
        scratch_shapes=[pltpu.VMEM((kdim, 2 * tn), BF16), pltpu.VMEM((tm + 8, 2 * tn), F32)],
        compiler_params=_cparams(("parallel", "arbitrary")),
        name="ffn_up",
    )(h, w_up, w_up, conv_w, conv_w, conv_b[:, None, :], conv_b[:, None, :])


def _in_layout(d, aw, iw, ih, hk, hv, kvl):
    segs = [("q_a", aw), ("c_kv", kvl), ("q_i", iw), ("k_i", HEAD_DIM), ("w_i", ih), ("pad", None),
            ("q_h", hk), ("f_h", hk), ("i_h", hv), ("g_h", hv), ("gate_a", d), ("gate_b", d)]
    lay, off = {}, 0
    for name, w in segs:
        if name == "pad":
            w = (-off) % IN_PAD_ALIGN
            lay["pad_width"] = w
        lay[name] = off
        off += w
    assert off % IN_PROJ_TN == 0
    lay["total"] = off
    return lay


def _pad_cast_kernel(src_ref, o_ref, prev_ref, *, n_pad, n_tail, off, keep):
    n = pl.program_id(0)
    tn = o_ref.shape[1]

    @pl.when(n < n_pad)
    def _():
        o_ref[...] = src_ref[...].T.astype(BF16)

    @pl.when(n == n_pad)
    def _():
        row = lax.broadcasted_iota(jnp.int32, (tn, 1), 0)
        o_ref[...] = jnp.where(row < keep, src_ref[...], 0.0).T.astype(BF16)
        prev_ref[...] = src_ref[...]

    @pl.when((n > n_pad) & (n < n_tail))
    def _():
        o_ref[...] = jnp.zeros_like(o_ref)

    @pl.when(n >= n_tail)
    def _():
        o_ref[:, :tn - off] = prev_ref[off:, :].T.astype(BF16)
        o_ref[:, tn - off:] = src_ref[:off, :].T.astype(BF16)
        prev_ref[...] = src_ref[...]


def pad_cast_w_in(w_in_t, layer, lay, ih):
    n_src, kdim = w_in_t.shape[1], w_in_t.shape[2]
    tn = IN_PROJ_TN
    assert lay["k_i"] % tn == 0 and lay["q_h"] % tn == 0 and ih % 8 == 0
    n_pad = lay["k_i"] // tn
    n_tail = lay["q_h"] // tn
    back = -(-lay["pad_width"] // tn)
    off = back * tn - lay["pad_width"]
    assert 0 < off < tn and off % 8 == 0 and n_tail - back == n_pad
    last = (n_src - 1) // tn

    def index(n):
        blk = jnp.where(n <= n_pad, n, n - back + 1)
        return (layer, jnp.clip(blk, 0, last), 0)

    return pl.pallas_call(
        functools.partial(_pad_cast_kernel, n_pad=n_pad, n_tail=n_tail, off=off, keep=HEAD_DIM + ih),
        grid=(lay["total"] // tn,),
        in_specs=[pl.BlockSpec((None, tn, kdim), index)],
        out_specs=pl.BlockSpec((kdim, tn), lambda n: (0, n)),
        out_shape=jax.ShapeDtypeStruct((kdim, lay["total"]), BF16),
        scratch_shapes=[pltpu.VMEM((tn, kdim), F32)],
        compiler_params=_cparams(("arbitrary",)),
        name="pad_cast_w_in",
    )(w_in_t)


def _pick(n, cands):
    for c in cands:
        if n % c == 0:
            return c
    raise ValueError(f"no tile for {n}")


def kernel(x, rel_bias, hgrn_lb_logits, mix_pre_g, mix_post_g, w_in, kv_norm_g, w_uk, w_uv, idx_k_ln_g, idx_k_ln_b, hgrn_norm_g, w_proj_attn, w_proj_hgrn, w_out, ffn_pre_g, ffn_post_g, w_up, conv_w, conv_b, w_down):
    batch, seq, d = x.shape
    depth = w_in.shape[0]
    aw = w_proj_attn.shape[1]
    hv = w_proj_hgrn.shape[1]
    hk = hgrn_lb_logits.shape[1]
    kvl = w_uk.shape[1]
    dff = w_down.shape[1]
    heads = aw // HEAD_DIM
    hheads = hv // HEAD_DIM
    ih = w_in.shape[2] - (aw + kvl + HEAD_DIM + 2 * hk + 2 * hv + 2 * d)
    ih = ih // (HEAD_DIM + 1)
    iw = ih * HEAD_DIM
    lay = _in_layout(d, aw, iw, ih, hk, hv, kvl)
    n = batch * seq

    lb_p = jax.nn.softmax(hgrn_lb_logits.astype(F32), axis=0)
    lb_all = jnp.cumsum(lb_p, axis=0) - lb_p[0:1]
    bias_tiles = rel_bias_tiles(rel_bias, heads)

    tm = _pick(n, (1024, 512, 256))
    w_in_t = jnp.swapaxes(w_in, 1, 2)
    xf = x.reshape(n, d)
    h = rmsnorm_cast(xf, mix_pre_g[0])
    for l in range(depth):
        w_in_l = pad_cast_w_in(w_in_t, l, lay, ih)
        proj = matmul(h, w_in_l, _pick(n, (2 * tm, tm)), IN_PROJ_TN, name="in_proj")
        k, v, kidx, wi = dsa_prep(proj, lay, kv_norm_g[l], w_uk[l], w_uv[l],
                                  idx_k_ln_g[l], idx_k_ln_b[l], ih)
        attn = dsa_attention(proj, lay, wi, kidx, k, v, bias_tiles, batch, seq, heads, ih)
        hg = hgrn2(proj, lay, lb_all[l], hgrn_norm_g[l], batch, seq, hheads)
        merged = gated_merge(attn, hg, w_proj_attn, w_proj_hgrn, l, proj, lay, tm=tm)
        y = matmul_wstat(merged, w_out, l, tm, 512, name="out_proj")
        xf, h = norm_residual(xf, y, mix_post_g[l], ffn_pre_g[l])
        act = ffn_up_conv_geglu(h, w_up, conv_w, conv_b, l, seq, tm, _pick(dff, (256, 128)))
        y = matmul(act, w_down[l].astype(BF16), tm, 512, tk=_pick(dff, (dff // 2, dff)),
                   name="ffn_down")
        xf, h = norm_residual(xf, y, ffn_post_g[l], mix_pre_g[l + 1] if l + 1 < depth else None)
    return xf.reshape(batch, seq, d)
```

```python
import functools
import math

import jax
import jax.numpy as jnp
from jax import lax
from jax.experimental import pallas as pl
from jax.experimental.pallas import tpu as pltpu

F32 = jnp.float32
BF16 = jnp.bfloat16

NORM_EPS = 1e-6
HEAD_DIM = 128
IDX_TOPK = 256
REL_BUCKETS = 32
REL_MAX_DIST = 128
CONV_WIDTH = 3
LANES = 128
VMEM_LIMIT_BYTES = 56 * 1024 * 1024

ATT_BLOCK = 256
QI_BLOCK = 512
HGRN_CHUNK = 64
HGRN_SUB = 16
HGRN_SAFE_LOG2 = 100.0
IN_PROJ_TN = 512
IN_PAD_ALIGN = 1024
INT_MIN = -2 ** 31
MASK_NEG = -1e30
M_INIT = -1e20
LOG2E = math.log2(math.e)


def _cparams(sem):
    return pltpu.CompilerParams(dimension_semantics=sem, vmem_limit_bytes=VMEM_LIMIT_BYTES)


def _dot(a, b):
    return jnp.dot(a, b, preferred_element_type=F32)


def _dot_nt(a, b):
    return lax.dot_general(a, b, (((1,), (1,)), ((), ())), preferred_element_type=F32)


def _rmsnorm_kernel(x_ref, g_ref, o_ref):
    x = x_ref[...]
    r = lax.rsqrt(jnp.mean(x * x, axis=-1, keepdims=True) + NORM_EPS)
    o_ref[...] = (x * r * g_ref[...]).astype(o_ref.dtype)


def rmsnorm_cast(x, g, rows=256):
    n, d = x.shape
    return pl.pallas_call(
        _rmsnorm_kernel,
        grid=(n // rows,),
        in_specs=[pl.BlockSpec((rows, d), lambda i: (i, 0)),
                  pl.BlockSpec((1, d), lambda i: (0, 0))],
        out_specs=pl.BlockSpec((rows, d), lambda i: (i, 0)),
        out_shape=jax.ShapeDtypeStruct((n, d), BF16),
        compiler_params=_cparams(("parallel",)),
        name="rmsnorm_cast",
    )(x, g.reshape(1, d))


def _mm_kernel(x_ref, w_ref, o_ref):
    o_ref[...] = _dot(x_ref[...], w_ref[...]).astype(o_ref.dtype)


def _mm_acc_kernel(x_ref, w_ref, o_ref, acc_ref, *, nk):
    k = pl.program_id(2)

    @pl.when(k == 0)
    def _():
        acc_ref[...] = jnp.zeros_like(acc_ref)

    acc_ref[...] += _dot(x_ref[...], w_ref[...])

    @pl.when(k == nk - 1)
    def _():
        o_ref[...] = acc_ref[...].astype(o_ref.dtype)


def matmul(x, w, tm, tn, out_dtype=BF16, name="matmul"):
    m, kdim = x.shape
    n = w.shape[1]
    assert m % tm == 0 and n % tn == 0
    return pl.pallas_call(
        _mm_kernel,
        grid=(m // tm, n // tn),
        in_specs=[pl.BlockSpec((tm, kdim), lambda i, j: (i, 0)),
                  pl.BlockSpec((kdim, tn), lambda i, j: (0, j))],
        out_specs=pl.BlockSpec((tm, tn), lambda i, j: (i, j)),
        out_shape=jax.ShapeDtypeStruct((m, n), out_dtype),
        compiler_params=_cparams(("parallel", "parallel")),
        name=name,
    )(x, w)


def matmul_ksplit(x, w, layer, tm, tn, tk, out_dtype=BF16, name="matmul_ksplit"):
    m, kdim = x.shape
    n = w.shape[2]
    nk = kdim // tk
    assert m % tm == 0 and n % tn == 0 and kdim % tk == 0
    return pl.pallas_call(
        functools.partial(_mm_acc_kernel, nk=nk),
        grid=(m // tm, n // tn, nk),
        in_specs=[pl.BlockSpec((tm, tk), lambda i, j, k: (i, k)),
                  pl.BlockSpec((None, tk, tn), lambda i, j, k: (layer, k, j))],
        out_specs=pl.BlockSpec((tm, tn), lambda i, j, k: (i, j)),
        out_shape=jax.ShapeDtypeStruct((m, n), out_dtype),
        scratch_shapes=[pltpu.VMEM((tm, tn), F32)],
        compiler_params=_cparams(("parallel", "parallel", "arbitrary")),
        name=name,
    )(x, w)


def _mm_wstat_kernel(x_ref, w_ref, o_ref, wb_ref):
    @pl.when(pl.program_id(1) == 0)
    def _():
        wb_ref[...] = w_ref[...].astype(BF16)

    o_ref[...] = _dot(x_ref[...], wb_ref[...]).astype(o_ref.dtype)


def matmul_wstat(x, w, layer, tm, tn, out_dtype=BF16, name="matmul_wstat"):
    m, kdim = x.shape
    n = w.shape[2]
    assert m % tm == 0 and n % tn == 0
    return pl.pallas_call(
        _mm_wstat_kernel,
        grid=(n // tn, m // tm),
        in_specs=[pl.BlockSpec((tm, kdim), lambda j, i: (i, 0)),
                  pl.BlockSpec((None, kdim, tn), lambda j, i: (layer, 0, j))],
        out_specs=pl.BlockSpec((tm, tn), lambda j, i: (i, j)),
        out_shape=jax.ShapeDtypeStruct((m, n), out_dtype),
        scratch_shapes=[pltpu.VMEM((kdim, tn), BF16)],
        compiler_params=_cparams(("parallel", "arbitrary")),
        name=name,
    )(x, w)


def _dsa_prep_kernel(ckv_ref, kw_ref, kvg_ref, wuk_ref, wuv_ref, lng_ref, lnb_ref,
                     k_ref, v_ref, kidx_ref, wi_ref, *, k_scale, w_scale):
    c = ckv_ref[...].astype(F32)
    c = c * lax.rsqrt(jnp.mean(c * c, axis=-1, keepdims=True) + NORM_EPS) * kvg_ref[...]
    cb = c.astype(BF16)
    k_ref[...] = (_dot(cb, wuk_ref[...]) * k_scale).astype(k_ref.dtype)
    v_ref[:, :HEAD_DIM] = _dot(cb, wuv_ref[...]).astype(v_ref.dtype)
    v_ref[:, HEAD_DIM:] = jnp.ones((v_ref.shape[0], HEAD_DIM), v_ref.dtype)
    kw = kw_ref[...].astype(F32)
    ki = kw[:, :HEAD_DIM]
    mu = jnp.mean(ki, axis=-1, keepdims=True)
    kc = ki - mu
    kn = kc * lax.rsqrt(jnp.mean(kc * kc, axis=-1, keepdims=True) + NORM_EPS)
    kidx_ref[...] = (kn * lng_ref[...] + lnb_ref[...]).astype(kidx_ref.dtype)
    wi_ref[...] = kw[:, HEAD_DIM:] * w_scale


def dsa_prep(proj, lay, kv_norm_g, w_uk, w_uv, ln_g, ln_b, idx_heads, rows=512):
    n = proj.shape[0]
    kvl = w_uk.shape[0]
    row = lambda i: (i, 0)
    const = lambda i: (0, 0)
    out = jax.ShapeDtypeStruct((n, HEAD_DIM), BF16)
    return pl.pallas_call(
        functools.partial(_dsa_prep_kernel, k_scale=HEAD_DIM ** -0.5 * LOG2E,
                          w_scale=idx_heads ** -0.5 * HEAD_DIM ** -0.5),
        grid=(n // rows,),
        in_specs=[pl.BlockSpec((rows, kvl), lambda i: (i, lay["c_kv"] // kvl)),
                  pl.BlockSpec((rows, 2 * HEAD_DIM), lambda i: (i, lay["k_i"] // (2 * HEAD_DIM))),
                  pl.BlockSpec((1, kvl), const),
                  pl.BlockSpec((kvl, HEAD_DIM), const),
                  pl.BlockSpec((kvl, HEAD_DIM), const),
                  pl.BlockSpec((1, HEAD_DIM), const),
                  pl.BlockSpec((1, HEAD_DIM), const)],
        out_specs=[pl.BlockSpec((rows, HEAD_DIM), row), pl.BlockSpec((rows, 2 * HEAD_DIM), row),
                   pl.BlockSpec((rows, HEAD_DIM), row), pl.BlockSpec((rows, HEAD_DIM), row)],
        out_shape=[out, jax.ShapeDtypeStruct((n, 2 * HEAD_DIM), BF16), out,
                   jax.ShapeDtypeStruct((n, HEAD_DIM), F32)],
        compiler_params=_cparams(("parallel",)),
        name="dsa_prep",
    )(proj, proj, kv_norm_g.reshape(1, kvl), w_uk.astype(BF16), w_uv.astype(BF16),
      ln_g.reshape(1, HEAD_DIM), ln_b.reshape(1, HEAD_DIM))


def _sortable_key(x):
    u = lax.bitcast_convert_type(x, jnp.int32)
    return u ^ ((u >> 31) & jnp.int32(0x7FFFFFFF))


def _dsa_kernel(*refs, heads, idx_heads, topk, n_qi):
    qa_ref = refs[0]
    qi_refs = refs[1:1 + n_qi]
    (wi_ref, kidx_ref, k_ref, v_ref, bias_ref, o_ref,
     qas_ref, qis_ref, keys_ref, mask_ref, m_ref, acc_ref) = refs[1 + n_qi:]
    qb = ATT_BLOCK
    ck = ATT_BLOCK
    j = pl.program_id(1)

    for h in range(heads):
        qas_ref[h * qb:(h + 1) * qb, :] = qa_ref[:, h * HEAD_DIM:(h + 1) * HEAD_DIM]
    for h in range(idx_heads):
        src, off = divmod(h * HEAD_DIM, QI_BLOCK)
        qis_ref[h * qb:(h + 1) * qb, :] = qi_refs[src][:, off:off + HEAD_DIM]

    row_pos = j * qb + lax.broadcasted_iota(jnp.int32, (qb, ck), 0)
    lane = lax.broadcasted_iota(jnp.int32, (qb, ck), 1)
    hg = 8 if idx_heads % 8 == 0 else idx_heads

    def score_chunk(c, carry):
        kc = kidx_ref[pl.ds(pl.multiple_of(c * ck, ck), ck), :]
        acc = jnp.zeros((qb, ck), F32)
        for g in range(idx_heads // hg):
            s = _dot_nt(qis_ref[g * hg * qb:(g + 1) * hg * qb, :], kc)
            for hh in range(hg):
                h = g * hg + hh
                acc = acc + jnp.maximum(s[hh * qb:(hh + 1) * qb, :], 0.0) * wi_ref[:, h:h + 1]
        valid = (c * ck + lane) <= row_pos
        keys_ref[c] = jnp.where(valid, _sortable_key(acc), jnp.int32(INT_MIN))
        return carry

    lax.fori_loop(0, j + 1, score_chunk, 0)

    def bit_step(i, lo):
        cand = lo + lax.shift_left(jnp.int32(1), jnp.int32(31) - i)

        def count_chunk(c, part):
            ge = (keys_ref[c] >= cand).astype(jnp.int32)
            for t in range(ck // LANES):
                part = part + ge[:, t * LANES:(t + 1) * LANES]
            return part

        part = lax.fori_loop(0, j + 1, count_chunk, jnp.zeros((qb, LANES), jnp.int32))
        cnt = jnp.sum(part, axis=-1, keepdims=True)
        return jnp.where(cnt >= topk, cand, lo)

    lo = lax.fori_loop(0, 32, bit_step, jnp.full((qb, 1), INT_MIN, jnp.int32))
    thr = jnp.maximum(lo, jnp.int32(INT_MIN + 1))

    m_ref[...] = jnp.full_like(m_ref, M_INIT)
    acc_ref[...] = jnp.zeros_like(acc_ref)

    def attend(c, bias_idx):
        off = pl.multiple_of(c * ck, ck)
        kc = k_ref[pl.ds(off, ck), :]
        vc = v_ref[pl.ds(off, ck), :]
        mask_ref[...] = jnp.where(keys_ref[c] >= thr, 0.0, MASK_NEG)
        for h in range(heads):
            rows = slice(h * qb, (h + 1) * qb)
            s = _dot_nt(qas_ref[rows, :], kc) + mask_ref[...]
            if bias_idx is not None:
                s = s + bias_ref[bias_idx, rows, :]
            parts = [s[:, t * LANES:(t + 1) * LANES] for t in range(ck // LANES)]
            smax = functools.reduce(jnp.maximum, parts)
            m_prev = m_ref[rows, :]
            m_new = jnp.maximum(m_prev, jnp.max(smax, axis=-1, keepdims=True))
            p = jnp.concatenate([jnp.exp2(x - m_new) for x in parts], axis=1).astype(BF16)
            alpha = jnp.exp2(m_prev - m_new)
            pv = _dot(p, vc)
            acc_ref[rows, :HEAD_DIM] = alpha * acc_ref[rows, :HEAD_DIM] + pv[:, :HEAD_DIM]
            acc_ref[rows, HEAD_DIM:] = alpha * acc_ref[rows, HEAD_DIM:] + pv[:, HEAD_DIM:]
            m_ref[rows, :] = m_new

    def far_chunk(c, carry):
        attend(c, None)
        return carry

    lax.fori_loop(0, j - 1, far_chunk, 0)

    @pl.when(j >= 1)
    def _():
        attend(j - 1, 1)

    attend(j, 0)

    for h in range(heads):
        rows = slice(h * qb, (h + 1) * qb)
        o = acc_ref[rows, :HEAD_DIM] / acc_ref[rows, HEAD_DIM:]
        o_ref[:, h * HEAD_DIM:(h + 1) * HEAD_DIM] = o.astype(o_ref.dtype)


def dsa_attention(proj, lay, wi, kidx, k, v, bias_tiles, batch, seq, heads, idx_heads):
    qb = ATT_BLOCK
    nqb = seq // qb
    aw = heads * HEAD_DIM
    iw = idx_heads * HEAD_DIM
    topk = min(IDX_TOPK, seq // 4)
    assert lay["q_i"] % QI_BLOCK == 0 and iw % QI_BLOCK == 0 and lay["q_a"] % aw == 0
    n_qi = iw // QI_BLOCK
    return pl.pallas_call(
        functools.partial(_dsa_kernel, heads=heads, idx_heads=idx_heads, topk=topk, n_qi=n_qi),
        grid=(batch, nqb),
        in_specs=[pl.BlockSpec((qb, aw), lambda b, j: (b * nqb + j, lay["q_a"] // aw)),
                  *[pl.BlockSpec((qb, QI_BLOCK), lambda b, j, t=t: (b * nqb + j, lay["q_i"] // QI_BLOCK + t))
                    for t in range(n_qi)],
                  pl.BlockSpec((qb, HEAD_DIM), lambda b, j: (b * nqb + j, 0)),
                  pl.BlockSpec((seq, HEAD_DIM), lambda b, j: (b, 0)),
                  pl.BlockSpec((seq, HEAD_DIM), lambda b, j: (b, 0)),
                  pl.BlockSpec((seq, 2 * HEAD_DIM), lambda b, j: (b, 0)),
                  pl.BlockSpec((2, heads * qb, qb), lambda b, j: (0, 0, 0),
                               pipeline_mode=pl.Buffered(1))],
        out_specs=pl.BlockSpec((qb, aw), lambda b, j: (b * nqb + j, 0)),
        out_shape=jax.ShapeDtypeStruct((batch * seq, aw), BF16),
        scratch_shapes=[pltpu.VMEM((heads * qb, HEAD_DIM), BF16),
                        pltpu.VMEM((idx_heads * qb, HEAD_DIM), BF16),
                        pltpu.VMEM((seq // qb, qb, qb), jnp.int32),
                        pltpu.VMEM((qb, qb), F32),
                        pltpu.VMEM((heads * qb, LANES), F32),
                        pltpu.VMEM((heads * qb, 2 * HEAD_DIM), F32)],
        compiler_params=_cparams(("parallel", "arbitrary")),
        name="dsa_attention",
    )(proj, *([proj] * n_qi), wi, kidx, k, v, bias_tiles)


def _t5_bucket(dist):
    max_exact = REL_BUCKETS // 2
    d = jnp.maximum(dist, 0)
    df = jnp.maximum(d, 1).astype(F32)
    large = max_exact + (jnp.log(df / max_exact) / math.log(REL_MAX_DIST / max_exact)
                         * (REL_BUCKETS - max_exact)).astype(jnp.int32)
    large = jnp.minimum(large, REL_BUCKETS - 1)
    return jnp.where(d < max_exact, d, large)


def rel_bias_tiles(rel_bias, heads):
    qb = ATT_BLOCK
    assert _static_far_bucket(qb) == REL_BUCKETS - 1
    tbl = (rel_bias[_t5_bucket(jnp.arange(4 * qb))] - rel_bias[REL_BUCKETS - 1][None, :]).T * LOG2E
    y = jnp.arange(2 * qb)
    tiles = []
    for off in (0, qb):
        d = off + jnp.where(y < qb, -y, 2 * qb - y)
        u = tbl[:, jnp.maximum(d, 0)]
        flat = jnp.broadcast_to(u[:, None, :], (heads, qb, 2 * qb)).reshape(heads, 2 * qb * qb)
        skew = flat[:, :qb * (2 * qb - 1)].reshape(heads, qb, 2 * qb - 1)
        tiles.append(skew[:, :, :qb].reshape(heads * qb, qb))
    return jnp.stack(tiles).astype(F32)


def _static_far_bucket(d):
    max_exact = REL_BUCKETS // 2
    v = max_exact + int(math.log(d / max_exact) / math.log(REL_MAX_DIST / max_exact)
                        * (REL_BUCKETS - max_exact))
    return min(v, REL_BUCKETS - 1)


def _hgrn_kernel(q_ref, f_ref, i_ref, g_ref, lb_ref, ng_ref, o_ref,
                 st_ref, a_ref, k_ref, qf_ref, p_ref, *, heads_per_step, tb):
    c = HGRN_CHUNK
    sub = HGRN_SUB
    nsub = c // sub
    hs = range(heads_per_step)

    @pl.when(pl.program_id(2) == 0)
    def _():
        st_ref[...] = jnp.zeros_like(st_ref)

    r_io = lax.broadcasted_iota(jnp.int32, (c, c), 0)
    c_io = lax.broadcasted_iota(jnp.int32, (c, c), 1)
    tri = (c_io <= r_io).astype(F32)
    sub_r = lax.broadcasted_iota(jnp.int32, (sub, c), 0)
    sub_c = lax.broadcasted_iota(jnp.int32, (sub, c), 1)
    ng = ng_ref[...]

    def a_start(hh, sb):
        return a_ref[hh, sb * sub - 1:sb * sub, :] if sb > 0 else jnp.zeros((1, HEAD_DIM), F32)

    def chunk(ci, carry):
        r0 = pl.multiple_of(ci * c, c)

        worst = jnp.zeros((1, HEAD_DIM), F32)
        for hh in hs:
            cols = slice(hh * HEAD_DIM, (hh + 1) * HEAD_DIM)
            lb = lb_ref[:, cols]
            z = f_ref[pl.ds(r0, c), cols].astype(F32)
            qx = q_ref[pl.ds(r0, c), cols].astype(F32)
            sig = jax.nn.sigmoid(z)
            lf = jnp.log2(lb + (1.0 - lb) * sig)
            k_ref[hh] = (1.0 - lb) * jax.nn.sigmoid(-z)
            qf_ref[hh] = qx * jax.nn.sigmoid(qx)
            a = jnp.dot(tri, lf, preferred_element_type=F32, precision=lax.Precision.HIGHEST)
            a_ref[hh] = a
            for sb in range(nsub):
                top = a[sb * sub - 1:sb * sub, :] if sb > 0 else jnp.zeros((1, HEAD_DIM), F32)
                worst = jnp.maximum(worst, top - a[(sb + 1) * sub - 1:(sb + 1) * sub, :])
        safe = jnp.max(worst) <= HGRN_SAFE_LOG2

        @pl.when(safe)
        def _():
            for hh in hs:
                a = a_ref[hh]
                kf = k_ref[hh]
                for sb in range(nsub):
                    rs = slice(sb * sub, (sb + 1) * sub)
                    top = a_start(hh, sb)
                    q_t = qf_ref[hh, rs, :] * jnp.exp2(a_ref[hh, rs, :] - top)
                    k_t = kf * jnp.exp2(top - a)
                    p = _dot_nt(q_t.astype(BF16), k_t.astype(BF16))
                    p_ref[hh, rs, :] = jnp.where(sub_c <= sub_r + sb * sub, p, 0.0)

        @pl.when(jnp.logical_not(safe))
        def _():
            for hh in hs:
                a = a_ref[hh]
                kf = k_ref[hh]
                for sb in range(nsub):
                    rs = slice(sb * sub, (sb + 1) * sub)
                    top = a_start(hh, sb)
                    a_blk = a_ref[hh, rs, :]
                    q_blk = qf_ref[hh, rs, :]
                    q_t = q_blk * jnp.exp2(a_blk - top)
                    k_t = kf * jnp.exp2(top - a)
                    p_off = _dot_nt(q_t.astype(BF16), k_t.astype(BF16))
                    p_diag = jnp.zeros((sub, c), F32)
                    for s in range(sub):
                        row = sb * sub + s
                        e = jnp.exp2(a_blk - a_ref[hh, row:row + 1, :])
                        col = jnp.sum(q_blk * e * k_ref[hh, row:row + 1, :], axis=-1, keepdims=True)
                        p_diag = jnp.where((sub_c == row) & (sub_r >= s), col, p_diag)
                    p_ref[hh, rs, :] = jnp.where(sub_c < sb * sub, p_off, p_diag)

        for hh in hs:
            cols = slice(hh * HEAD_DIM, (hh + 1) * HEAD_DIM)
            a = a_ref[hh]
            kf = k_ref[hh]
            v = i_ref[pl.ds(r0, c), cols].astype(F32)
            gx = g_ref[pl.ds(r0, c), cols].astype(F32)
            vb = v.astype(BF16)
            st = st_ref[hh]
            o = (_dot(p_ref[hh].astype(BF16), vb)
                 + _dot_nt((qf_ref[hh] * jnp.exp2(a)).astype(BF16), st.astype(BF16)))
            a_last = a[c - 1:c, :]
            k_end = kf * jnp.exp2(a_last - a)
            st_ref[hh] = st * jnp.exp2(a_last) + _dot(v.T.astype(BF16), k_end.astype(BF16))
            o = o * lax.rsqrt(jnp.mean(o * o, axis=-1, keepdims=True) + NORM_EPS) * ng
            o_ref[pl.ds(r0, c), cols] = (o * (gx * jax.nn.sigmoid(gx))).astype(o_ref.dtype)
        return carry

    lax.fori_loop(0, tb // c, chunk, 0)


def hgrn2(proj, lay, lb, norm_g, batch, seq, heads, heads_per_step=8, tb=512):
    heads_per_step = min(heads_per_step, heads)
    assert heads % heads_per_step == 0
    hw = heads_per_step * HEAD_DIM
    nt = seq // tb
    ng = heads // heads_per_step

    def col(name):
        base = lay[name] // hw
        return lambda b, h, t: (b * nt + t, base + h)

    return pl.pallas_call(
        functools.partial(_hgrn_kernel, heads_per_step=heads_per_step, tb=tb),
        grid=(batch, ng, nt),
        in_specs=[pl.BlockSpec((tb, hw), col("q_h")),
                  pl.BlockSpec((tb, hw), col("f_h")),
                  pl.BlockSpec((tb, hw), col("i_h")),
                  pl.BlockSpec((tb, hw), col("g_h")),
                  pl.BlockSpec((1, hw), lambda b, h, t: (0, h)),
                  pl.BlockSpec((1, HEAD_DIM), lambda b, h, t: (0, 0))],
        out_specs=pl.BlockSpec((tb, hw), lambda b, h, t: (b * nt + t, h)),
        out_shape=jax.ShapeDtypeStruct((batch * seq, heads * HEAD_DIM), BF16),
        scratch_shapes=[pltpu.VMEM((heads_per_step, HEAD_DIM, HEAD_DIM), F32),
                        pltpu.VMEM((heads_per_step, HGRN_CHUNK, HEAD_DIM), F32),
                        pltpu.VMEM((heads_per_step, HGRN_CHUNK, HEAD_DIM), F32),
                        pltpu.VMEM((heads_per_step, HGRN_CHUNK, HEAD_DIM), F32),
                        pltpu.VMEM((heads_per_step, HGRN_CHUNK, HGRN_CHUNK), F32)],
        compiler_params=_cparams(("parallel", "parallel", "arbitrary")),
        name="hgrn2",
    )(proj, proj, proj, proj, lb.reshape(1, -1), norm_g.reshape(1, HEAD_DIM))


def _merge_kernel(a_ref, b_ref, wa_ref, wb_ref, ga_ref, gb_ref, o_ref, wab_ref, wbb_ref):
    @pl.when(pl.program_id(1) == 0)
    def _():
        wab_ref[...] = wa_ref[...].astype(BF16)
        wbb_ref[...] = wb_ref[...].astype(BF16)

    ya = _dot(a_ref[...], wab_ref[...])
    yb = _dot(b_ref[...], wbb_ref[...])
    ga = jax.nn.sigmoid(ga_ref[...].astype(F32))
    gb = jax.nn.sigmoid(gb_ref[...].astype(F32))
    o_ref[...] = (ga * ya + gb * yb).astype(o_ref.dtype)


def gated_merge(attn, hg, w_pa, w_pb, layer, proj, lay, tm=1024, tn=512):
    m, ka = attn.shape
    kb = hg.shape[1]
    d = w_pa.shape[2]
    assert lay["gate_a"] % tn == 0 and lay["gate_b"] % tn == 0
    ga0, gb0 = lay["gate_a"] // tn, lay["gate_b"] // tn
    return pl.pallas_call(
        _merge_kernel,
        grid=(d // tn, m // tm),
        in_specs=[pl.BlockSpec((tm, ka), lambda j, i: (i, 0)),
                  pl.BlockSpec((tm, kb), lambda j, i: (i, 0)),
                  pl.BlockSpec((None, ka, tn), lambda j, i: (layer, 0, j)),
                  pl.BlockSpec((None, kb, tn), lambda j, i: (layer, 0, j)),
                  pl.BlockSpec((tm, tn), lambda j, i: (i, ga0 + j)),
                  pl.BlockSpec((tm, tn), lambda j, i: (i, gb0 + j))],
        out_specs=pl.BlockSpec((tm, tn), lambda j, i: (i, j)),
        out_shape=jax.ShapeDtypeStruct((m, d), BF16),
        scratch_shapes=[pltpu.VMEM((ka, tn), BF16), pltpu.VMEM((kb, tn), BF16)],
        compiler_params=_cparams(("parallel", "arbitrary")),
        name="gated_merge",
    )(attn, hg, w_pa, w_pb, proj, proj)


def _norm_residual_kernel(x_ref, y_ref, gp_ref, gn_ref, xo_ref, ho_ref):
    y = y_ref[...].astype(F32)
    yn = y * lax.rsqrt(jnp.mean(y * y, axis=-1, keepdims=True) + NORM_EPS) * gp_ref[...]
    x = x_ref[...] + yn
    xo_ref[...] = x
    ho_ref[...] = (x * lax.rsqrt(jnp.mean(x * x, axis=-1, keepdims=True) + NORM_EPS)
                   * gn_ref[...]).astype(ho_ref.dtype)


def _norm_residual_last_kernel(x_ref, y_ref, gp_ref, xo_ref):
    y = y_ref[...].astype(F32)
    xo_ref[...] = x_ref[...] + y * lax.rsqrt(jnp.mean(y * y, axis=-1, keepdims=True) + NORM_EPS) * gp_ref[...]


def norm_residual(x, y, g_post, g_next=None, rows=256):
    n, d = x.shape
    row = pl.BlockSpec((rows, d), lambda i: (i, 0))
    vec = pl.BlockSpec((1, d), lambda i: (0, 0))
    if g_next is None:
        return pl.pallas_call(
            _norm_residual_last_kernel,
            grid=(n // rows,),
            in_specs=[row, row, vec],
            out_specs=row,
            out_shape=jax.ShapeDtypeStruct((n, d), F32),
            compiler_params=_cparams(("parallel",)),
            name="norm_residual_last",
        )(x, y, g_post.reshape(1, d)), None
    return pl.pallas_call(
        _norm_residual_kernel,
        grid=(n // rows,),
        in_specs=[row, row, vec, vec],
        out_specs=[row, row],
        out_shape=[jax.ShapeDtypeStruct((n, d), F32), jax.ShapeDtypeStruct((n, d), BF16)],
        compiler_params=_cparams(("parallel",)),
        name="norm_residual",
    )(x, y, g_post.reshape(1, d), g_next.reshape(1, d))


GELU_C = math.sqrt(2.0 / math.pi)


def _gelu_tanh(x):
    z = x * (x * x * (-2.0 * GELU_C * 0.044715 * LOG2E) + (-2.0 * GELU_C * LOG2E))
    return x / (1.0 + jnp.exp2(z))


def _ffn_up_kernel(x_ref, wg_ref, wv_ref, cwg_ref, cwv_ref, cbg_ref, cbv_ref, o_ref,
                   wb_ref, u_ref, *, tm, tn, seq):
    i = pl.program_id(1)
    starts_seq = (i * tm) % seq == 0

    @pl.when(i == 0)
    def _():
        wb_ref[:, :tn] = wg_ref[...].astype(BF16)
        wb_ref[:, tn:] = wv_ref[...].astype(BF16)

    @pl.when(starts_seq)
    def _():
        u_ref[0:8, :] = jnp.zeros((8, 2 * tn), F32)

    @pl.when(jnp.logical_not(starts_seq))
    def _():
        u_ref[0:8, :] = u_ref[tm:tm + 8, :]

    u_ref[8:8 + tm, :] = _dot(x_ref[...], wb_ref[...])

    def conv(lo, w_ref, b_ref):
        cols = slice(lo, lo + tn)
        return (b_ref[...] + w_ref[0:1, :] * u_ref[6:6 + tm, cols]
                + w_ref[1:2, :] * u_ref[7:7 + tm, cols] + w_ref[2:3, :] * u_ref[8:8 + tm, cols])

    gate = conv(0, cwg_ref, cbg_ref)
    val = conv(tn, cwv_ref, cbv_ref)
    o_ref[...] = (_gelu_tanh(gate) * val).astype(o_ref.dtype)


def ffn_up_conv_geglu(h, w_up, conv_w, conv_b, layer, seq, tm, tn):
    m, kdim = h.shape
    dff = w_up.shape[2] // 2
    assert m % tm == 0 and dff % tn == 0 and seq % tm == 0
    nc = dff // tn
    wspec = lambda off: pl.BlockSpec((None, kdim, tn), lambda j, i: (layer, 0, off + j))
    cspec = lambda r, off: pl.BlockSpec((None, r, tn), lambda j, i: (layer, 0, off + j))
    return pl.pallas_call(
        functools.partial(_ffn_up_kernel, tm=tm, tn=tn, seq=seq),
        grid=(nc, m // tm),
        in_specs=[pl.BlockSpec((tm, kdim), lambda j, i: (i, 0)),
                  wspec(0), wspec(nc),
                  cspec(CONV_WIDTH, 0), cspec(CONV_WIDTH, nc),
                  cspec(1, 0), cspec(1, nc)],
        out_specs=pl.BlockSpec((tm, tn), lambda j, i: (i, j)),
        out_shape=jax.ShapeDtypeStruct((m, dff), BF16),
        scratch_shapes=[pltpu.VMEM((kdim, 2 * tn), BF16), pltpu.VMEM((tm + 8, 2 * tn), F32)],
        compiler_params=_cparams(("parallel", "arbitrary")),
        name="ffn_up",
    )(h, w_up, w_up, conv_w, conv_w, conv_b[:, None, :], conv_b[:, None, :])


def _in_layout(d, aw, iw, ih, hk, hv, kvl):
    segs = [("q_a", aw), ("c_kv", kvl), ("q_i", iw), ("k_i", HEAD_DIM), ("w_i", ih), ("pad", None),
            ("q_h", hk), ("f_h", hk), ("i_h", hv), ("g_h", hv), ("gate_a", d), ("gate_b", d)]
    lay, off = {}, 0
    for name, w in segs:
        if name == "pad":
            w = (-off) % IN_PAD_ALIGN
            lay["pad_width"] = w
        lay[name] = off
        off += w
    assert off % IN_PROJ_TN == 0
    lay["total"] = off
    return lay


def _pad_cast_kernel(src_ref, o_ref, prev_ref, *, n_pad, n_tail, off, keep):
    n = pl.program_id(0)
    tn = o_ref.shape[1]

    @pl.when(n < n_pad)
    def _():
        o_ref[...] = src_ref[...].T.astype(BF16)

    @pl.when(n == n_pad)
    def _():
        row = lax.broadcasted_iota(jnp.int32, (tn, 1), 0)
        o_ref[...] = jnp.where(row < keep, src_ref[...], 0.0).T.astype(BF16)
        prev_ref[...] = src_ref[...]

    @pl.when((n > n_pad) & (n < n_tail))
    def _():
        o_ref[...] = jnp.zeros_like(o_ref)

    @pl.when(n >= n_tail)
    def _():
        o_ref[:, :tn - off] = prev_ref[off:, :].T.astype(BF16)
        o_ref[:, tn - off:] = src_ref[:off, :].T.astype(BF16)
        prev_ref[...] = src_ref[...]


def pad_cast_w_in(w_in_t, layer, lay, ih):
    n_src, kdim = w_in_t.shape[1], w_in_t.shape[2]
    tn = IN_PROJ_TN
    assert lay["k_i"] % tn == 0 and lay["q_h"] % tn == 0 and ih % 8 == 0
    n_pad = lay["k_i"] // tn
    n_tail = lay["q_h"] // tn
    back = -(-lay["pad_width"] // tn)
    off = back * tn - lay["pad_width"]
    assert 0 < off < tn and off % 8 == 0 and n_tail - back == n_pad
    last = (n_src - 1) // tn

    def index(n):
        blk = jnp.where(n <= n_pad, n, n - back + 1)
        return (layer, jnp.clip(blk, 0, last), 0)

    return pl.pallas_call(
        functools.partial(_pad_cast_kernel, n_pad=n_pad, n_tail=n_tail, off=off, keep=HEAD_DIM + ih),
        grid=(lay["total"] // tn,),
        in_specs=[pl.BlockSpec((None, tn, kdim), index)],
        out_specs=pl.BlockSpec((kdim, tn), lambda n: (0, n)),
        out_shape=jax.ShapeDtypeStruct((kdim, lay["total"]), BF16),
        scratch_shapes=[pltpu.VMEM((tn, kdim), F32)],
        compiler_params=_cparams(("arbitrary",)),
        name="pad_cast_w_in",
    )(w_in_t)


def _pick(n, cands):
    for c in cands:
        if n % c == 0:
            return c
    raise ValueError(f"no tile for {n}")


def kernel(x, rel_bias, hgrn_lb_logits, mix_pre_g, mix_post_g, w_in, kv_norm_g, w_uk, w_uv, idx_k_ln_g, idx_k_ln_b, hgrn_norm_g, w_proj_attn, w_proj_hgrn, w_out, ffn_pre_g, ffn_post_g, w_up, conv_w, conv_b, w_down):
    batch, seq, d = x.shape
    depth = w_in.shape[0]
    aw = w_proj_attn.shape[1]
    hv = w_proj_hgrn.shape[1]
    hk = hgrn_lb_logits.shape[1]
    kvl = w_uk.shape[1]
    dff = w_down.shape[1]
    heads = aw // HEAD_DIM
    hheads = hv // HEAD_DIM
    ih = w_in.shape[2] - (aw + kvl + HEAD_DIM + 2 * hk + 2 * hv + 2 * d)
    ih = ih // (HEAD_DIM + 1)
    iw = ih * HEAD_DIM
    lay = _in_layout(d, aw, iw, ih, hk, hv, kvl)
    n = batch * seq

    lb_p = jax.nn.softmax(hgrn_lb_logits.astype(F32), axis=0)
    lb_all = jnp.cumsum(lb_p, axis=0) - lb_p[0:1]
    bias_tiles = rel_bias_tiles(rel_bias, heads)

    tm = _pick(n, (1024, 512, 256))
    w_in_t = jnp.swapaxes(w_in, 1, 2)
    w_down_b = w_down.astype(BF16)
    xf = x.reshape(n, d)
    h = rmsnorm_cast(xf, mix_pre_g[0])
    for l in range(depth):
        w_in_l = pad_cast_w_in(w_in_t, l, lay, ih)
        proj = matmul(h, w_in_l, _pick(n, (2 * tm, tm)), IN_PROJ_TN, name="in_proj")
        k, v, kidx, wi = dsa_prep(proj, lay, kv_norm_g[l], w_uk[l], w_uv[l],
                                  idx_k_ln_g[l], idx_k_ln_b[l], ih)
        attn = dsa_attention(proj, lay, wi, kidx, k, v, bias_tiles, batch, seq, heads, ih)
        hg = hgrn2(proj, lay, lb_all[l], hgrn_norm_g[l], batch, seq, hheads)
        merged = gated_merge(attn, hg, w_proj_attn, w_proj_hgrn, l, proj, lay, tm=tm)
        y = matmul_wstat(merged, w_out, l, tm, 512, name="out_proj")
        xf, h = norm_residual(xf, y, mix_post_g[l], ffn_pre_g[l])
        act = ffn_up_conv_geglu(h, w_up, conv_w, conv_b, l, seq, tm, _pick(dff, (256, 128)))
        y = matmul_ksplit(act, w_down_b, l, tm, 512, _pick(dff, (dff // 2, dff)), name="ffn_down")
        xf, h = norm_residual(xf, y, ffn_post_g[l], mix_pre_g[l + 1] if l + 1 < depth else None)
    return xf.reshape(batch, seq, d)
```

```python
import functools
import math

import jax
import jax.numpy as jnp
from jax import lax
from jax.experimental import pallas as pl
from jax.experimental.pallas import tpu as pltpu

F32 = jnp.float32
BF16 = jnp.bfloat16

NORM_EPS = 1e-6
HEAD_DIM = 128
IDX_TOPK = 256
REL_BUCKETS = 32
REL_MAX_DIST = 128
CONV_WIDTH = 3
LANES = 128
VMEM_LIMIT_BYTES = 56 * 1024 * 1024

ATT_BLOCK = 256
QI_BLOCK = 512
HGRN_CHUNK = 64
HGRN_SUB = 16
HGRN_SAFE_LOG2 = 100.0
IN_PROJ_TN = 512
IN_PAD_ALIGN = 1024
INT_MIN = -2 ** 31
MASK_NEG = -1e30
M_INIT = -1e20
LOG2E = math.log2(math.e)


def _cparams(sem):
    return pltpu.CompilerParams(dimension_semantics=sem, vmem_limit_bytes=VMEM_LIMIT_BYTES)


def _dot(a, b):
    return jnp.dot(a, b, preferred_element_type=F32)


def _dot_nt(a, b):
    return lax.dot_general(a, b, (((1,), (1,)), ((), ())), preferred_element_type=F32)


def _rmsnorm_kernel(x_ref, g_ref, o_ref):
    x = x_ref[...]
    r = lax.rsqrt(jnp.mean(x * x, axis=-1, keepdims=True) + NORM_EPS)
    o_ref[...] = (x * r * g_ref[...]).astype(o_ref.dtype)


def rmsnorm_cast(x, g, rows=256):
    n, d = x.shape
    return pl.pallas_call(
        _rmsnorm_kernel,
        grid=(n // rows,),
        in_specs=[pl.BlockSpec((rows, d), lambda i: (i, 0)),
                  pl.BlockSpec((1, d), lambda i: (0, 0))],
        out_specs=pl.BlockSpec((rows, d), lambda i: (i, 0)),
        out_shape=jax.ShapeDtypeStruct((n, d), BF16),
        compiler_params=_cparams(("parallel",)),
        name="rmsnorm_cast",
    )(x, g.reshape(1, d))


def _mm_kernel(x_ref, w_ref, o_ref):
    o_ref[...] = _dot(x_ref[...], w_ref[...]).astype(o_ref.dtype)


def _mm_acc_kernel(x_ref, w_ref, o_ref, acc_ref, *, nk):
    k = pl.program_id(2)

    @pl.when(k == 0)
    def _():
        acc_ref[...] = jnp.zeros_like(acc_ref)

    acc_ref[...] += _dot(x_ref[...], w_ref[...])

    @pl.when(k == nk - 1)
    def _():
        o_ref[...] = acc_ref[...].astype(o_ref.dtype)


def matmul(x, w, tm, tn, out_dtype=BF16, name="matmul"):
    m, kdim = x.shape
    n = w.shape[1]
    assert m % tm == 0 and n % tn == 0
    return pl.pallas_call(
        _mm_kernel,
        grid=(m // tm, n // tn),
        in_specs=[pl.BlockSpec((tm, kdim), lambda i, j: (i, 0)),
                  pl.BlockSpec((kdim, tn), lambda i, j: (0, j))],
        out_specs=pl.BlockSpec((tm, tn), lambda i, j: (i, j)),
        out_shape=jax.ShapeDtypeStruct((m, n), out_dtype),
        compiler_params=_cparams(("parallel", "parallel")),
        name=name,
    )(x, w)


def matmul_ksplit(x, w, layer, tm, tn, tk, out_dtype=BF16, name="matmul_ksplit"):
    m, kdim = x.shape
    n = w.shape[2]
    nk = kdim // tk
    assert m % tm == 0 and n % tn == 0 and kdim % tk == 0
    return pl.pallas_call(
        functools.partial(_mm_acc_kernel, nk=nk),
        grid=(m // tm, n // tn, nk),
        in_specs=[pl.BlockSpec((tm, tk), lambda i, j, k: (i, k)),
                  pl.BlockSpec((None, tk, tn), lambda i, j, k: (layer, k, j))],
        out_specs=pl.BlockSpec((tm, tn), lambda i, j, k: (i, j)),
        out_shape=jax.ShapeDtypeStruct((m, n), out_dtype),
        scratch_shapes=[pltpu.VMEM((tm, tn), F32)],
        compiler_params=_cparams(("parallel", "parallel", "arbitrary")),
        name=name,
    )(x, w)


def _mm_wstat_kernel(x_ref, w_ref, o_ref, wb_ref):
    @pl.when(pl.program_id(1) == 0)
    def _():
        wb_ref[...] = w_ref[...].astype(BF16)

    o_ref[...] = _dot(x_ref[...], wb_ref[...]).astype(o_ref.dtype)


def matmul_wstat(x, w, layer, tm, tn, out_dtype=BF16, name="matmul_wstat"):
    m, kdim = x.shape
    n = w.shape[2]
    assert m % tm == 0 and n % tn == 0
    return pl.pallas_call(
        _mm_wstat_kernel,
        grid=(n // tn, m // tm),
        in_specs=[pl.BlockSpec((tm, kdim), lambda j, i: (i, 0)),
                  pl.BlockSpec((None, kdim, tn), lambda j, i: (layer, 0, j))],
        out_specs=pl.BlockSpec((tm, tn), lambda j, i: (i, j)),
        out_shape=jax.ShapeDtypeStruct((m, n), out_dtype),
        scratch_shapes=[pltpu.VMEM((kdim, tn), BF16)],
        compiler_params=_cparams(("parallel", "arbitrary")),
        name=name,
    )(x, w)


def _dsa_prep_kernel(ckv_ref, kw_ref, kvg_ref, wuk_ref, wuv_ref, lng_ref, lnb_ref,
                     k_ref, v_ref, kidx_ref, wi_ref, *, k_scale, w_scale):
    c = ckv_ref[...].astype(F32)
    c = c * lax.rsqrt(jnp.mean(c * c, axis=-1, keepdims=True) + NORM_EPS) * kvg_ref[...]
    cb = c.astype(BF16)
    k_ref[...] = (_dot(cb, wuk_ref[...]) * k_scale).astype(k_ref.dtype)
    v_ref[:, :HEAD_DIM] = _dot(cb, wuv_ref[...]).astype(v_ref.dtype)
    v_ref[:, HEAD_DIM:] = jnp.ones((v_ref.shape[0], HEAD_DIM), v_ref.dtype)
    kw = kw_ref[...].astype(F32)
    ki = kw[:, :HEAD_DIM]
    mu = jnp.mean(ki, axis=-1, keepdims=True)
    kc = ki - mu
    kn = kc * lax.rsqrt(jnp.mean(kc * kc, axis=-1, keepdims=True) + NORM_EPS)
    kidx_ref[...] = (kn * lng_ref[...] + lnb_ref[...]).astype(kidx_ref.dtype)
    wi_ref[...] = kw[:, HEAD_DIM:] * w_scale


def dsa_prep(proj, lay, kv_norm_g, w_uk, w_uv, ln_g, ln_b, idx_heads, rows=512):
    n = proj.shape[0]
    kvl = w_uk.shape[0]
    row = lambda i: (i, 0)
    const = lambda i: (0, 0)
    out = jax.ShapeDtypeStruct((n, HEAD_DIM), BF16)
    return pl.pallas_call(
        functools.partial(_dsa_prep_kernel, k_scale=HEAD_DIM ** -0.5 * LOG2E,
                          w_scale=idx_heads ** -0.5 * HEAD_DIM ** -0.5),
        grid=(n // rows,),
        in_specs=[pl.BlockSpec((rows, kvl), lambda i: (i, lay["c_kv"] // kvl)),
                  pl.BlockSpec((rows, 2 * HEAD_DIM), lambda i: (i, lay["k_i"] // (2 * HEAD_DIM))),
                  pl.BlockSpec((1, kvl), const),
                  pl.BlockSpec((kvl, HEAD_DIM), const),
                  pl.BlockSpec((kvl, HEAD_DIM), const),
                  pl.BlockSpec((1, HEAD_DIM), const),
                  pl.BlockSpec((1, HEAD_DIM), const)],
        out_specs=[pl.BlockSpec((rows, HEAD_DIM), row), pl.BlockSpec((rows, 2 * HEAD_DIM), row),
                   pl.BlockSpec((rows, HEAD_DIM), row), pl.BlockSpec((rows, HEAD_DIM), row)],
        out_shape=[out, jax.ShapeDtypeStruct((n, 2 * HEAD_DIM), BF16), out,
                   jax.ShapeDtypeStruct((n, HEAD_DIM), F32)],
        compiler_params=_cparams(("parallel",)),
        name="dsa_prep",
    )(proj, proj, kv_norm_g.reshape(1, kvl), w_uk.astype(BF16), w_uv.astype(BF16),
      ln_g.reshape(1, HEAD_DIM), ln_b.reshape(1, HEAD_DIM))


def _sortable_key(x):
    u = lax.bitcast_convert_type(x, jnp.int32)
    return u ^ ((u >> 31) & jnp.int32(0x7FFFFFFF))


def _dsa_kernel(*refs, heads, idx_heads, topk, n_qi):
    qa_ref = refs[0]
    qi_refs = refs[1:1 + n_qi]
    (wi_ref, kidx_ref, k_ref, v_ref, bias_ref, o_ref,
     qas_ref, qis_ref, keys_ref, mask_ref, m_ref, acc_ref) = refs[1 + n_qi:]
    qb = ATT_BLOCK
    ck = ATT_BLOCK
    j = pl.program_id(1)

    for h in range(heads):
        qas_ref[h * qb:(h + 1) * qb, :] = qa_ref[:, h * HEAD_DIM:(h + 1) * HEAD_DIM]
    for h in range(idx_heads):
        src, off = divmod(h * HEAD_DIM, QI_BLOCK)
        qis_ref[h * qb:(h + 1) * qb, :] = qi_refs[src][:, off:off + HEAD_DIM]

    row_pos = j * qb + lax.broadcasted_iota(jnp.int32, (qb, ck), 0)
    lane = lax.broadcasted_iota(jnp.int32, (qb, ck), 1)
    hg = 8 if idx_heads % 8 == 0 else idx_heads

    def score_chunk(c, carry):
        kc = kidx_ref[pl.ds(pl.multiple_of(c * ck, ck), ck), :]
        acc = jnp.zeros((qb, ck), F32)
        for g in range(idx_heads // hg):
            s = _dot_nt(qis_ref[g * hg * qb:(g + 1) * hg * qb, :], kc)
            for hh in range(hg):
                h = g * hg + hh
                acc = acc + jnp.maximum(s[hh * qb:(hh + 1) * qb, :], 0.0) * wi_ref[:, h:h + 1]
        valid = (c * ck + lane) <= row_pos
        keys_ref[c] = jnp.where(valid, _sortable_key(acc), jnp.int32(INT_MIN))
        return carry

    lax.fori_loop(0, j + 1, score_chunk, 0)

    def count_ge(cand):
        def count_chunk(c, part):
            ge = (keys_ref[c] >= cand).astype(jnp.int32)
            for t in range(ck // LANES):
                part = part + ge[:, t * LANES:(t + 1) * LANES]
            return part

        part = lax.fori_loop(0, j + 1, count_chunk, jnp.zeros((qb, LANES), jnp.int32))
        return jnp.sum(part, axis=-1, keepdims=True)

    def bit_step(i, carry):
        lo, n_lo = carry
        cand = lo + lax.shift_left(jnp.int32(1), jnp.int32(31) - i)
        cnt = count_ge(cand)
        keep = cnt >= topk
        return jnp.where(keep, cand, lo), jnp.where(keep, cnt, n_lo)

    lo, n_lo = lax.fori_loop(0, 32, bit_step, (jnp.full((qb, 1), INT_MIN, jnp.int32),
                                               jnp.zeros((qb, 1), jnp.int32)))
    thr = jnp.maximum(lo, jnp.int32(INT_MIN + 1))

    @pl.when(jnp.max(n_lo) > topk)
    def _():
        need = (topk - count_ge(thr + 1)).astype(F32)
        tri = (lax.broadcasted_iota(jnp.int32, (ck, ck), 0)
               <= lax.broadcasted_iota(jnp.int32, (ck, ck), 1)).astype(BF16)

        def demote(c, seen):
            key = keys_ref[c]
            tie = key == thr
            rank = seen + _dot(jnp.where(tie, 1.0, 0.0).astype(BF16), tri)
            keys_ref[c] = jnp.where(tie & (rank > need), thr - 1, key)
            return rank[:, ck - 1:ck]

        lax.fori_loop(0, j + 1, demote, jnp.zeros((qb, 1), F32))

    m_ref[...] = jnp.full_like(m_ref, M_INIT)
    acc_ref[...] = jnp.zeros_like(acc_ref)

    def attend(c, bias_idx):
        off = pl.multiple_of(c * ck, ck)
        kc = k_ref[pl.ds(off, ck), :]
        vc = v_ref[pl.ds(off, ck), :]
        mask_ref[...] = jnp.where(keys_ref[c] >= thr, 0.0, MASK_NEG)
        for h in range(heads):
            rows = slice(h * qb, (h + 1) * qb)
            s = _dot_nt(qas_ref[rows, :], kc) + mask_ref[...]
            if bias_idx is not None:
                s = s + bias_ref[bias_idx, rows, :]
            parts = [s[:, t * LANES:(t + 1) * LANES] for t in range(ck // LANES)]
            smax = functools.reduce(jnp.maximum, parts)
            m_prev = m_ref[rows, :]
            m_new = jnp.maximum(m_prev, jnp.max(smax, axis=-1, keepdims=True))
            p = jnp.concatenate([jnp.exp2(x - m_new) for x in parts], axis=1).astype(BF16)
            alpha = jnp.exp2(m_prev - m_new)
            pv = _dot(p, vc)
            acc_ref[rows, :HEAD_DIM] = alpha * acc_ref[rows, :HEAD_DIM] + pv[:, :HEAD_DIM]
            acc_ref[rows, HEAD_DIM:] = alpha * acc_ref[rows, HEAD_DIM:] + pv[:, HEAD_DIM:]
            m_ref[rows, :] = m_new

    def far_chunk(c, carry):
        attend(c, None)
        return carry

    lax.fori_loop(0, j - 1, far_chunk, 0)

    @pl.when(j >= 1)
    def _():
        attend(j - 1, 1)

    attend(j, 0)

    for h in range(heads):
        rows = slice(h * qb, (h + 1) * qb)
        o = acc_ref[rows, :HEAD_DIM] / acc_ref[rows, HEAD_DIM:]
        o_ref[:, h * HEAD_DIM:(h + 1) * HEAD_DIM] = o.astype(o_ref.dtype)


def dsa_attention(proj, lay, wi, kidx, k, v, bias_tiles, batch, seq, heads, idx_heads):
    qb = ATT_BLOCK
    nqb = seq // qb
    aw = heads * HEAD_DIM
    iw = idx_heads * HEAD_DIM
    topk = min(IDX_TOPK, seq // 4)
    assert lay["q_i"] % QI_BLOCK == 0 and iw % QI_BLOCK == 0 and lay["q_a"] % aw == 0
    n_qi = iw // QI_BLOCK
    return pl.pallas_call(
        functools.partial(_dsa_kernel, heads=heads, idx_heads=idx_heads, topk=topk, n_qi=n_qi),
        grid=(batch, nqb),
        in_specs=[pl.BlockSpec((qb, aw), lambda b, j: (b * nqb + j, lay["q_a"] // aw)),
                  *[pl.BlockSpec((qb, QI_BLOCK), lambda b, j, t=t: (b * nqb + j, lay["q_i"] // QI_BLOCK + t))
                    for t in range(n_qi)],
                  pl.BlockSpec((qb, HEAD_DIM), lambda b, j: (b * nqb + j, 0)),
                  pl.BlockSpec((seq, HEAD_DIM), lambda b, j: (b, 0)),
                  pl.BlockSpec((seq, HEAD_DIM), lambda b, j: (b, 0)),
                  pl.BlockSpec((seq, 2 * HEAD_DIM), lambda b, j: (b, 0)),
                  pl.BlockSpec((2, heads * qb, qb), lambda b, j: (0, 0, 0),
                               pipeline_mode=pl.Buffered(1))],
        out_specs=pl.BlockSpec((qb, aw), lambda b, j: (b * nqb + j, 0)),
        out_shape=jax.ShapeDtypeStruct((batch * seq, aw), BF16),
        scratch_shapes=[pltpu.VMEM((heads * qb, HEAD_DIM), BF16),
                        pltpu.VMEM((idx_heads * qb, HEAD_DIM), BF16),
                        pltpu.VMEM((seq // qb, qb, qb), jnp.int32),
                        pltpu.VMEM((qb, qb), F32),
                        pltpu.VMEM((heads * qb, LANES), F32),
                        pltpu.VMEM((heads * qb, 2 * HEAD_DIM), F32)],
        compiler_params=_cparams(("parallel", "arbitrary")),
        name="dsa_attention",
    )(proj, *([proj] * n_qi), wi, kidx, k, v, bias_tiles)


def _t5_bucket(dist):
    max_exact = REL_BUCKETS // 2
    d = jnp.maximum(dist, 0)
    df = jnp.maximum(d, 1).astype(F32)
    large = max_exact + (jnp.log(df / max_exact) / math.log(REL_MAX_DIST / max_exact)
                         * (REL_BUCKETS - max_exact)).astype(jnp.int32)
    large = jnp.minimum(large, REL_BUCKETS - 1)
    return jnp.where(d < max_exact, d, large)


def rel_bias_tiles(rel_bias, heads):
    qb = ATT_BLOCK
    assert _static_far_bucket(qb) == REL_BUCKETS - 1
    tbl = (rel_bias[_t5_bucket(jnp.arange(4 * qb))] - rel_bias[REL_BUCKETS - 1][None, :]).T * LOG2E
    y = jnp.arange(2 * qb)
    tiles = []
    for off in (0, qb):
        d = off + jnp.where(y < qb, -y, 2 * qb - y)
        u = tbl[:, jnp.maximum(d, 0)]
        flat = jnp.broadcast_to(u[:, None, :], (heads, qb, 2 * qb)).reshape(heads, 2 * qb * qb)
        skew = flat[:, :qb * (2 * qb - 1)].reshape(heads, qb, 2 * qb - 1)
        tiles.append(skew[:, :, :qb].reshape(heads * qb, qb))
    return jnp.stack(tiles).astype(F32)


def _static_far_bucket(d):
    max_exact = REL_BUCKETS // 2
    v = max_exact + int(math.log(d / max_exact) / math.log(REL_MAX_DIST / max_exact)
                        * (REL_BUCKETS - max_exact))
    return min(v, REL_BUCKETS - 1)


def _hgrn_kernel(q_ref, f_ref, i_ref, g_ref, lb_ref, ng_ref, o_ref,
                 st_ref, a_ref, k_ref, qf_ref, p_ref, *, heads_per_step, tb):
    c = HGRN_CHUNK
    sub = HGRN_SUB
    nsub = c // sub
    hs = range(heads_per_step)

    @pl.when(pl.program_id(2) == 0)
    def _():
        st_ref[...] = jnp.zeros_like(st_ref)

    r_io = lax.broadcasted_iota(jnp.int32, (c, c), 0)
    c_io = lax.broadcasted_iota(jnp.int32, (c, c), 1)
    tri = (c_io <= r_io).astype(F32)
    sub_r = lax.broadcasted_iota(jnp.int32, (sub, c), 0)
    sub_c = lax.broadcasted_iota(jnp.int32, (sub, c), 1)
    ng = ng_ref[...]

    def a_start(hh, sb):
        return a_ref[hh, sb * sub - 1:sb * sub, :] if sb > 0 else jnp.zeros((1, HEAD_DIM), F32)

    def chunk(ci, carry):
        r0 = pl.multiple_of(ci * c, c)

        worst = jnp.zeros((1, HEAD_DIM), F32)
        for hh in hs:
            cols = slice(hh * HEAD_DIM, (hh + 1) * HEAD_DIM)
            lb = lb_ref[:, cols]
            z = f_ref[pl.ds(r0, c), cols].astype(F32)
            qx = q_ref[pl.ds(r0, c), cols].astype(F32)
            sig = jax.nn.sigmoid(z)
            lf = jnp.log2(lb + (1.0 - lb) * sig)
            k_ref[hh] = (1.0 - lb) * jax.nn.sigmoid(-z)
            qf_ref[hh] = qx * jax.nn.sigmoid(qx)
            a = jnp.dot(tri, lf, preferred_element_type=F32, precision=lax.Precision.HIGHEST)
            a_ref[hh] = a
            for sb in range(nsub):
                top = a[sb * sub - 1:sb * sub, :] if sb > 0 else jnp.zeros((1, HEAD_DIM), F32)
                worst = jnp.maximum(worst, top - a[(sb + 1) * sub - 1:(sb + 1) * sub, :])
        safe = jnp.max(worst) <= HGRN_SAFE_LOG2

        @pl.when(safe)
        def _():
            for hh in hs:
                a = a_ref[hh]
                kf = k_ref[hh]
                for sb in range(nsub):
                    rs = slice(sb * sub, (sb + 1) * sub)
                    top = a_start(hh, sb)
                    q_t = qf_ref[hh, rs, :] * jnp.exp2(a_ref[hh, rs, :] - top)
                    k_t = kf * jnp.exp2(top - a)
                    p = _dot_nt(q_t.astype(BF16), k_t.astype(BF16))
                    p_ref[hh, rs, :] = jnp.where(sub_c <= sub_r + sb * sub, p, 0.0)

        @pl.when(jnp.logical_not(safe))
        def _():
            for hh in hs:
                a = a_ref[hh]
                kf = k_ref[hh]
                for sb in range(nsub):
                    rs = slice(sb * sub, (sb + 1) * sub)
                    top = a_start(hh, sb)
                    a_blk = a_ref[hh, rs, :]
                    q_blk = qf_ref[hh, rs, :]
                    q_t = q_blk * jnp.exp2(a_blk - top)
                    k_t = kf * jnp.exp2(top - a)
                    p_off = _dot_nt(q_t.astype(BF16), k_t.astype(BF16))
                    p_diag = jnp.zeros((sub, c), F32)
                    for s in range(sub):
                        row = sb * sub + s
                        e = jnp.exp2(a_blk - a_ref[hh, row:row + 1, :])
                        col = jnp.sum(q_blk * e * k_ref[hh, row:row + 1, :], axis=-1, keepdims=True)
                        p_diag = jnp.where((sub_c == row) & (sub_r >= s), col, p_diag)
                    p_ref[hh, rs, :] = jnp.where(sub_c < sb * sub, p_off, p_diag)

        for hh in hs:
            cols = slice(hh * HEAD_DIM, (hh + 1) * HEAD_DIM)
            a = a_ref[hh]
            kf = k_ref[hh]
            v = i_ref[pl.ds(r0, c), cols].astype(F32)
            gx = g_ref[pl.ds(r0, c), cols].astype(F32)
            vb = v.astype(BF16)
            st = st_ref[hh]
            o = (_dot(p_ref[hh].astype(BF16), vb)
                 + _dot_nt((qf_ref[hh] * jnp.exp2(a)).astype(BF16), st.astype(BF16)))
            a_last = a[c - 1:c, :]
            k_end = kf * jnp.exp2(a_last - a)
            st_ref[hh] = st * jnp.exp2(a_last) + _dot(v.T.astype(BF16), k_end.astype(BF16))
            o = o * lax.rsqrt(jnp.mean(o * o, axis=-1, keepdims=True) + NORM_EPS) * ng
            o_ref[pl.ds(r0, c), cols] = (o * (gx * jax.nn.sigmoid(gx))).astype(o_ref.dtype)
        return carry

    lax.fori_loop(0, tb // c, chunk, 0)


def hgrn2(proj, lay, lb, norm_g, batch, seq, heads, heads_per_step=8, tb=512):
    heads_per_step = min(heads_per_step, heads)
    assert heads % heads_per_step == 0
    hw = heads_per_step * HEAD_DIM
    nt = seq // tb
    ng = heads // heads_per_step

    def col(name):
        base = lay[name] // hw
        return lambda b, h, t: (b * nt + t, base + h)

    return pl.pallas_call(
        functools.partial(_hgrn_kernel, heads_per_step=heads_per_step, tb=tb),
        grid=(batch, ng, nt),
        in_specs=[pl.BlockSpec((tb, hw), col("q_h")),
                  pl.BlockSpec((tb, hw), col("f_h")),
                  pl.BlockSpec((tb, hw), col("i_h")),
                  pl.BlockSpec((tb, hw), col("g_h")),
                  pl.BlockSpec((1, hw), lambda b, h, t: (0, h)),
                  pl.BlockSpec((1, HEAD_DIM), lambda b, h, t: (0, 0))],
        out_specs=pl.BlockSpec((tb, hw), lambda b, h, t: (b * nt + t, h)),
        out_shape=jax.ShapeDtypeStruct((batch * seq, heads * HEAD_DIM), BF16),
        scratch_shapes=[pltpu.VMEM((heads_per_step, HEAD_DIM, HEAD_DIM), F32),
                        pltpu.VMEM((heads_per_step, HGRN_CHUNK, HEAD_DIM), F32),
                        pltpu.VMEM((heads_per_step, HGRN_CHUNK, HEAD_DIM), F32),
                        pltpu.VMEM((heads_per_step, HGRN_CHUNK, HEAD_DIM), F32),
                        pltpu.VMEM((heads_per_step, HGRN_CHUNK, HGRN_CHUNK), F32)],
        compiler_params=_cparams(("parallel", "parallel", "arbitrary")),
        name="hgrn2",
    )(proj, proj, proj, proj, lb.reshape(1, -1), norm_g.reshape(1, HEAD_DIM))


def _merge_kernel(a_ref, b_ref, wa_ref, wb_ref, ga_ref, gb_ref, o_ref, wab_ref, wbb_ref):
    @pl.when(pl.program_id(1) == 0)
    def _():
        wab_ref[...] = wa_ref[...].astype(BF16)
        wbb_ref[...] = wb_ref[...].astype(BF16)

    ya = _dot(a_ref[...], wab_ref[...])
    yb = _dot(b_ref[...], wbb_ref[...])
    ga = jax.nn.sigmoid(ga_ref[...].astype(F32))
    gb = jax.nn.sigmoid(gb_ref[...].astype(F32))
    o_ref[...] = (ga * ya + gb * yb).astype(o_ref.dtype)


def gated_merge(attn, hg, w_pa, w_pb, layer, proj, lay, tm=1024, tn=512):
    m, ka = attn.shape
    kb = hg.shape[1]
    d = w_pa.shape[2]
    assert lay["gate_a"] % tn == 0 and lay["gate_b"] % tn == 0
    ga0, gb0 = lay["gate_a"] // tn, lay["gate_b"] // tn
    return pl.pallas_call(
        _merge_kernel,
        grid=(d // tn, m // tm),
        in_specs=[pl.BlockSpec((tm, ka), lambda j, i: (i, 0)),
                  pl.BlockSpec((tm, kb), lambda j, i: (i, 0)),
                  pl.BlockSpec((None, ka, tn), lambda j, i: (layer, 0, j)),
                  pl.BlockSpec((None, kb, tn), lambda j, i: (layer, 0, j)),
                  pl.BlockSpec((tm, tn), lambda j, i: (i, ga0 + j)),
                  pl.BlockSpec((tm, tn), lambda j, i: (i, gb0 + j))],
        out_specs=pl.BlockSpec((tm, tn), lambda j, i: (i, j)),
        out_shape=jax.ShapeDtypeStruct((m, d), BF16),
        scratch_shapes=[pltpu.VMEM((ka, tn), BF16), pltpu.VMEM((kb, tn), BF16)],
        compiler_params=_cparams(("parallel", "arbitrary")),
        name="gated_merge",
    )(attn, hg, w_pa, w_pb, proj, proj)


def _norm_residual_kernel(x_ref, y_ref, gp_ref, gn_ref, xo_ref, ho_ref):
    y = y_ref[...].astype(F32)
    yn = y * lax.rsqrt(jnp.mean(y * y, axis=-1, keepdims=True) + NORM_EPS) * gp_ref[...]
    x = x_ref[...] + yn
    xo_ref[...] = x
    ho_ref[...] = (x * lax.rsqrt(jnp.mean(x * x, axis=-1, keepdims=True) + NORM_EPS)
                   * gn_ref[...]).astype(ho_ref.dtype)


def _norm_residual_last_kernel(x_ref, y_ref, gp_ref, xo_ref):
    y = y_ref[...].astype(F32)
    xo_ref[...] = x_ref[...] + y * lax.rsqrt(jnp.mean(y * y, axis=-1, keepdims=True) + NORM_EPS) * gp_ref[...]


def norm_residual(x, y, g_post, g_next=None, rows=256):
    n, d = x.shape
    row = pl.BlockSpec((rows, d), lambda i: (i, 0))
    vec = pl.BlockSpec((1, d), lambda i: (0, 0))
    if g_next is None:
        return pl.pallas_call(
            _norm_residual_last_kernel,
            grid=(n // rows,),
            in_specs=[row, row, vec],
            out_specs=row,
            out_shape=jax.ShapeDtypeStruct((n, d), F32),
            compiler_params=_cparams(("parallel",)),
            name="norm_residual_last",
        )(x, y, g_post.reshape(1, d)), None
    return pl.pallas_call(
        _norm_residual_kernel,
        grid=(n // rows,),
        in_specs=[row, row, vec, vec],
        out_specs=[row, row],
        out_shape=[jax.ShapeDtypeStruct((n, d), F32), jax.ShapeDtypeStruct((n, d), BF16)],
        compiler_params=_cparams(("parallel",)),
        name="norm_residual",
    )(x, y, g_post.reshape(1, d), g_next.reshape(1, d))


GELU_C = math.sqrt(2.0 / math.pi)


def _gelu_tanh(x):
    z = x * (x * x * (-2.0 * GELU_C * 0.044715 * LOG2E) + (-2.0 * GELU_C * LOG2E))
    return x / (1.0 + jnp.exp2(z))


def _ffn_up_kernel(x_ref, wg_ref, wv_ref, cwg_ref, cwv_ref, cbg_ref, cbv_ref, o_ref,
                   wb_ref, u_ref, *, tm, tn, seq):
    i = pl.program_id(1)
    starts_seq = (i * tm) % seq == 0

    @pl.when(i == 0)
    def _():
        wb_ref[:, :tn] = wg_ref[...].astype(BF16)
        wb_ref[:, tn:] = wv_ref[...].astype(BF16)

    @pl.when(starts_seq)
    def _():
        u_ref[0:8, :] = jnp.zeros((8, 2 * tn), F32)

    @pl.when(jnp.logical_not(starts_seq))
    def _():
        u_ref[0:8, :] = u_ref[tm:tm + 8, :]

    u_ref[8:8 + tm, :] = _dot(x_ref[...], wb_ref[...])

    def conv(lo, w_ref, b_ref):
        cols = slice(lo, lo + tn)
        return (b_ref[...] + w_ref[0:1, :] * u_ref[6:6 + tm, cols]
                + w_ref[1:2, :] * u_ref[7:7 + tm, cols] + w_ref[2:3, :] * u_ref[8:8 + tm, cols])

    gate = conv(0, cwg_ref, cbg_ref)
    val = conv(tn, cwv_ref, cbv_ref)
    o_ref[...] = (_gelu_tanh(gate) * val).astype(o_ref.dtype)


def ffn_up_conv_geglu(h, w_up, conv_w, conv_b, layer, seq, tm, tn):
    m, kdim = h.shape
    dff = w_up.shape[2] // 2
    assert m % tm == 0 and dff % tn == 0 and seq % tm == 0
    nc = dff // tn
    wspec = lambda off: pl.BlockSpec((None, kdim, tn), lambda j, i: (layer, 0, off + j))
    cspec = lambda r, off: pl.BlockSpec((None, r, tn), lambda j, i: (layer, 0, off + j))
    return pl.pallas_call(
        functools.partial(_ffn_up_kernel, tm=tm, tn=tn, seq=seq),
        grid=(nc, m // tm),
        in_specs=[pl.BlockSpec((tm, kdim), lambda j, i: (i, 0)),
                  wspec(0), wspec(nc),
                  cspec(CONV_WIDTH, 0), cspec(CONV_WIDTH, nc),
                  cspec(1, 0), cspec(1, nc)],
        out_specs=pl.BlockSpec((tm, tn), lambda j, i: (i, j)),
        out_shape=jax.ShapeDtypeStruct((m, dff), BF16),
        scratch_shapes=[pltpu.VMEM((kdim, 2 * tn), BF16), pltpu.VMEM((tm + 8, 2 * tn), F32)],
        compiler_params=_cparams(("parallel", "arbitrary")),
        name="ffn_up",
    )(h, w_up, w_up, conv_w, conv_w, conv_b[:, None, :], conv_b[:, None, :])


def _in_layout(d, aw, iw, ih, hk, hv, kvl):
    segs = [("q_a", aw), ("c_kv", kvl), ("q_i", iw), ("k_i", HEAD_DIM), ("w_i", ih), ("pad", None),
            ("q_h", hk), ("f_h", hk), ("i_h", hv), ("g_h", hv), ("gate_a", d), ("gate_b", d)]
    lay, off = {}, 0
    for name, w in segs:
        if name == "pad":
            w = (-off) % IN_PAD_ALIGN
            lay["pad_width"] = w
        lay[name] = off
        off += w
    assert off % IN_PROJ_TN == 0
    lay["total"] = off
    return lay


def _pad_cast_kernel(src_ref, o_ref, prev_ref, *, n_pad, n_tail, off, keep):
    n = pl.program_id(0)
    tn = o_ref.shape[1]

    @pl.when(n < n_pad)
    def _():
        o_ref[...] = src_ref[...].T.astype(BF16)

    @pl.when(n == n_pad)
    def _():
        row = lax.broadcasted_iota(jnp.int32, (tn, 1), 0)
        o_ref[...] = jnp.where(row < keep, src_ref[...], 0.0).T.astype(BF16)
        prev_ref[...] = src_ref[...]

    @pl.when((n > n_pad) & (n < n_tail))
    def _():
        o_ref[...] = jnp.zeros_like(o_ref)

    @pl.when(n >= n_tail)
    def _():
        o_ref[:, :tn - off] = prev_ref[off:, :].T.astype(BF16)
        o_ref[:, tn - off:] = src_ref[:off, :].T.astype(BF16)
        prev_ref[...] = src_ref[...]


def pad_cast_w_in(w_in_t, layer, lay, ih):
    n_src, kdim = w_in_t.shape[1], w_in_t.shape[2]
    tn = IN_PROJ_TN
    assert lay["k_i"] % tn == 0 and lay["q_h"] % tn == 0 and ih % 8 == 0
    n_pad = lay["k_i"] // tn
    n_tail = lay["q_h"] // tn
    back = -(-lay["pad_width"] // tn)
    off = back * tn - lay["pad_width"]
    assert 0 < off < tn and off % 8 == 0 and n_tail - back == n_pad
    last = (n_src - 1) // tn

    def index(n):
        blk = jnp.where(n <= n_pad, n, n - back + 1)
        return (layer, jnp.clip(blk, 0, last), 0)

    return pl.pallas_call(
        functools.partial(_pad_cast_kernel, n_pad=n_pad, n_tail=n_tail, off=off, keep=HEAD_DIM + ih),
        grid=(lay["total"] // tn,),
        in_specs=[pl.BlockSpec((None, tn, kdim), index)],
        out_specs=pl.BlockSpec((kdim, tn), lambda n: (0, n)),
        out_shape=jax.ShapeDtypeStruct((kdim, lay["total"]), BF16),
        scratch_shapes=[pltpu.VMEM((tn, kdim), F32)],
        compiler_params=_cparams(("arbitrary",)),
        name="pad_cast_w_in",
    )(w_in_t)


def _pick(n, cands):
    for c in cands:
        if n % c == 0:
            return c
    raise ValueError(f"no tile for {n}")


def kernel(x, rel_bias, hgrn_lb_logits, mix_pre_g, mix_post_g, w_in, kv_norm_g, w_uk, w_uv, idx_k_ln_g, idx_k_ln_b, hgrn_norm_g, w_proj_attn, w_proj_hgrn, w_out, ffn_pre_g, ffn_post_g, w_up, conv_w, conv_b, w_down):
    batch, seq, d = x.shape
    depth = w_in.shape[0]
    aw = w_proj_attn.shape[1]
    hv = w_proj_hgrn.shape[1]
    hk = hgrn_lb_logits.shape[1]
    kvl = w_uk.shape[1]
    dff = w_down.shape[1]
    heads = aw // HEAD_DIM
    hheads = hv // HEAD_DIM
    ih = w_in.shape[2] - (aw + kvl + HEAD_DIM + 2 * hk + 2 * hv + 2 * d)
    ih = ih // (HEAD_DIM + 1)
    iw = ih * HEAD_DIM
    lay = _in_layout(d, aw, iw, ih, hk, hv, kvl)
    n = batch * seq

    lb_p = jax.nn.softmax(hgrn_lb_logits.astype(F32), axis=0)
    lb_all = jnp.cumsum(lb_p, axis=0) - lb_p[0:1]
    bias_tiles = rel_bias_tiles(rel_bias, heads)

    tm = _pick(n, (1024, 512, 256))
    w_in_t = jnp.swapaxes(w_in, 1, 2)
    w_down_b = w_down.astype(BF16)
    xf = x.reshape(n, d)
    h = rmsnorm_cast(xf, mix_pre_g[0])
    for l in range(depth):
        w_in_l = pad_cast_w_in(w_in_t, l, lay, ih)
        proj = matmul(h, w_in_l, _pick(n, (2 * tm, tm)), IN_PROJ_TN, name="in_proj")
        k, v, kidx, wi = dsa_prep(proj, lay, kv_norm_g[l], w_uk[l], w_uv[l],
                                  idx_k_ln_g[l], idx_k_ln_b[l], ih)
        attn = dsa_attention(proj, lay, wi, kidx, k, v, bias_tiles, batch, seq, heads, ih)
        hg = hgrn2(proj, lay, lb_all[l], hgrn_norm_g[l], batch, seq, hheads)
        merged = gated_merge(attn, hg, w_proj_attn, w_proj_hgrn, l, proj, lay, tm=tm)
        y = matmul_wstat(merged, w_out, l, tm, 512, name="out_proj")
        xf, h = norm_residual(xf, y, mix_post_g[l], ffn_pre_g[l])
        act = ffn_up_conv_geglu(h, w_up, conv_w, conv_b, l, seq, tm, _pick(dff, (256, 128)))
        y = matmul_ksplit(act, w_down_b, l, tm, 512, _pick(dff, (dff // 2, dff)), name="ffn_down")
        xf, h = norm_residual(xf, y, ffn_post_g[l], mix_pre_g[l + 1] if l + 1 < depth else None)
    return xf.reshape(batch, seq, d)
```
